```python
import math
import jax, jax.numpy as jnp
from jax import lax
import numpy as np

D_MODEL = 1024
BATCH = 2
SEQ = 16384
DEPTH = 2

CHUNK = 64
Q_BLOCK = 128
DA_HEADS = 6
DA_HEAD_DIM = 32
DA_V_DIM = 2 * DA_HEAD_DIM
DA_QK = DA_HEADS * 2 * DA_HEAD_DIM
DA_WIDTH = DA_HEADS * DA_V_DIM
GDN_HEADS = 6
GDN_HEAD_DIM = 64
GDN_WIDTH = GDN_HEADS * GDN_HEAD_DIM
CONV_K = 4
S5_GROUP_DIM = 16
S5_GROUPS = 16
S5_WIDTH = S5_GROUPS * S5_GROUP_DIM
S5_STATE = 64
D_MIX = DA_WIDTH + GDN_WIDTH + S5_WIDTH
N_IN = 2 * DA_QK + DA_WIDTH + 4 * GDN_WIDTH + 2 * GDN_HEADS + S5_WIDTH
N_EXPERT_GROUPS = 4
EXPERTS_PER_GROUP = 4
N_EXPERTS = N_EXPERT_GROUPS * EXPERTS_PER_GROUP
TOP_K_INNER = 2
D_EXPERT = 512
ALPHA = (2 * DEPTH) ** 0.25
DEEPNORM_BETA = (8 * DEPTH) ** -0.25
LN_EPS = 1e-5
RMS_EPS = 1e-6

kernel_name = 'hybrid_diffattn_gdn_s5_hmoe_deepnorm'


def split_cols(t, sizes):
    outs, start = [], 0
    for s in sizes:
        outs.append(t[..., start:start + s])
        start += s
    return outs


def layer_norm(x, g, b):
    xf = x.astype(jnp.float32)
    mu = jnp.mean(xf, -1, keepdims=True)
    var = jnp.mean(jnp.square(xf - mu), -1, keepdims=True)
    y = (xf - mu) * lax.rsqrt(var + LN_EPS) * g.astype(jnp.float32) + b.astype(jnp.float32)
    return y.astype(x.dtype)


def rms_norm(x, g):
    xf = x.astype(jnp.float32)
    y = xf * lax.rsqrt(jnp.mean(jnp.square(xf), -1, keepdims=True) + RMS_EPS)
    return (y * g.astype(jnp.float32)).astype(x.dtype)


def l2_normalize(x):
    return x * lax.rsqrt(jnp.sum(jnp.square(x), -1, keepdims=True) + RMS_EPS)


def causal_depthwise_conv(x, w):
    K, C = w.shape
    return lax.conv_general_dilated(x, w[:, None, :], window_strides=(1,), padding=[(K - 1, 0)],
                                    dimension_numbers=('NWC', 'WIO', 'NWC'), feature_group_count=C)


def diff_attention(q, k, v, lam, norm_g, lam_init):
    Bsz, S, H, _, dh = q.shape
    dv = v.shape[-1]
    nb = S // Q_BLOCK
    qb = jnp.swapaxes(q.reshape(Bsz, nb, Q_BLOCK, H, 2, dh), 0, 1)
    k_chunk = jnp.arange(S) // CHUNK
    vf = v.astype(jnp.float32)
    scale = dh ** -0.5

    def block(args):
        qi, i = args
        s = jnp.einsum('bqhcd,bkhcd->bhcqk', qi, k).astype(jnp.float32) * scale
        q_chunk = (i * Q_BLOCK + jnp.arange(Q_BLOCK)) // CHUNK
        mask = k_chunk[None, :] <= q_chunk[:, None]
        p = jax.nn.softmax(jnp.where(mask, s, -jnp.inf), axis=-1)
        a = p[:, :, 0] - lam * p[:, :, 1]
        return jnp.einsum('bhqk,bkhe->bqhe', a, vf)

    o = lax.map(block, (qb, jnp.arange(nb)))
    o = jnp.swapaxes(o, 0, 1).reshape(Bsz, S, H, dv)
    o = rms_norm(o, norm_g) * (1.0 - lam_init)
    return o.reshape(Bsz, S, H * dv).astype(v.dtype)


def chunked_gated_delta_rule(q, k, v, beta, g):
    Bsz, S, H, dk = q.shape
    dv = v.shape[-1]
    n, C = S // CHUNK, CHUNK

    def chunks(t):
        return t.reshape(Bsz, n, C, H, -1).transpose(0, 3, 1, 2, 4)

    q = chunks(q) * dk ** -0.5
    k = chunks(k)
    v = chunks(v)
    beta = beta.reshape(Bsz, n, C, H).transpose(0, 3, 1, 2)
    gc = jnp.cumsum(g.reshape(Bsz, n, C, H).transpose(0, 3, 1, 2), axis=-1)
    tri = jnp.tril(jnp.ones((C, C), bool))
    strict = jnp.tril(jnp.ones((C, C), bool), -1)
    decay = jnp.exp(jnp.where(tri, gc[..., :, None] - gc[..., None, :], -jnp.inf))
    kb = k * beta[..., None]
    l_mat = jnp.where(strict, jnp.einsum('bhnid,bhnjd->bhnij', kb, k) * decay, 0.0)
    rhs = jnp.concatenate([v * beta[..., None], kb * jnp.exp(gc)[..., None]], -1)
    sol = lax.linalg.triangular_solve(l_mat + jnp.eye(C, dtype=l_mat.dtype), rhs,
                                      left_side=True, lower=True)
    u, w = sol[..., :dv], sol[..., dv:]
    a_intra = jnp.einsum('bhnid,bhnjd->bhnij', q, k) * decay
    q_dec = q * jnp.exp(gc)[..., None]
    g_last = gc[..., -1]
    k_dec = k * jnp.exp(g_last[..., None] - gc)[..., None]

    def step(state, inp):
        qd, a, uu, ww, kd, gl = inp
        v_new = uu - jnp.einsum('bhcd,bhde->bhce', ww, state)
        o = jnp.einsum('bhcd,bhde->bhce', qd, state) + jnp.einsum('bhij,bhje->bhie', a, v_new)
        state = state * jnp.exp(gl)[..., None, None] + jnp.einsum('bhcd,bhce->bhde', kd, v_new)
        return state, o

    xs = tuple(jnp.moveaxis(t, 2, 0) for t in (q_dec, a_intra, u, w, k_dec, g_last))
    s0 = jnp.zeros((Bsz, H, dk, dv), jnp.float32)
    _, o = lax.scan(step, s0, xs)
    return o.transpose(1, 0, 3, 2, 4).reshape(Bsz, S, H, dv)


def gdn_mixer(qkv, gate, beta_raw, a_raw, conv_w, a_log, dt_bias, norm_g):
    Bsz, S, _ = qkv.shape
    qkv = jax.nn.silu(causal_depthwise_conv(qkv, conv_w)).astype(jnp.float32)
    q, k, v = split_cols(qkv, (GDN_WIDTH, GDN_WIDTH, GDN_WIDTH))
    hs = (Bsz, S, GDN_HEADS, GDN_HEAD_DIM)
    q = l2_normalize(q.reshape(hs))
    k = l2_normalize(k.reshape(hs))
    v = v.reshape(hs)
    beta = jax.nn.sigmoid(beta_raw.astype(jnp.float32))
    g = -jnp.exp(a_log.astype(jnp.float32)) * jax.nn.softplus(
        a_raw.astype(jnp.float32) + dt_bias.astype(jnp.float32))
    o = chunked_gated_delta_rule(q, k, v, beta, g)
    o = rms_norm(o, norm_g) * jax.nn.silu(gate.astype(jnp.float32)).reshape(hs)
    return o.reshape(Bsz, S, GDN_WIDTH).astype(gate.dtype)


def _complex_affine_combine(e1, e2):
    a1r, a1i, b1r, b1i = e1
    a2r, a2i, b2r, b2i = e2
    return (a2r * a1r - a2i * a1i,
            a2r * a1i + a2i * a1r,
            a2r * b1r - a2i * b1i + b2r,
            a2r * b1i + a2i * b1r + b2i)


def s5_mixer(u, lam_re, lam_im, log_dt, b_re, b_im, c_re, c_im, d, w_glu):
    f32 = jnp.float32
    Bsz, S, _ = u.shape
    uf = u.astype(f32)
    ug = uf.reshape(Bsz, S, S5_GROUPS, S5_GROUP_DIM)
    lre, lim = lam_re.astype(f32), lam_im.astype(f32)
    dt = jnp.exp(log_dt.astype(f32))[:, None]
    mag = jnp.exp(lre * dt)
    ab_re, ab_im = mag * jnp.cos(lim * dt), mag * jnp.sin(lim * dt)
    num_re, num_im = ab_re - 1.0, ab_im
    den = lre * lre + lim * lim
    coef_re = (num_re * lre + num_im * lim) / den
    coef_im = (num_im * lre - num_re * lim) / den
    br, bi = b_re.astype(f32), b_im.astype(f32)
    bb_re = coef_re[..., None] * br - coef_im[..., None] * bi
    bb_im = coef_re[..., None] * bi + coef_im[..., None] * br
    bu_re = jnp.einsum('bsgh,gph->bsgp', ug, bb_re)
    bu_im = jnp.einsum('bsgh,gph->bsgp', ug, bb_im)
    a_re = jnp.broadcast_to(ab_re, bu_re.shape)
    a_im = jnp.broadcast_to(ab_im, bu_re.shape)
    _, _, x_re, x_im = lax.associative_scan(_complex_affine_combine, (a_re, a_im, bu_re, bu_im), axis=1)
    y = (jnp.einsum('ghp,bsgp->bsgh', c_re.astype(f32), x_re)
         - jnp.einsum('ghp,bsgp->bsgh', c_im.astype(f32), x_im))
    y = y.reshape(Bsz, S, S5_WIDTH) + d.astype(f32) * uf
    y = jax.nn.gelu(y)
    y = y * jax.nn.sigmoid(y @ w_glu.astype(f32))
    return y.astype(u.dtype)


def hier_moe(h, w_grp, b_grp, w_exp, b_exp, w1, w3, w2):
    Bsz, S, D = h.shape
    hf = h.reshape(-1, D)
    t = hf.shape[0]
    g_prob = jax.nn.softmax((hf @ w_grp + b_grp).astype(jnp.float32), axis=-1)
    g_p, g_idx = lax.top_k(g_prob, 1)
    g_onehot = jax.nn.one_hot(g_idx[:, 0], N_EXPERT_GROUPS, dtype=jnp.float32)
    e_logits = (hf @ w_exp + b_exp).astype(jnp.float32).reshape(t, N_EXPERT_GROUPS, EXPERTS_PER_GROUP)
    e_sel = jnp.sum(e_logits * g_onehot[:, :, None], axis=1)
    e_val, e_idx = lax.top_k(e_sel, TOP_K_INNER)
    e_w = jax.nn.softmax(e_val, axis=-1) * g_p
    glob = g_idx * EXPERTS_PER_GROUP + e_idx
    comb = jnp.sum(jax.nn.one_hot(glob, N_EXPERTS, dtype=jnp.float32) * e_w[..., None], axis=1)
    out = jnp.zeros((t, D), jnp.float32)
    for e in range(N_EXPERTS):
        hid = jax.nn.silu(hf @ w1[e]) * (hf @ w3[e])
        out = out + comb[:, e:e + 1] * (hid @ w2[e]).astype(jnp.float32)
    return out.reshape(Bsz, S, D).astype(h.dtype)


def setup_inputs(seed: int = 0) -> dict:
    key = jax.random.key(seed)
    ks = iter(jax.random.split(key, 48))
    f32 = jnp.float32
    L = DEPTH

    def nrm(shape, scale):
        return jax.random.normal(next(ks), shape, f32) * scale

    def unif(shape, lo, hi):
        return jax.random.uniform(next(ks), shape, f32, lo, hi)

    x = nrm((BATCH, SEQ, D_MODEL), 1.0)
    ln_in_g = 1.0 + nrm((D_MODEL,), 0.02)
    ln_in_b = nrm((D_MODEL,), 0.02)
    w_in = nrm((L, D_MODEL, N_IN), D_MODEL ** -0.5)
    w_out = nrm((L, D_MIX, D_MODEL), D_MIX ** -0.5 * DEEPNORM_BETA)
    lam_q1 = nrm((L, DA_HEAD_DIM), 0.1)
    lam_k1 = nrm((L, DA_HEAD_DIM), 0.1)
    lam_q2 = nrm((L, DA_HEAD_DIM), 0.1)
    lam_k2 = nrm((L, DA_HEAD_DIM), 0.1)
    diff_norm_g = 1.0 + nrm((L, DA_V_DIM), 0.02)
    dn_conv_w = nrm((L, CONV_K, 3 * GDN_WIDTH), CONV_K ** -0.5)
    dn_a_log = jnp.log(unif((L, GDN_HEADS), 1.0, 16.0))
    dt0 = jnp.exp(unif((L, GDN_HEADS), math.log(1e-3), math.log(1e-1)))
    dn_dt_bias = dt0 + jnp.log(-jnp.expm1(-dt0))
    dn_norm_g = 1.0 + nrm((L, GDN_HEAD_DIM), 0.02)
    n_idx = jnp.arange(S5_STATE, dtype=f32)
    s5_lambda_re = -0.5 + nrm((L, S5_GROUPS, S5_STATE), 0.01)
    s5_lambda_im = math.pi * n_idx + nrm((L, S5_GROUPS, S5_STATE), 0.01)
    s5_log_dt = unif((L, S5_GROUPS), math.log(1e-3), math.log(1e-1))
    s5_b_re = nrm((L, S5_GROUPS, S5_STATE, S5_GROUP_DIM), (2 * S5_GROUP_DIM) ** -0.5)
    s5_b_im = nrm((L, S5_GROUPS, S5_STATE, S5_GROUP_DIM), (2 * S5_GROUP_DIM) ** -0.5)
    s5_c_re = nrm((L, S5_GROUPS, S5_GROUP_DIM, S5_STATE), S5_STATE ** -0.5)
    s5_c_im = nrm((L, S5_GROUPS, S5_GROUP_DIM, S5_STATE), S5_STATE ** -0.5)
    s5_d = nrm((L, S5_WIDTH), 1.0)
    s5_w_glu = nrm((L, S5_WIDTH, S5_WIDTH), S5_WIDTH ** -0.5)
    ln1_g = 1.0 + nrm((L, D_MODEL), 0.02)
    ln1_b = nrm((L, D_MODEL), 0.02)
    moe_w_grp = nrm((L, D_MODEL, N_EXPERT_GROUPS), D_MODEL ** -0.5)
    moe_b_grp = nrm((L, N_EXPERT_GROUPS), 0.01)
    moe_w_exp = nrm((L, D_MODEL, N_EXPERTS), D_MODEL ** -0.5)
    moe_b_exp = nrm((L, N_EXPERTS), 0.01)
    moe_w1 = nrm((L, N_EXPERTS, D_MODEL, D_EXPERT), D_MODEL ** -0.5)
    moe_w3 = nrm((L, N_EXPERTS, D_MODEL, D_EXPERT), D_MODEL ** -0.5)
    moe_w2 = nrm((L, N_EXPERTS, D_EXPERT, D_MODEL), D_EXPERT ** -0.5 * DEEPNORM_BETA)
    ln2_g = 1.0 + nrm((L, D_MODEL), 0.02)
    ln2_b = nrm((L, D_MODEL), 0.02)
    return {'x': x, 'ln_in_g': ln_in_g, 'ln_in_b': ln_in_b, 'w_in': w_in, 'w_out': w_out,
            'lam_q1': lam_q1, 'lam_k1': lam_k1, 'lam_q2': lam_q2, 'lam_k2': lam_k2,
            'diff_norm_g': diff_norm_g, 'dn_conv_w': dn_conv_w, 'dn_a_log': dn_a_log,
            'dn_dt_bias': dn_dt_bias, 'dn_norm_g': dn_norm_g,
            's5_lambda_re': s5_lambda_re, 's5_lambda_im': s5_lambda_im, 's5_log_dt': s5_log_dt,
            's5_b_re': s5_b_re, 's5_b_im': s5_b_im, 's5_c_re': s5_c_re, 's5_c_im': s5_c_im,
            's5_d': s5_d, 's5_w_glu': s5_w_glu, 'ln1_g': ln1_g, 'ln1_b': ln1_b,
            'moe_w_grp': moe_w_grp, 'moe_b_grp': moe_b_grp, 'moe_w_exp': moe_w_exp,
            'moe_b_exp': moe_b_exp, 'moe_w1': moe_w1, 'moe_w3': moe_w3, 'moe_w2': moe_w2,
            'ln2_g': ln2_g, 'ln2_b': ln2_b}


def reference(x, ln_in_g, ln_in_b, w_in, w_out, lam_q1, lam_k1, lam_q2, lam_k2,
              diff_norm_g, dn_conv_w, dn_a_log, dn_dt_bias, dn_norm_g,
              s5_lambda_re, s5_lambda_im, s5_log_dt, s5_b_re, s5_b_im, s5_c_re, s5_c_im,
              s5_d, s5_w_glu, ln1_g, ln1_b, moe_w_grp, moe_b_grp, moe_w_exp, moe_b_exp,
              moe_w1, moe_w3, moe_w2, ln2_g, ln2_b):
    Bsz, S, _ = x.shape
    h = layer_norm(x, ln_in_g, ln_in_b)
    for l in range(DEPTH):
        lam_init = 0.8 - 0.6 * math.exp(-0.3 * l)
        proj = h @ w_in[l]
        a_q, a_k, a_v, b_qkv, b_gate, b_beta, b_a, c_u = split_cols(
            proj, (DA_QK, DA_QK, DA_WIDTH, 3 * GDN_WIDTH, GDN_WIDTH, GDN_HEADS, GDN_HEADS, S5_WIDTH))
        lam = (jnp.exp(jnp.sum(lam_q1[l] * lam_k1[l])) - jnp.exp(jnp.sum(lam_q2[l] * lam_k2[l]))).astype(jnp.float32) + lam_init
        y_a = diff_attention(a_q.reshape(Bsz, S, DA_HEADS, 2, DA_HEAD_DIM),
                             a_k.reshape(Bsz, S, DA_HEADS, 2, DA_HEAD_DIM),
                             a_v.reshape(Bsz, S, DA_HEADS, DA_V_DIM), lam, diff_norm_g[l], lam_init)
        y_b = gdn_mixer(b_qkv, b_gate, b_beta, b_a, dn_conv_w[l], dn_a_log[l], dn_dt_bias[l], dn_norm_g[l])
        y_c = s5_mixer(c_u, s5_lambda_re[l], s5_lambda_im[l], s5_log_dt[l], s5_b_re[l], s5_b_im[l],
                       s5_c_re[l], s5_c_im[l], s5_d[l], s5_w_glu[l])
        mix = jnp.concatenate([y_a.astype(h.dtype), y_b.astype(h.dtype), y_c.astype(h.dtype)], -1) @ w_out[l]
        h = layer_norm(ALPHA * h + mix, ln1_g[l], ln1_b[l])
        ffn = hier_moe(h, moe_w_grp[l], moe_b_grp[l], moe_w_exp[l], moe_b_exp[l],
                       moe_w1[l], moe_w3[l], moe_w2[l])
        h = layer_norm(ALPHA * h + ffn, ln2_g[l], ln2_b[l])
    return h
```

```python
import functools
import math

import jax
import jax.numpy as jnp
from jax import lax
from jax.experimental import pallas as pl
from jax.experimental.pallas import tpu as pltpu

F32 = jnp.float32
BF16 = jnp.bfloat16

D_MODEL = 1024
DEPTH = 2
CHUNK = 64
DA_HEADS = 6
DA_HEAD_DIM = 32
DA_V_DIM = 64
DA_WIDTH = 384
GDN_HEADS = 6
GDN_HEAD_DIM = 64
GDN_WIDTH = 384
CONV_K = 4
S5_GROUP_DIM = 16
S5_GROUPS = 16
S5_WIDTH = 256
S5_STATE = 64
S5_LANES = S5_GROUPS * S5_STATE
N_EXPERT_GROUPS = 4
EXPERTS_PER_GROUP = 4
N_EXPERTS = 16
D_EXPERT = 512
ALPHA = (2 * DEPTH) ** 0.25
LN_EPS = 1e-5
RMS_EPS = 1e-6
LOG2E = 1.4426950408889634

V7X_VMEM_LIMIT_BYTES = 56 * 1024 * 1024
SUBLANES = 8
LANES = 128
NEG_BIG = -1e30

BETA_LANE0 = 0
A_LANE0 = 8


def _cparams(sem):
    return pltpu.CompilerParams(dimension_semantics=sem, vmem_limit_bytes=V7X_VMEM_LIMIT_BYTES)


def _layer_norm(x, g, b):
    mu = jnp.mean(x, axis=-1, keepdims=True)
    xc = x - mu
    var = jnp.mean(xc * xc, axis=-1, keepdims=True)
    return xc * lax.rsqrt(var + LN_EPS) * g + b


def _dot(a, b):
    return jnp.dot(a, b, preferred_element_type=F32)


def _dot_nt(a, b):
    return lax.dot_general(a, b, (((1,), (1,)), ((), ())), preferred_element_type=F32)


def _dot_tn(a, b):
    return lax.dot_general(a, b, (((0,), (0,)), ((), ())), preferred_element_type=F32)


def _proj_kernel(x_ref, g_ref, b_ref, wqT_ref, wvT_ref, wk_ref, wg_ref, wsm_ref, wc_ref,
                 *out_refs, apply_ln, q_scale):
    if apply_ln:
        h_ref, qT_ref, vT_ref, k_ref, gdn_ref, small_ref, cu_ref = out_refs
        h = _layer_norm(x_ref[...], g_ref[...], b_ref[...])
        h_ref[...] = h
    else:
        qT_ref, vT_ref, k_ref, gdn_ref, small_ref, cu_ref = out_refs
        h = x_ref[...]
    hb = h.astype(BF16)
    qT_ref[0] = (_dot_nt(wqT_ref[...], hb) * q_scale).astype(BF16)
    vT_ref[0] = _dot_nt(wvT_ref[...], hb).astype(BF16)
    k_ref[0] = _dot(hb, wk_ref[...]).astype(BF16)
    gdn_ref[...] = _dot(hb, wg_ref[...])
    small_ref[...] = _dot(hb, wsm_ref[...])
    cu_ref[...] = _dot(hb, wc_ref[...])


def _in_projection(x2d, g, b, wts, *, batch, seq, apply_ln, tm=512):
    t = batch * seq
    nt = seq // tm
    wqT, wvT, wk, wg, wsm, wc = wts
    q_scale = (DA_HEAD_DIM ** -0.5) * LOG2E
    row = lambda bi, i: (bi * nt + i, 0)
    const = lambda bi, i: (0, 0)
    out_shape = [
        jax.ShapeDtypeStruct((batch, DA_WIDTH, seq), BF16),
        jax.ShapeDtypeStruct((batch, DA_WIDTH, seq), BF16),
        jax.ShapeDtypeStruct((batch, seq, DA_WIDTH), BF16),
        jax.ShapeDtypeStruct((t, 4 * GDN_WIDTH), F32),
        jax.ShapeDtypeStruct((t, LANES), F32),
        jax.ShapeDtypeStruct((t, S5_WIDTH), F32),
    ]
    out_specs = [
        pl.BlockSpec((1, DA_WIDTH, tm), lambda bi, i: (bi, 0, i)),
        pl.BlockSpec((1, DA_WIDTH, tm), lambda bi, i: (bi, 0, i)),
        pl.BlockSpec((1, tm, DA_WIDTH), lambda bi, i: (bi, i, 0)),
        pl.BlockSpec((tm, 4 * GDN_WIDTH), row),
        pl.BlockSpec((tm, LANES), row),
        pl.BlockSpec((tm, S5_WIDTH), row),
    ]
    if apply_ln:
        out_shape = [jax.ShapeDtypeStruct((t, D_MODEL), F32)] + out_shape
        out_specs = [pl.BlockSpec((tm, D_MODEL), row)] + out_specs
    in_specs = [
        pl.BlockSpec((tm, D_MODEL), row),
        pl.BlockSpec((1, D_MODEL), const),
        pl.BlockSpec((1, D_MODEL), const),
        pl.BlockSpec(wqT.shape, const),
        pl.BlockSpec(wvT.shape, const),
        pl.BlockSpec(wk.shape, const),
        pl.BlockSpec(wg.shape, const),
        pl.BlockSpec(wsm.shape, const),
        pl.BlockSpec(wc.shape, const),
    ]
    return pl.pallas_call(
        functools.partial(_proj_kernel, apply_ln=apply_ln, q_scale=q_scale),
        out_shape=out_shape,
        grid=(batch, nt),
        in_specs=in_specs,
        out_specs=out_specs,
        compiler_params=_cparams(("parallel", "parallel")),
        name="in_projection_ln" if apply_ln else "in_projection",
    )(x2d, g.reshape(1, -1), b.reshape(1, -1), wqT, wvT, wk, wg, wsm, wc)


def _attn_kernel(lam_ref, qT_ref, k_ref, vT_ref, g_ref, o_ref, qbd_ref, m_ref, l_ref, acc_ref,
                 *, tq, tk, out_scale):
    h = pl.program_id(1)
    i = pl.program_id(2)
    dh = DA_HEAD_DIM
    n_diag = tq // tk

    qbd_ref[...] = jnp.zeros_like(qbd_ref)
    q = qT_ref[0]
    for par in range(2):
        @pl.when(h % 2 == par)
        def _():
            base = par * 2 * dh
            qbd_ref[base:base + dh, 0:tq] = q[0:dh]
            qbd_ref[base + dh:base + 2 * dh, tq:2 * tq] = q[dh:2 * dh]

    m_ref[...] = jnp.full_like(m_ref, NEG_BIG)
    l_ref[...] = jnp.zeros_like(l_ref)
    acc_ref[...] = jnp.zeros_like(acc_ref)

    def step(j, mask):
        start = pl.multiple_of(j * tk, tk)
        kt = k_ref[0, pl.ds(start, tk), :]
        s = _dot(kt, qbd_ref[...])
        if mask is not None:
            s = jnp.where(mask, s, NEG_BIG)
        m_old = m_ref[...]
        m_new = jnp.maximum(m_old, jnp.max(s, axis=0, keepdims=True))
        alpha = jnp.exp2(m_old - m_new)
        p = jnp.exp2(s - m_new)
        l_ref[...] = alpha * l_ref[...] + jnp.sum(p, axis=0, keepdims=True)
        m_ref[...] = m_new
        vt = vT_ref[0, :, pl.ds(start, tk)]
        acc_ref[...] = alpha * acc_ref[...] + _dot(vt, p.astype(BF16))

    def full_body(j, carry):
        step(j, None)
        return carry

    lax.fori_loop(0, i * n_diag, full_body, 0)

    kc = lax.broadcasted_iota(jnp.int32, (tk, 1), 0) // CHUNK
    col = lax.broadcasted_iota(jnp.int32, (1, 2 * tq), 1)
    qc = jnp.where(col >= tq, col - tq, col) // CHUNK
    for d in range(n_diag):
        step(i * n_diag + d, (kc + d * (tk // CHUNK)) <= qc)

    l = l_ref[...]
    acc = acc_ref[...]
    lam = lam_ref[0]
    o = acc[:, 0:tq] / l[:, 0:tq] - lam * (acc[:, tq:2 * tq] / l[:, tq:2 * tq])
    ms = jnp.mean(o * o, axis=0, keepdims=True)
    o_ref[0] = o * lax.rsqrt(ms + RMS_EPS) * g_ref[...] * out_scale


def _diff_attention(lam, qT, k, vT, norm_g, *, lam_init, tq=512, tk=256):
    batch, _, seq = qT.shape
    nq = seq // tq
    dv = DA_V_DIM
    return pl.pallas_call(
        functools.partial(_attn_kernel, tq=tq, tk=tk, out_scale=1.0 - lam_init),
        out_shape=jax.ShapeDtypeStruct((batch, DA_WIDTH, seq), F32),
        grid=(batch, DA_HEADS, nq),
        in_specs=[
            pl.BlockSpec(memory_space=pltpu.SMEM),
            pl.BlockSpec((1, dv, tq), lambda b, h, i: (b, h, i)),
            pl.BlockSpec((1, seq, LANES), lambda b, h, i: (b, 0, h // 2)),
            pl.BlockSpec((1, dv, seq), lambda b, h, i: (b, h, 0)),
            pl.BlockSpec((dv, 1), lambda b, h, i: (0, 0)),
        ],
        out_specs=pl.BlockSpec((1, dv, tq), lambda b, h, i: (b, h, i)),
        scratch_shapes=[
            pltpu.VMEM((LANES, 2 * tq), BF16),
            pltpu.VMEM((1, 2 * tq), F32),
            pltpu.VMEM((1, 2 * tq), F32),
            pltpu.VMEM((dv, 2 * tq), F32),
        ],
        compiler_params=_cparams(("parallel", "parallel", "parallel")),
        name="diff_attention",
    )(lam.reshape(1), qT, k, vT, norm_g.reshape(dv, 1))


def _split_bf16(x):
    hi = x.astype(BF16)
    lo = (x - hi.astype(F32)).astype(BF16)
    return hi, lo


def _mm_rhs_split(a, b):
    n = b.shape[1]
    hi, lo = _split_bf16(b)
    r = _dot(a.astype(BF16), jnp.concatenate([hi, lo], axis=1))
    return r[:, 0:n] + r[:, n:2 * n]


def _gdn_kernel(qkv_ref, gate_ref, small_ref, convw_ref, gl_ref, ng_ref, o_ref,
                xbuf_ref, state_ref, *, rows):
    step_i = pl.program_id(1)
    dk = GDN_HEAD_DIM
    nch = rows // CHUNK
    halo = SUBLANES

    @pl.when(step_i == 0)
    def _():
        xbuf_ref[0:halo, :] = jnp.zeros((halo, 3 * GDN_WIDTH), F32)
        state_ref[...] = jnp.zeros_like(state_ref)

    xbuf_ref[halo:halo + rows, :] = qkv_ref[...]
    y = convw_ref[CONV_K - 1:CONV_K, :] * xbuf_ref[halo:halo + rows, :]
    for j in range(CONV_K - 1):
        off = halo - (CONV_K - 1) + j
        y = y + convw_ref[j:j + 1, :] * xbuf_ref[off:off + rows, :]
    xbuf_ref[0:halo, :] = xbuf_ref[rows:rows + halo, :]
    y = y * jax.nn.sigmoid(y)

    small = small_ref[...]
    beta_all = jax.nn.sigmoid(small)
    sp_in = small + gl_ref[1:2, :]
    softplus = jnp.maximum(sp_in, 0.0) + jnp.log(1.0 + jnp.exp(-jnp.abs(sp_in)))
    g_all = gl_ref[0:1, :] * softplus

    ri = lax.broadcasted_iota(jnp.int32, (CHUNK, CHUNK), 0)
    ci = lax.broadcasted_iota(jnp.int32, (CHUNK, CHUNK), 1)
    tri = ri >= ci
    strict = ri > ci
    tril_f = tri.astype(F32)

    gc_parts = []
    for c in range(nch):
        gch = g_all[c * CHUNK:(c + 1) * CHUNK, :]
        gc_parts.append(jnp.dot(tril_f, gch, preferred_element_type=F32,
                                precision=lax.Precision.HIGHEST))
    gc_all = jnp.concatenate(gc_parts, axis=0) if nch > 1 else gc_parts[0]
    pad = (-rows) % LANES
    gc_sq = jnp.concatenate([gc_all, jnp.zeros((pad, LANES), F32)], axis=0) if pad else gc_all
    gcT = gc_sq.T

    gate = gate_ref[...]
    ng = ng_ref[...]
    for c in range(nch):
        r0 = c * CHUNK
        for hh in range(GDN_HEADS):
            q = y[r0:r0 + CHUNK, hh * dk:(hh + 1) * dk]
            k = y[r0:r0 + CHUNK, GDN_WIDTH + hh * dk:GDN_WIDTH + (hh + 1) * dk]
            v = y[r0:r0 + CHUNK, 2 * GDN_WIDTH + hh * dk:2 * GDN_WIDTH + (hh + 1) * dk]
            q = q * lax.rsqrt(jnp.sum(q * q, axis=-1, keepdims=True) + RMS_EPS) * (dk ** -0.5)
            k = k * lax.rsqrt(jnp.sum(k * k, axis=-1, keepdims=True) + RMS_EPS)
            beta = beta_all[r0:r0 + CHUNK, BETA_LANE0 + hh:BETA_LANE0 + hh + 1]
            gcol = gc_all[r0:r0 + CHUNK, A_LANE0 + hh:A_LANE0 + hh + 1]
            grow = gcT[A_LANE0 + hh:A_LANE0 + hh + 1, r0:r0 + CHUNK]
            glast = gcT[A_LANE0 + hh:A_LANE0 + hh + 1, r0 + CHUNK - 1:r0 + CHUNK]
            decay = jnp.where(tri, jnp.exp(jnp.where(tri, gcol - grow, 0.0)), 0.0)
            eg = jnp.exp(gcol)
            kb = k * beta
            kbf = k.astype(BF16)
            lmat = jnp.where(strict, _dot_nt(kb.astype(BF16), kbf) * decay, 0.0)
            rhs = jnp.concatenate([v * beta, kb * eg], axis=1)
            x = rhs - _mm_rhs_split(lmat, rhs)
            p = lmat
            for _ in range(5):
                p = _mm_rhs_split(p, p)
                x = x + _mm_rhs_split(p, x)
            u = x[:, 0:dk]
            w = x[:, dk:2 * dk]
            a_intra = _dot_nt(q.astype(BF16), kbf) * decay
            q_dec = q * eg
            k_dec = k * jnp.exp(glast - gcol)
            st = state_ref[hh]
            stb = st.astype(BF16)
            v_new = u - _dot(w.astype(BF16), stb)
            o = _dot(q_dec.astype(BF16), stb) + _dot(a_intra.astype(BF16), v_new.astype(BF16))
            state_ref[hh] = st * jnp.exp(glast) + _dot_tn(k_dec.astype(BF16), v_new.astype(BF16))
            ms = jnp.mean(o * o, axis=-1, keepdims=True)
            on = o * lax.rsqrt(ms + RMS_EPS) * ng
            gt = gate[r0:r0 + CHUNK, hh * dk:(hh + 1) * dk]
            o_ref[r0:r0 + CHUNK, hh * dk:(hh + 1) * dk] = on * (gt * jax.nn.sigmoid(gt))


def _gdn_mixer(gdn_in, small, conv_w, a_log, dt_bias, norm_g, *, batch, seq, rows=128):
    t = batch * seq
    ns = seq // rows
    gl = jnp.zeros((SUBLANES, LANES), F32)
    gl = gl.at[0, A_LANE0:A_LANE0 + GDN_HEADS].set(-jnp.exp(a_log.astype(F32)))
    gl = gl.at[1, A_LANE0:A_LANE0 + GDN_HEADS].set(dt_bias.astype(F32))
    convw = jnp.zeros((SUBLANES, 3 * GDN_WIDTH), F32).at[0:CONV_K].set(conv_w.astype(F32))
    row = lambda b, i: (b * ns + i, 0)
    const = lambda b, i: (0, 0)
    return pl.pallas_call(
        functools.partial(_gdn_kernel, rows=rows),
        out_shape=jax.ShapeDtypeStruct((t, GDN_WIDTH), F32),
        grid=(batch, ns),
        in_specs=[
            pl.BlockSpec((rows, 3 * GDN_WIDTH), row),
            pl.BlockSpec((rows, GDN_WIDTH), lambda b, i: (b * ns + i, 3)),
            pl.BlockSpec((rows, LANES), row),
            pl.BlockSpec((SUBLANES, 3 * GDN_WIDTH), const),
            pl.BlockSpec((SUBLANES, LANES), const),
            pl.BlockSpec((1, GDN_HEAD_DIM), const),
        ],
        out_specs=pl.BlockSpec((rows, GDN_WIDTH), row),
        scratch_shapes=[
            pltpu.VMEM((rows + SUBLANES, 3 * GDN_WIDTH), F32),
            pltpu.VMEM((GDN_HEADS, GDN_HEAD_DIM, GDN_HEAD_DIM), F32),
        ],
        compiler_params=_cparams(("parallel", "arbitrary")),
        name="gated_deltanet",
    )(gdn_in, gdn_in, small, convw, gl, norm_g.reshape(1, -1).astype(F32))


def _cmul(ar, ai, br, bi):
    return ar * br - ai * bi, ar * bi + ai * br


def _s5_kernel(u_ref, bblk_ref, ccat_ref, apow_ref, d_ref, wglu_ref, o_ref,
               bu_ref, x_ref, carry_ref, *, tm):
    n = S5_LANES

    @pl.when(pl.program_id(1) == 0)
    def _():
        carry_ref[...] = jnp.zeros_like(carry_ref)

    u = u_ref[...]
    bu_ref[...] = _dot(u.astype(BF16), bblk_ref[...])

    def group(gidx, carry):
        c_re, c_im = carry
        r0 = pl.multiple_of(gidx * SUBLANES, SUBLANES)
        x_re = bu_ref[pl.ds(r0, SUBLANES), 0:n]
        x_im = bu_ref[pl.ds(r0, SUBLANES), n:2 * n]
        for lvl, d in enumerate((1, 2, 4)):
            a_re = apow_ref[lvl * 2 * SUBLANES:lvl * 2 * SUBLANES + SUBLANES, :]
            a_im = apow_ref[lvl * 2 * SUBLANES + SUBLANES:(lvl + 1) * 2 * SUBLANES, :]
            s_re = pltpu.roll(x_re, d, 0)
            s_im = pltpu.roll(x_im, d, 0)
            t_re, t_im = _cmul(a_re, a_im, s_re, s_im)
            x_re = x_re + t_re
            x_im = x_im + t_im
        p_re = apow_ref[6 * SUBLANES:7 * SUBLANES, :]
        p_im = apow_ref[7 * SUBLANES:8 * SUBLANES, :]
        t_re, t_im = _cmul(p_re, p_im, c_re, c_im)
        x_re = x_re + t_re
        x_im = x_im + t_im
        x_ref[pl.ds(r0, SUBLANES), 0:n] = x_re
        x_ref[pl.ds(r0, SUBLANES), n:2 * n] = x_im
        return x_re[SUBLANES - 1:SUBLANES, :], x_im[SUBLANES - 1:SUBLANES, :]

    c_re, c_im = lax.fori_loop(0, tm // SUBLANES, group,
                               (carry_ref[0:1, :], carry_ref[1:2, :]))
    carry_ref[0:1, :] = c_re
    carry_ref[1:2, :] = c_im

    yv = _dot(x_ref[...].astype(BF16), ccat_ref[...]) + d_ref[...] * u
    yv = 0.5 * yv * (1.0 + jnp.tanh(0.7978845608028654 * (yv + 0.044715 * (yv * yv * yv))))
    z = _dot(yv.astype(BF16), wglu_ref[...])
    o_ref[...] = yv * jax.nn.sigmoid(z)


def _s5_params(lam_re, lam_im, log_dt, b_re, b_im, c_re, c_im):
    f32 = F32
    lre, lim = lam_re.astype(f32), lam_im.astype(f32)
    dt = jnp.exp(log_dt.astype(f32))[:, None]
    mag = jnp.exp(lre * dt)
    ab_re, ab_im = mag * jnp.cos(lim * dt), mag * jnp.sin(lim * dt)
    num_re, num_im = ab_re - 1.0, ab_im
    den = lre * lre + lim * lim
    coef_re = (num_re * lre + num_im * lim) / den
    coef_im = (num_im * lre - num_re * lim) / den
    br, bi = b_re.astype(f32), b_im.astype(f32)
    bb_re = coef_re[..., None] * br - coef_im[..., None] * bi
    bb_im = coef_re[..., None] * bi + coef_im[..., None] * br
    eye = jnp.eye(S5_GROUPS, dtype=f32)
    blk_re = jnp.einsum('gph,gk->ghkp', bb_re, eye).reshape(S5_WIDTH, S5_LANES)
    blk_im = jnp.einsum('gph,gk->ghkp', bb_im, eye).reshape(S5_WIDTH, S5_LANES)
    bblk = jnp.concatenate([blk_re, blk_im], axis=1).astype(BF16)
    cb_re = jnp.einsum('ghp,gk->gpkh', c_re.astype(f32), eye).reshape(S5_LANES, S5_WIDTH)
    cb_im = jnp.einsum('ghp,gk->gpkh', c_im.astype(f32), eye).reshape(S5_LANES, S5_WIDTH)
    ccat = jnp.concatenate([cb_re, -cb_im], axis=0).astype(BF16)
    a1 = (ab_re.reshape(1, -1), ab_im.reshape(1, -1))
    pows = [a1]
    for _ in range(SUBLANES - 1):
        pows.append(_cmul(pows[-1][0], pows[-1][1], a1[0], a1[1]))
    rid = jnp.arange(SUBLANES)[:, None]
    rows = []
    for d in (1, 2, 4):
        mask = (rid >= d).astype(f32)
        rows.append(mask * pows[d - 1][0])
        rows.append(mask * pows[d - 1][1])
    rows.append(jnp.concatenate([pows[r][0] for r in range(SUBLANES)], axis=0))
    rows.append(jnp.concatenate([pows[r][1] for r in range(SUBLANES)], axis=0))
    apow = jnp.concatenate(rows, axis=0)
    return bblk, ccat, apow


def _s5_mixer(cu, lam_re, lam_im, log_dt, b_re, b_im, c_re, c_im, d, w_glu, *, batch, seq, tm=256):
    t = batch * seq
    ns = seq // tm
    bblk, ccat, apow = _s5_params(lam_re, lam_im, log_dt, b_re, b_im, c_re, c_im)
    row = lambda b, i: (b * ns + i, 0)
    const = lambda b, i: (0, 0)
    return pl.pallas_call(
        functools.partial(_s5_kernel, tm=tm),
        out_shape=jax.ShapeDtypeStruct((t, S5_WIDTH), F32),
        grid=(batch, ns),
        in_specs=[
            pl.BlockSpec((tm, S5_WIDTH), row),
            pl.BlockSpec(bblk.shape, const),
            pl.BlockSpec(ccat.shape, const),
            pl.BlockSpec(apow.shape, const),
            pl.BlockSpec((1, S5_WIDTH), const),
            pl.BlockSpec((S5_WIDTH, S5_WIDTH), const),
        ],
        out_specs=pl.BlockSpec((tm, S5_WIDTH), row),
        scratch_shapes=[
            pltpu.VMEM((tm, 2 * S5_LANES), F32),
            pltpu.VMEM((tm, 2 * S5_LANES), F32),
            pltpu.VMEM((SUBLANES, S5_LANES), F32),
        ],
        compiler_params=_cparams(("parallel", "arbitrary")),
        name="s5_mixer",
    )(cu, bblk, ccat, apow, d.reshape(1, -1).astype(F32), w_glu.astype(BF16))


def _route_rows(lt, n_tok):
    g = [lt[r:r + 1, :] for r in range(N_EXPERT_GROUPS)]
    gm = functools.reduce(jnp.maximum, g)
    gsum = functools.reduce(lambda a, b: a + b, [jnp.exp(x - gm) for x in g])
    g_p = 1.0 / gsum
    taken = jnp.zeros_like(gm) > 1.0
    g_hot = []
    for x in g:
        hit = jnp.logical_and(x == gm, jnp.logical_not(taken))
        g_hot.append(hit)
        taken = jnp.logical_or(taken, hit)
    e_sel = []
    for j in range(EXPERTS_PER_GROUP):
        acc = jnp.zeros_like(gm)
        for gi in range(N_EXPERT_GROUPS):
            r = 8 + gi * EXPERTS_PER_GROUP + j
            acc = acc + jnp.where(g_hot[gi], lt[r:r + 1, :], 0.0)
        e_sel.append(acc)
    m1 = functools.reduce(jnp.maximum, e_sel)
    taken = jnp.zeros_like(gm) > 1.0
    hot1 = []
    for x in e_sel:
        hit = jnp.logical_and(x == m1, jnp.logical_not(taken))
        hot1.append(hit)
        taken = jnp.logical_or(taken, hit)
    rest = [jnp.where(hh, NEG_BIG, x) for hh, x in zip(hot1, e_sel)]
    m2 = functools.reduce(jnp.maximum, rest)
    taken = jnp.zeros_like(gm) > 1.0
    hot2 = []
    for hh, x in zip(hot1, rest):
        hit = jnp.logical_and(jnp.logical_and(x == m2, jnp.logical_not(hh)), jnp.logical_not(taken))
        hot2.append(hit)
        taken = jnp.logical_or(taken, hit)
    e2 = jnp.exp(m2 - m1)
    w1 = g_p / (1.0 + e2)
    w2 = g_p * e2 / (1.0 + e2)
    rows = []
    for gi in range(N_EXPERT_GROUPS):
        for j in range(EXPERTS_PER_GROUP):
            val = jnp.where(hot1[j], w1, 0.0) + jnp.where(hot2[j], w2, 0.0)
            rows.append(jnp.where(g_hot[gi], val, 0.0))
    return jnp.concatenate(rows, axis=0)


def _outproj_kernel(h_ref, yaT_ref, yb_ref, yc_ref, wa_ref, wb_ref, wc_ref, g_ref, b_ref,
                    wrT_ref, br_ref, h1_ref, h1b_ref, comb_ref, *, tm):
    ya = yaT_ref[0].T
    mix = _dot(ya.astype(BF16), wa_ref[...])
    mix = mix + _dot(yb_ref[...].astype(BF16), wb_ref[...])
    mix = mix + _dot(yc_ref[...].astype(BF16), wc_ref[...])
    h1 = _layer_norm(ALPHA * h_ref[...] + mix, g_ref[...], b_ref[...])
    h1_ref[...] = h1
    h1b_ref[...] = h1.astype(BF16)
    h_hi, h_lo = _split_bf16(h1)
    w_hi = wrT_ref[0:LANES, :]
    w_lo = wrT_ref[LANES:2 * LANES, :]
    lt = _dot_nt(w_hi, h_hi) + _dot_nt(w_hi, h_lo) + _dot_nt(w_lo, h_hi) + br_ref[...]
    comb = _route_rows(lt, tm)
    combp = jnp.concatenate([comb, jnp.zeros((LANES - N_EXPERTS, tm), F32)], axis=0)
    comb_ref[...] = combp.T


def _out_projection(h, yaT, yb, yc, w_out, ln_g, ln_b, w_grp, b_grp, w_exp, b_exp,
                    *, batch, seq, tm=512):
    t = batch * seq
    nt = seq // tm
    wa = w_out[0:DA_WIDTH].astype(BF16)
    wb = w_out[DA_WIDTH:DA_WIDTH + GDN_WIDTH].astype(BF16)
    wc = w_out[DA_WIDTH + GDN_WIDTH:].astype(BF16)
    wr = jnp.zeros((D_MODEL, LANES), F32)
    wr = wr.at[:, 0:N_EXPERT_GROUPS].set(w_grp.astype(F32)).at[:, 8:8 + N_EXPERTS].set(w_exp.astype(F32))
    wrT = wr.T
    wr_hi = wrT.astype(BF16)
    wr_lo = (wrT - wr_hi.astype(F32)).astype(BF16)
    wr_cat = jnp.concatenate([wr_hi, wr_lo], axis=0)
    br = jnp.zeros((LANES, 1), F32)
    br = br.at[0:N_EXPERT_GROUPS, 0].set(b_grp.astype(F32)).at[8:8 + N_EXPERTS, 0].set(b_exp.astype(F32))
    row = lambda b, i: (b * nt + i, 0)
    const = lambda b, i: (0, 0)
    return pl.pallas_call(
        functools.partial(_outproj_kernel, tm=tm),
        out_shape=[
            jax.ShapeDtypeStruct((t, D_MODEL), F32),
            jax.ShapeDtypeStruct((t, D_MODEL), BF16),
            jax.ShapeDtypeStruct((t, LANES), F32),
        ],
        grid=(batch, nt),
        in_specs=[
            pl.BlockSpec((tm, D_MODEL), row),
            pl.BlockSpec((1, DA_WIDTH, tm), lambda b, i: (b, 0, i)),
            pl.BlockSpec((tm, GDN_WIDTH), row),
            pl.BlockSpec((tm, S5_WIDTH), row),
            pl.BlockSpec(wa.shape, const),
            pl.BlockSpec(wb.shape, const),
            pl.BlockSpec(wc.shape, const),
            pl.BlockSpec((1, D_MODEL), const),
            pl.BlockSpec((1, D_MODEL), const),
            pl.BlockSpec(wr_cat.shape, const),
            pl.BlockSpec((LANES, 1), const),
        ],
        out_specs=[
            pl.BlockSpec((tm, D_MODEL), row),
            pl.BlockSpec((tm, D_MODEL), row),
            pl.BlockSpec((tm, LANES), row),
        ],
        compiler_params=_cparams(("parallel", "parallel")),
        name="out_projection_router",
    )(h, yaT, yb, yc, wa, wb, wc, ln_g.reshape(1, -1), ln_b.reshape(1, -1), wr_cat, br)


def _moe_kernel(hb_ref, h1_ref, comb_ref, w1_ref, w3_ref, w2_ref, g_ref, b_ref, o_ref, acc_ref):
    e = pl.program_id(1)

    @pl.when(e == 0)
    def _():
        acc_ref[...] = jnp.zeros_like(acc_ref)

    x = hb_ref[...]
    lane = lax.broadcasted_iota(jnp.int32, (1, LANES), 1)
    c = jnp.sum(jnp.where(lane == e, comb_ref[...], 0.0), axis=1, keepdims=True)
    a = _dot(x, w1_ref[0])
    b = _dot(x, w3_ref[0])
    hid = a * jax.nn.sigmoid(a) * b
    acc_ref[...] += c * _dot(hid.astype(BF16), w2_ref[0])

    @pl.when(e == N_EXPERTS - 1)
    def _():
        o_ref[...] = _layer_norm(ALPHA * h1_ref[...] + acc_ref[...], g_ref[...], b_ref[...])


def _moe(h1, h1b, comb, w1, w3, w2, ln_g, ln_b, *, tm=1024):
    t = h1.shape[0]
    nt = t // tm
    row = lambda i, e: (i, 0)
    const = lambda i, e: (0, 0)
    return pl.pallas_call(
        _moe_kernel,
        out_shape=jax.ShapeDtypeStruct((t, D_MODEL), F32),
        grid=(nt, N_EXPERTS),
        in_specs=[
            pl.BlockSpec((tm, D_MODEL), row),
            pl.BlockSpec((tm, D_MODEL), row),
            pl.BlockSpec((tm, LANES), row),
            pl.BlockSpec((1, D_MODEL, D_EXPERT), lambda i, e: (e, 0, 0)),
            pl.BlockSpec((1, D_MODEL, D_EXPERT), lambda i, e: (e, 0, 0)),
            pl.BlockSpec((1, D_EXPERT, D_MODEL), lambda i, e: (e, 0, 0)),
            pl.BlockSpec((1, D_MODEL), const),
            pl.BlockSpec((1, D_MODEL), const),
        ],
        out_specs=pl.BlockSpec((tm, D_MODEL), row),
        scratch_shapes=[pltpu.VMEM((tm, D_MODEL), F32)],
        compiler_params=_cparams(("parallel", "arbitrary")),
        name="moe_ffn",
    )(h1b, h1, comb, w1, w3, w2, ln_g.reshape(1, -1), ln_b.reshape(1, -1))


def _split_w_in(w):
    o = 0
    wq = w[:, o:o + DA_WIDTH]; o += DA_WIDTH
    wk = w[:, o:o + DA_WIDTH]; o += DA_WIDTH
    wv = w[:, o:o + DA_WIDTH]; o += DA_WIDTH
    wg = w[:, o:o + 4 * GDN_WIDTH]; o += 4 * GDN_WIDTH
    wbeta = w[:, o:o + GDN_HEADS]; o += GDN_HEADS
    wa = w[:, o:o + GDN_HEADS]; o += GDN_HEADS
    wc = w[:, o:o + S5_WIDTH]
    wsm = jnp.zeros((D_MODEL, LANES), w.dtype)
    wsm = wsm.at[:, BETA_LANE0:BETA_LANE0 + GDN_HEADS].set(wbeta)
    wsm = wsm.at[:, A_LANE0:A_LANE0 + GDN_HEADS].set(wa)
    return (wq.T.astype(BF16), wv.T.astype(BF16), wk.astype(BF16), wg.astype(BF16),
            wsm.astype(BF16), wc.astype(BF16))


def kernel(x, ln_in_g, ln_in_b, w_in, w_out, lam_q1, lam_k1, lam_q2, lam_k2, diff_norm_g, dn_conv_w, dn_a_log, dn_dt_bias, dn_norm_g, s5_lambda_re, s5_lambda_im, s5_log_dt, s5_b_re, s5_b_im, s5_c_re, s5_c_im, s5_d, s5_w_glu, ln1_g, ln1_b, moe_w_grp, moe_b_grp, moe_w_exp, moe_b_exp, moe_w1, moe_w3, moe_w2, ln2_g, ln2_b):
    batch, seq, d = x.shape
    h = x.reshape(batch * seq, d)
    for l in range(DEPTH):
        lam_init = 0.8 - 0.6 * math.exp(-0.3 * l)
        wts = _split_w_in(w_in[l])
        outs = _in_projection(h, ln_in_g, ln_in_b, wts, batch=batch, seq=seq, apply_ln=(l == 0))
        if l == 0:
            h, qT, vT, k, gdn_in, small, cu = outs
        else:
            qT, vT, k, gdn_in, small, cu = outs
        lam = (jnp.exp(jnp.sum(lam_q1[l] * lam_k1[l])) - jnp.exp(jnp.sum(lam_q2[l] * lam_k2[l]))
               ).astype(F32) + lam_init
        yaT = _diff_attention(lam, qT, k, vT, diff_norm_g[l].astype(F32), lam_init=lam_init)
        yb = _gdn_mixer(gdn_in, small, dn_conv_w[l], dn_a_log[l], dn_dt_bias[l], dn_norm_g[l],
                        batch=batch, seq=seq)
        yc = _s5_mixer(cu, s5_lambda_re[l], s5_lambda_im[l], s5_log_dt[l], s5_b_re[l], s5_b_im[l],
                       s5_c_re[l], s5_c_im[l], s5_d[l], s5_w_glu[l], batch=batch, seq=seq)
        h1, h1b, comb = _out_projection(h, yaT, yb, yc, w_out[l], ln1_g[l], ln1_b[l],
                                        moe_w_grp[l], moe_b_grp[l], moe_w_exp[l], moe_b_exp[l],
                                        batch=batch, seq=seq)
        h = _moe(h1, h1b, comb, moe_w1[l].astype(BF16), moe_w3[l].astype(BF16),
                 moe_w2[l].astype(BF16), ln2_g[l], ln2_b[l])
    return h.reshape(batch, seq, d)
```

```python
import functools
import math

import jax
import jax.numpy as jnp
from jax import lax
from jax.experimental import pallas as pl
from jax.experimental.pallas import tpu as pltpu

F32 = jnp.float32
BF16 = jnp.bfloat16

D_MODEL = 1024
DEPTH = 2
CHUNK = 64
DA_HEADS = 6
DA_HEAD_DIM = 32
DA_V_DIM = 64
DA_WIDTH = 384
GDN_HEADS = 6
GDN_HEAD_DIM = 64
GDN_WIDTH = 384
CONV_K = 4
S5_GROUP_DIM = 16
S5_GROUPS = 16
S5_WIDTH = 256
S5_STATE = 64
S5_LANES = S5_GROUPS * S5_STATE
N_EXPERT_GROUPS = 4
EXPERTS_PER_GROUP = 4
N_EXPERTS = 16
D_EXPERT = 512
ALPHA = (2 * DEPTH) ** 0.25
LN_EPS = 1e-5
RMS_EPS = 1e-6
LOG2E = 1.4426950408889634

V7X_VMEM_LIMIT_BYTES = 56 * 1024 * 1024
SUBLANES = 8
LANES = 128
NEG_BIG = -1e30
FAST_MAX_LOG2 = 100.0
K_PAD = LANES

BETA_LANE0 = 0
A_LANE0 = 8


def _cparams(sem):
    return pltpu.CompilerParams(dimension_semantics=sem, vmem_limit_bytes=V7X_VMEM_LIMIT_BYTES)


def _layer_norm(x, g, b):
    mu = jnp.mean(x, axis=-1, keepdims=True)
    xc = x - mu
    var = jnp.mean(xc * xc, axis=-1, keepdims=True)
    return xc * lax.rsqrt(var + LN_EPS) * g + b


def _dot(a, b):
    return jnp.dot(a, b, preferred_element_type=F32)


def _dot_nt(a, b):
    return lax.dot_general(a, b, (((1,), (1,)), ((), ())), preferred_element_type=F32)


def _dot_tn(a, b):
    return lax.dot_general(a, b, (((0,), (0,)), ((), ())), preferred_element_type=F32)


def _proj_kernel(x_ref, g_ref, b_ref, wqT_ref, wvT_ref, wk_ref, kone_ref, wg_ref, wsm_ref, wc_ref,
                 *out_refs, apply_ln, q_scale):
    if apply_ln:
        h_ref, qT_ref, vT_ref, k_ref, gdn_ref, small_ref, cu_ref = out_refs
        h = _layer_norm(x_ref[...], g_ref[...], b_ref[...])
        h_ref[...] = h
    else:
        qT_ref, vT_ref, k_ref, gdn_ref, small_ref, cu_ref = out_refs
        h = x_ref[...]
    hb = h.astype(BF16)
    qT_ref[0] = (_dot_nt(wqT_ref[...], hb) * q_scale).astype(BF16)
    vT_ref[0] = _dot_nt(wvT_ref[...], hb).astype(BF16)
    k_ref[0] = (_dot(hb, wk_ref[...]) + kone_ref[...]).astype(BF16)
    gdn_ref[...] = _dot(hb, wg_ref[...])
    small_ref[...] = _dot(hb, wsm_ref[...])
    cu_ref[...] = _dot(hb, wc_ref[...])


def _in_projection(x2d, g, b, wts, *, batch, seq, apply_ln, tm=512):
    t = batch * seq
    nt = seq // tm
    wqT, wvT, wk, kone, wg, wsm, wc = wts
    kw = DA_HEADS * K_PAD
    q_scale = (DA_HEAD_DIM ** -0.5) * LOG2E
    row = lambda bi, i: (bi * nt + i, 0)
    const = lambda bi, i: (0, 0)
    out_shape = [
        jax.ShapeDtypeStruct((batch, DA_WIDTH, seq), BF16),
        jax.ShapeDtypeStruct((batch, DA_WIDTH, seq), BF16),
        jax.ShapeDtypeStruct((batch, seq, kw), BF16),
        jax.ShapeDtypeStruct((t, 4 * GDN_WIDTH), F32),
        jax.ShapeDtypeStruct((t, LANES), F32),
        jax.ShapeDtypeStruct((t, S5_WIDTH), F32),
    ]
    out_specs = [
        pl.BlockSpec((1, DA_WIDTH, tm), lambda bi, i: (bi, 0, i)),
        pl.BlockSpec((1, DA_WIDTH, tm), lambda bi, i: (bi, 0, i)),
        pl.BlockSpec((1, tm, kw), lambda bi, i: (bi, i, 0)),
        pl.BlockSpec((tm, 4 * GDN_WIDTH), row),
        pl.BlockSpec((tm, LANES), row),
        pl.BlockSpec((tm, S5_WIDTH), row),
    ]
    if apply_ln:
        out_shape = [jax.ShapeDtypeStruct((t, D_MODEL), F32)] + out_shape
        out_specs = [pl.BlockSpec((tm, D_MODEL), row)] + out_specs
    in_specs = [
        pl.BlockSpec((tm, D_MODEL), row),
        pl.BlockSpec((1, D_MODEL), const),
        pl.BlockSpec((1, D_MODEL), const),
        pl.BlockSpec(wqT.shape, const),
        pl.BlockSpec(wvT.shape, const),
        pl.BlockSpec(wk.shape, const),
        pl.BlockSpec(kone.shape, const),
        pl.BlockSpec(wg.shape, const),
        pl.BlockSpec(wsm.shape, const),
        pl.BlockSpec(wc.shape, const),
    ]
    return pl.pallas_call(
        functools.partial(_proj_kernel, apply_ln=apply_ln, q_scale=q_scale),
        out_shape=out_shape,
        grid=(batch, nt),
        in_specs=in_specs,
        out_specs=out_specs,
        compiler_params=_cparams(("parallel", "parallel")),
        name="in_projection_ln" if apply_ln else "in_projection",
    )(x2d, g.reshape(1, -1), b.reshape(1, -1), wqT, wvT, wk, kone, wg, wsm, wc)


def _attn_kernel(lam_ref, qT_ref, k_ref, vT_ref, g_ref, o_ref,
                 qbd_ref, sa_ref, sb_ref, m_ref, l_ref, acc_ref, cm_ref, *, tq, tk, out_scale):
    i = pl.program_id(2)
    dh = DA_HEAD_DIM
    ref_row = 2 * dh
    assert tq == 2 * tk

    def qk(j):
        start = pl.multiple_of(j * tk, tk)
        return _dot(k_ref[0, pl.ds(start, tk), :], qbd_ref[...])

    def pv(j, p):
        start = pl.multiple_of(j * tk, tk)
        return _dot(vT_ref[0, :, pl.ds(start, tk)], p.astype(BF16))

    def init_stats():
        m_ref[...] = jnp.full_like(m_ref, NEG_BIG)
        l_ref[...] = jnp.zeros_like(l_ref)
        acc_ref[...] = jnp.zeros_like(acc_ref)

    def exact_step(j, mask):
        s = qk(j)
        if mask is not None:
            s = jnp.where(mask, s, NEG_BIG)
        m_old = m_ref[...]
        m_new = jnp.maximum(m_old, jnp.max(s, axis=0, keepdims=True))
        alpha = jnp.exp2(m_old - m_new)
        p = jnp.exp2(s - m_new)
        l_ref[...] = alpha * l_ref[...] + jnp.sum(p, axis=0, keepdims=True)
        m_ref[...] = m_new
        acc_ref[...] = alpha * acc_ref[...] + pv(j, p)

    def stream_step(s_ref, j, mask):
        s = s_ref[...]
        if mask is not None:
            s = jnp.where(mask, s, NEG_BIG)
        cm_ref[...] = jnp.maximum(cm_ref[...], jnp.max(s, axis=0, keepdims=True))
        p = jnp.exp2(s)
        l_ref[...] += jnp.sum(p, axis=0, keepdims=True)
        acc_ref[...] += pv(j, p)

    kc = lax.broadcasted_iota(jnp.int32, (tk, 1), 0) // CHUNK
    col = lax.broadcasted_iota(jnp.int32, (1, 2 * tq), 1)
    qc = jnp.where(col >= tq, col - tq, col) // CHUNK

    def diag_mask(d):
        return (kc + d * (tk // CHUNK)) <= qc

    qbd_ref[...] = jnp.zeros_like(qbd_ref)
    q = qT_ref[0]
    qbd_ref[0:dh, 0:tq] = q[0:dh]
    qbd_ref[dh:2 * dh, tq:2 * tq] = q[dh:2 * dh]
    init_stats()

    off_diag = i > 0
    for d in range(2):
        exact_step(d, jnp.logical_or(diag_mask(d), off_diag))

    m = m_ref[...]
    mref = m.astype(BF16)
    fix = jnp.exp2(m - mref.astype(F32))
    l_ref[...] = l_ref[...] * fix
    acc_ref[...] = acc_ref[...] * fix
    qbd_ref[ref_row:ref_row + 16, :] = jnp.broadcast_to(-mref, (16, 2 * tq))
    cm_ref[...] = jnp.zeros_like(cm_ref)

    @pl.when(off_diag)
    def _():
        sa_ref[...] = qk(2)

        def pair_body(p, carry):
            t = 2 * p
            sb_ref[...] = qk(t + 1)
            stream_step(sa_ref, t, None)
            sa_ref[...] = qk(t + 2)
            stream_step(sb_ref, t + 1, None)
            return carry

        lax.fori_loop(1, i, pair_body, 0)
        t = 2 * i
        sb_ref[...] = qk(t + 1)
        stream_step(sa_ref, t, diag_mask(0))
        stream_step(sb_ref, t + 1, diag_mask(1))

    @pl.when(jnp.max(cm_ref[...]) > FAST_MAX_LOG2)
    def _():
        qbd_ref[ref_row:ref_row + 16, :] = jnp.zeros((16, 2 * tq), BF16)
        init_stats()

        def body(j, carry):
            exact_step(j, None)
            return carry

        lax.fori_loop(0, 2 * i, body, 0)
        for d in range(2):
            exact_step(2 * i + d, diag_mask(d))

    l = l_ref[...]
    acc = acc_ref[...]
    lam = lam_ref[0]
    o = acc[:, 0:tq] / l[:, 0:tq] - lam * (acc[:, tq:2 * tq] / l[:, tq:2 * tq])
    ms = jnp.mean(o * o, axis=0, keepdims=True)
    o_ref[0] = o * lax.rsqrt(ms + RMS_EPS) * g_ref[...] * out_scale


def _diff_attention(lam, qT, k, vT, norm_g, *, lam_init, tq=512, tk=256):
    batch, _, seq = qT.shape
    nq = seq // tq
    dv = DA_V_DIM
    return pl.pallas_call(
        functools.partial(_attn_kernel, tq=tq, tk=tk, out_scale=1.0 - lam_init),
        out_shape=jax.ShapeDtypeStruct((batch, DA_WIDTH, seq), F32),
        grid=(batch, DA_HEADS, nq),
        in_specs=[
            pl.BlockSpec(memory_space=pltpu.SMEM),
            pl.BlockSpec((1, dv, tq), lambda b, h, i: (b, h, i)),
            pl.BlockSpec((1, seq, LANES), lambda b, h, i: (b, 0, h)),
            pl.BlockSpec((1, dv, seq), lambda b, h, i: (b, h, 0)),
            pl.BlockSpec((dv, 1), lambda b, h, i: (0, 0)),
        ],
        out_specs=pl.BlockSpec((1, dv, tq), lambda b, h, i: (b, h, i)),
        scratch_shapes=[
            pltpu.VMEM((LANES, 2 * tq), BF16),
            pltpu.VMEM((tk, 2 * tq), F32),
            pltpu.VMEM((tk, 2 * tq), F32),
            pltpu.VMEM((1, 2 * tq), F32),
            pltpu.VMEM((1, 2 * tq), F32),
            pltpu.VMEM((dv, 2 * tq), F32),
            pltpu.VMEM((1, 2 * tq), F32),
        ],
        compiler_params=_cparams(("parallel", "parallel", "parallel")),
        name="diff_attention",
    )(lam.reshape(1), qT, k, vT, norm_g.reshape(dv, 1))


def _split_bf16(x):
    hi = x.astype(BF16)
    lo = (x - hi.astype(F32)).astype(BF16)
    return hi, lo


def _mm_rhs_split(a, b):
    n = b.shape[1]
    hi, lo = _split_bf16(b)
    r = _dot(a.astype(BF16), jnp.concatenate([hi, lo], axis=1))
    return r[:, 0:n] + r[:, n:2 * n]


def _gdn_kernel(qkv_ref, gate_ref, small_ref, convw_ref, gl_ref, ng_ref, o_ref,
                xbuf_ref, state_ref, *, rows):
    step_i = pl.program_id(1)
    dk = GDN_HEAD_DIM
    nch = rows // CHUNK
    halo = SUBLANES

    @pl.when(step_i == 0)
    def _():
        xbuf_ref[0:halo, :] = jnp.zeros((halo, 3 * GDN_WIDTH), F32)
        state_ref[...] = jnp.zeros_like(state_ref)

    xbuf_ref[halo:halo + rows, :] = qkv_ref[...]
    y = convw_ref[CONV_K - 1:CONV_K, :] * xbuf_ref[halo:halo + rows, :]
    for j in range(CONV_K - 1):
        off = halo - (CONV_K - 1) + j
        y = y + convw_ref[j:j + 1, :] * xbuf_ref[off:off + rows, :]
    xbuf_ref[0:halo, :] = xbuf_ref[rows:rows + halo, :]
    y = y * jax.nn.sigmoid(y)

    small = small_ref[...]
    beta_all = jax.nn.sigmoid(small)
    sp_in = small + gl_ref[1:2, :]
    softplus = jnp.maximum(sp_in, 0.0) + jnp.log(1.0 + jnp.exp(-jnp.abs(sp_in)))
    g_all = gl_ref[0:1, :] * softplus

    ri = lax.broadcasted_iota(jnp.int32, (CHUNK, CHUNK), 0)
    ci = lax.broadcasted_iota(jnp.int32, (CHUNK, CHUNK), 1)
    tri = ri >= ci
    strict = ri > ci
    tril_f = tri.astype(F32)

    gc_parts = []
    for c in range(nch):
        gch = g_all[c * CHUNK:(c + 1) * CHUNK, :]
        gc_parts.append(jnp.dot(tril_f, gch, preferred_element_type=F32,
                                precision=lax.Precision.HIGHEST))
    gc_all = jnp.concatenate(gc_parts, axis=0) if nch > 1 else gc_parts[0]
    pad = (-rows) % LANES
    gc_sq = jnp.concatenate([gc_all, jnp.zeros((pad, LANES), F32)], axis=0) if pad else gc_all
    gcT = gc_sq.T

    gate = gate_ref[...]
    ng = ng_ref[...]
    heads = range(GDN_HEADS)
    items = [(c, hh) for c in range(nch) for hh in heads]
    qs, ks, kbs, kbfs, rhss, decays, egs, glasts, gcols = {}, {}, {}, {}, {}, {}, {}, {}, {}
    for it in items:
        c, hh = it
        r0 = c * CHUNK
        q = y[r0:r0 + CHUNK, hh * dk:(hh + 1) * dk]
        k = y[r0:r0 + CHUNK, GDN_WIDTH + hh * dk:GDN_WIDTH + (hh + 1) * dk]
        v = y[r0:r0 + CHUNK, 2 * GDN_WIDTH + hh * dk:2 * GDN_WIDTH + (hh + 1) * dk]
        q = q * lax.rsqrt(jnp.sum(q * q, axis=-1, keepdims=True) + RMS_EPS) * (dk ** -0.5)
        k = k * lax.rsqrt(jnp.sum(k * k, axis=-1, keepdims=True) + RMS_EPS)
        beta = beta_all[r0:r0 + CHUNK, BETA_LANE0 + hh:BETA_LANE0 + hh + 1]
        gcol = gc_all[r0:r0 + CHUNK, A_LANE0 + hh:A_LANE0 + hh + 1]
        grow = gcT[A_LANE0 + hh:A_LANE0 + hh + 1, r0:r0 + CHUNK]
        glasts[it] = gcT[A_LANE0 + hh:A_LANE0 + hh + 1, r0 + CHUNK - 1:r0 + CHUNK]
        decays[it] = jnp.where(tri, jnp.exp(jnp.where(tri, gcol - grow, 0.0)), 0.0)
        eg = jnp.exp(gcol)
        kb = k * beta
        qs[it], ks[it], kbs[it], kbfs[it], egs[it], gcols[it] = q, k, kb, k.astype(BF16), eg, gcol
        rhss[it] = jnp.concatenate([v * beta, kb * eg], axis=1)

    kk = {it: _dot_nt(kbs[it].astype(BF16), kbfs[it]) for it in items}
    qk = {it: _dot_nt(qs[it].astype(BF16), kbfs[it]) for it in items}
    lm = {it: jnp.where(strict, kk[it] * decays[it], 0.0) for it in items}
    a_intra = {it: qk[it] * decays[it] for it in items}
    xs = {it: rhss[it] - _mm_rhs_split(lm[it], rhss[it]) for it in items}
    ps = lm
    for _ in range(5):
        ps = {it: _mm_rhs_split(ps[it], ps[it]) for it in items}
        xs = {it: xs[it] + _mm_rhs_split(ps[it], xs[it]) for it in items}

    state = [state_ref[hh] for hh in heads]
    for c in range(nch):
        r0 = c * CHUNK
        stb = [state[hh].astype(BF16) for hh in heads]
        ws = [_dot(xs[(c, hh)][:, dk:2 * dk].astype(BF16), stb[hh]) for hh in heads]
        qst = [_dot((qs[(c, hh)] * egs[(c, hh)]).astype(BF16), stb[hh]) for hh in heads]
        vn = [(xs[(c, hh)][:, 0:dk] - ws[hh]).astype(BF16) for hh in heads]
        av = [_dot(a_intra[(c, hh)].astype(BF16), vn[hh]) for hh in heads]
        kv = [_dot_tn((ks[(c, hh)] * jnp.exp(glasts[(c, hh)] - gcols[(c, hh)])).astype(BF16), vn[hh])
              for hh in heads]
        for hh in heads:
            state[hh] = state[hh] * jnp.exp(glasts[(c, hh)]) + kv[hh]
            o = qst[hh] + av[hh]
            ms = jnp.mean(o * o, axis=-1, keepdims=True)
            on = o * lax.rsqrt(ms + RMS_EPS) * ng
            gt = gate[r0:r0 + CHUNK, hh * dk:(hh + 1) * dk]
            o_ref[r0:r0 + CHUNK, hh * dk:(hh + 1) * dk] = on * (gt * jax.nn.sigmoid(gt))
    for hh in heads:
        state_ref[hh] = state[hh]


def _gdn_mixer(gdn_in, small, conv_w, a_log, dt_bias, norm_g, *, batch, seq, rows=128):
    t = batch * seq
    ns = seq // rows
    gl = jnp.zeros((SUBLANES, LANES), F32)
    gl = gl.at[0, A_LANE0:A_LANE0 + GDN_HEADS].set(-jnp.exp(a_log.astype(F32)))
    gl = gl.at[1, A_LANE0:A_LANE0 + GDN_HEADS].set(dt_bias.astype(F32))
    convw = jnp.zeros((SUBLANES, 3 * GDN_WIDTH), F32).at[0:CONV_K].set(conv_w.astype(F32))
    row = lambda b, i: (b * ns + i, 0)
    const = lambda b, i: (0, 0)
    return pl.pallas_call(
        functools.partial(_gdn_kernel, rows=rows),
        out_shape=jax.ShapeDtypeStruct((t, GDN_WIDTH), F32),
        grid=(batch, ns),
        in_specs=[
            pl.BlockSpec((rows, 3 * GDN_WIDTH), row),
            pl.BlockSpec((rows, GDN_WIDTH), lambda b, i: (b * ns + i, 3)),
            pl.BlockSpec((rows, LANES), row),
            pl.BlockSpec((SUBLANES, 3 * GDN_WIDTH), const),
            pl.BlockSpec((SUBLANES, LANES), const),
            pl.BlockSpec((1, GDN_HEAD_DIM), const),
        ],
        out_specs=pl.BlockSpec((rows, GDN_WIDTH), row),
        scratch_shapes=[
            pltpu.VMEM((rows + SUBLANES, 3 * GDN_WIDTH), F32),
            pltpu.VMEM((GDN_HEADS, GDN_HEAD_DIM, GDN_HEAD_DIM), F32),
        ],
        compiler_params=_cparams(("parallel", "arbitrary")),
        name="gated_deltanet",
    )(gdn_in, gdn_in, small, convw, gl, norm_g.reshape(1, -1).astype(F32))


def _cmul(ar, ai, br, bi):
    return ar * br - ai * bi, ar * bi + ai * br


def _s5_kernel(u_ref, bblk_ref, ccat_ref, apow_ref, d_ref, wglu_ref, o_ref,
               bu_ref, x_ref, carry_ref, *, tm):
    n = S5_LANES

    @pl.when(pl.program_id(1) == 0)
    def _():
        carry_ref[...] = jnp.zeros_like(carry_ref)

    u = u_ref[...]
    bu_ref[...] = _dot(u.astype(BF16), bblk_ref[...])

    def group(gidx, carry):
        c_re, c_im = carry
        r0 = pl.multiple_of(gidx * SUBLANES, SUBLANES)
        x_re = bu_ref[pl.ds(r0, SUBLANES), 0:n]
        x_im = bu_ref[pl.ds(r0, SUBLANES), n:2 * n]
        for lvl, d in enumerate((1, 2, 4)):
            a_re = apow_ref[lvl * 2 * SUBLANES:lvl * 2 * SUBLANES + SUBLANES, :]
            a_im = apow_ref[lvl * 2 * SUBLANES + SUBLANES:(lvl + 1) * 2 * SUBLANES, :]
            s_re = pltpu.roll(x_re, d, 0)
            s_im = pltpu.roll(x_im, d, 0)
            t_re, t_im = _cmul(a_re, a_im, s_re, s_im)
            x_re = x_re + t_re
            x_im = x_im + t_im
        p_re = apow_ref[6 * SUBLANES:7 * SUBLANES, :]
        p_im = apow_ref[7 * SUBLANES:8 * SUBLANES, :]
        t_re, t_im = _cmul(p_re, p_im, c_re, c_im)
        x_re = x_re + t_re
        x_im = x_im + t_im
        x_ref[pl.ds(r0, SUBLANES), 0:n] = x_re
        x_ref[pl.ds(r0, SUBLANES), n:2 * n] = x_im
        return x_re[SUBLANES - 1:SUBLANES, :], x_im[SUBLANES - 1:SUBLANES, :]

    c_re, c_im = lax.fori_loop(0, tm // SUBLANES, group,
                               (carry_ref[0:1, :], carry_ref[1:2, :]))
    carry_ref[0:1, :] = c_re
    carry_ref[1:2, :] = c_im

    yv = _dot(x_ref[...].astype(BF16), ccat_ref[...]) + d_ref[...] * u
    yv = 0.5 * yv * (1.0 + jnp.tanh(0.7978845608028654 * (yv + 0.044715 * (yv * yv * yv))))
    z = _dot(yv.astype(BF16), wglu_ref[...])
    o_ref[...] = yv * jax.nn.sigmoid(z)


def _s5_params(lam_re, lam_im, log_dt, b_re, b_im, c_re, c_im):
    f32 = F32
    lre, lim = lam_re.astype(f32), lam_im.astype(f32)
    dt = jnp.exp(log_dt.astype(f32))[:, None]
    mag = jnp.exp(lre * dt)
    ab_re, ab_im = mag * jnp.cos(lim * dt), mag * jnp.sin(lim * dt)
    num_re, num_im = ab_re - 1.0, ab_im
    den = lre * lre + lim * lim
    coef_re = (num_re * lre + num_im * lim) / den
    coef_im = (num_im * lre - num_re * lim) / den
    br, bi = b_re.astype(f32), b_im.astype(f32)
    bb_re = coef_re[..., None] * br - coef_im[..., None] * bi
    bb_im = coef_re[..., None] * bi + coef_im[..., None] * br
    eye = jnp.eye(S5_GROUPS, dtype=f32)
    blk_re = jnp.einsum('gph,gk->ghkp', bb_re, eye).reshape(S5_WIDTH, S5_LANES)
    blk_im = jnp.einsum('gph,gk->ghkp', bb_im, eye).reshape(S5_WIDTH, S5_LANES)
    bblk = jnp.concatenate([blk_re, blk_im], axis=1).astype(BF16)
    cb_re = jnp.einsum('ghp,gk->gpkh', c_re.astype(f32), eye).reshape(S5_LANES, S5_WIDTH)
    cb_im = jnp.einsum('ghp,gk->gpkh', c_im.astype(f32), eye).reshape(S5_LANES, S5_WIDTH)
    ccat = jnp.concatenate([cb_re, -cb_im], axis=0).astype(BF16)
    a1 = (ab_re.reshape(1, -1), ab_im.reshape(1, -1))
    pows = [a1]
    for _ in range(SUBLANES - 1):
        pows.append(_cmul(pows[-1][0], pows[-1][1], a1[0], a1[1]))
    rid = jnp.arange(SUBLANES)[:, None]
    rows = []
    for d in (1, 2, 4):
        mask = (rid >= d).astype(f32)
        rows.append(mask * pows[d - 1][0])
        rows.append(mask * pows[d - 1][1])
    rows.append(jnp.concatenate([pows[r][0] for r in range(SUBLANES)], axis=0))
    rows.append(jnp.concatenate([pows[r][1] for r in range(SUBLANES)], axis=0))
    apow = jnp.concatenate(rows, axis=0)
    return bblk, ccat, apow


def _s5_mixer(cu, lam_re, lam_im, log_dt, b_re, b_im, c_re, c_im, d, w_glu, *, batch, seq, tm=256):
    t = batch * seq
    ns = seq // tm
    bblk, ccat, apow = _s5_params(lam_re, lam_im, log_dt, b_re, b_im, c_re, c_im)
    row = lambda b, i: (b * ns + i, 0)
    const = lambda b, i: (0, 0)
    return pl.pallas_call(
        functools.partial(_s5_kernel, tm=tm),
        out_shape=jax.ShapeDtypeStruct((t, S5_WIDTH), F32),
        grid=(batch, ns),
        in_specs=[
            pl.BlockSpec((tm, S5_WIDTH), row),
            pl.BlockSpec(bblk.shape, const),
            pl.BlockSpec(ccat.shape, const),
            pl.BlockSpec(apow.shape, const),
            pl.BlockSpec((1, S5_WIDTH), const),
            pl.BlockSpec((S5_WIDTH, S5_WIDTH), const),
        ],
        out_specs=pl.BlockSpec((tm, S5_WIDTH), row),
        scratch_shapes=[
            pltpu.VMEM((tm, 2 * S5_LANES), F32),
            pltpu.VMEM((tm, 2 * S5_LANES), F32),
            pltpu.VMEM((SUBLANES, S5_LANES), F32),
        ],
        compiler_params=_cparams(("parallel", "arbitrary")),
        name="s5_mixer",
    )(cu, bblk, ccat, apow, d.reshape(1, -1).astype(F32), w_glu.astype(BF16))


def _route_rows(lt, n_tok):
    g = [lt[r:r + 1, :] for r in range(N_EXPERT_GROUPS)]
    gm = functools.reduce(jnp.maximum, g)
    gsum = functools.reduce(lambda a, b: a + b, [jnp.exp(x - gm) for x in g])
    g_p = 1.0 / gsum
    taken = jnp.zeros_like(gm) > 1.0
    g_hot = []
    for x in g:
        hit = jnp.logical_and(x == gm, jnp.logical_not(taken))
        g_hot.append(hit)
        taken = jnp.logical_or(taken, hit)
    e_sel = []
    for j in range(EXPERTS_PER_GROUP):
        acc = jnp.zeros_like(gm)
        for gi in range(N_EXPERT_GROUPS):
            r = 8 + gi * EXPERTS_PER_GROUP + j
            acc = acc + jnp.where(g_hot[gi], lt[r:r + 1, :], 0.0)
        e_sel.append(acc)
    m1 = functools.reduce(jnp.maximum, e_sel)
    taken = jnp.zeros_like(gm) > 1.0
    hot1 = []
    for x in e_sel:
        hit = jnp.logical_and(x == m1, jnp.logical_not(taken))
        hot1.append(hit)
        taken = jnp.logical_or(taken, hit)
    rest = [jnp.where(hh, NEG_BIG, x) for hh, x in zip(hot1, e_sel)]
    m2 = functools.reduce(jnp.maximum, rest)
    taken = jnp.zeros_like(gm) > 1.0
    hot2 = []
    for hh, x in zip(hot1, rest):
        hit = jnp.logical_and(jnp.logical_and(x == m2, jnp.logical_not(hh)), jnp.logical_not(taken))
        hot2.append(hit)
        taken = jnp.logical_or(taken, hit)
    e2 = jnp.exp(m2 - m1)
    w1 = g_p / (1.0 + e2)
    w2 = g_p * e2 / (1.0 + e2)
    rows = []
    for gi in range(N_EXPERT_GROUPS):
        for j in range(EXPERTS_PER_GROUP):
            val = jnp.where(hot1[j], w1, 0.0) + jnp.where(hot2[j], w2, 0.0)
            rows.append(jnp.where(g_hot[gi], val, 0.0))
    return jnp.concatenate(rows, axis=0)


def _outproj_kernel(h_ref, yaT_ref, yb_ref, yc_ref, wa_ref, wb_ref, wc_ref, g_ref, b_ref,
                    wrT_ref, br_ref, h1_ref, h1b_ref, comb_ref, *, tm):
    ya = yaT_ref[0].T
    mix = _dot(ya.astype(BF16), wa_ref[...])
    mix = mix + _dot(yb_ref[...].astype(BF16), wb_ref[...])
    mix = mix + _dot(yc_ref[...].astype(BF16), wc_ref[...])
    h1 = _layer_norm(ALPHA * h_ref[...] + mix, g_ref[...], b_ref[...])
    h1_ref[...] = h1
    h1b_ref[...] = h1.astype(BF16)
    h_hi, h_lo = _split_bf16(h1)
    w_hi = wrT_ref[0:LANES, :]
    w_lo = wrT_ref[LANES:2 * LANES, :]
    lt = _dot_nt(w_hi, h_hi) + _dot_nt(w_hi, h_lo) + _dot_nt(w_lo, h_hi) + br_ref[...]
    comb = _route_rows(lt, tm)
    combp = jnp.concatenate([comb, jnp.zeros((LANES - N_EXPERTS, tm), F32)], axis=0)
    comb_ref[...] = combp.T


def _out_projection(h, yaT, yb, yc, w_out, ln_g, ln_b, w_grp, b_grp, w_exp, b_exp,
                    *, batch, seq, tm=512):
    t = batch * seq
    nt = seq // tm
    wa = w_out[0:DA_WIDTH].astype(BF16)
    wb = w_out[DA_WIDTH:DA_WIDTH + GDN_WIDTH].astype(BF16)
    wc = w_out[DA_WIDTH + GDN_WIDTH:].astype(BF16)
    wr = jnp.zeros((D_MODEL, LANES), F32)
    wr = wr.at[:, 0:N_EXPERT_GROUPS].set(w_grp.astype(F32)).at[:, 8:8 + N_EXPERTS].set(w_exp.astype(F32))
    wrT = wr.T
    wr_hi = wrT.astype(BF16)
    wr_lo = (wrT - wr_hi.astype(F32)).astype(BF16)
    wr_cat = jnp.concatenate([wr_hi, wr_lo], axis=0)
    br = jnp.zeros((LANES, 1), F32)
    br = br.at[0:N_EXPERT_GROUPS, 0].set(b_grp.astype(F32)).at[8:8 + N_EXPERTS, 0].set(b_exp.astype(F32))
    row = lambda b, i: (b * nt + i, 0)
    const = lambda b, i: (0, 0)
    return pl.pallas_call(
        functools.partial(_outproj_kernel, tm=tm),
        out_shape=[
            jax.ShapeDtypeStruct((t, D_MODEL), F32),
            jax.ShapeDtypeStruct((t, D_MODEL), BF16),
            jax.ShapeDtypeStruct((t, LANES), F32),
        ],
        grid=(batch, nt),
        in_specs=[
            pl.BlockSpec((tm, D_MODEL), row),
            pl.BlockSpec((1, DA_WIDTH, tm), lambda b, i: (b, 0, i)),
            pl.BlockSpec((tm, GDN_WIDTH), row),
            pl.BlockSpec((tm, S5_WIDTH), row),
            pl.BlockSpec(wa.shape, const),
            pl.BlockSpec(wb.shape, const),
            pl.BlockSpec(wc.shape, const),
            pl.BlockSpec((1, D_MODEL), const),
            pl.BlockSpec((1, D_MODEL), const),
            pl.BlockSpec(wr_cat.shape, const),
            pl.BlockSpec((LANES, 1), const),
        ],
        out_specs=[
            pl.BlockSpec((tm, D_MODEL), row),
            pl.BlockSpec((tm, D_MODEL), row),
            pl.BlockSpec((tm, LANES), row),
        ],
        compiler_params=_cparams(("parallel", "parallel")),
        name="out_projection_router",
    )(h, yaT, yb, yc, wa, wb, wc, ln_g.reshape(1, -1), ln_b.reshape(1, -1), wr_cat, br)


def _moe_kernel(hb_ref, h1_ref, comb_ref, w1_ref, w3_ref, w2_ref, g_ref, b_ref, o_ref, acc_ref):
    e = pl.program_id(1)

    @pl.when(e == 0)
    def _():
        acc_ref[...] = jnp.zeros_like(acc_ref)

    x = hb_ref[...]
    lane = lax.broadcasted_iota(jnp.int32, (1, LANES), 1)
    c = jnp.sum(jnp.where(lane == e, comb_ref[...], 0.0), axis=1, keepdims=True)
    a = _dot(x, w1_ref[0])
    b = _dot(x, w3_ref[0])
    hid = a * jax.nn.sigmoid(a) * b
    acc_ref[...] += c * _dot(hid.astype(BF16), w2_ref[0])

    @pl.when(e == N_EXPERTS - 1)
    def _():
        o_ref[...] = _layer_norm(ALPHA * h1_ref[...] + acc_ref[...], g_ref[...], b_ref[...])


def _moe(h1, h1b, comb, w1, w3, w2, ln_g, ln_b, *, tm=1024):
    t = h1.shape[0]
    nt = t // tm
    row = lambda i, e: (i, 0)
    const = lambda i, e: (0, 0)
    return pl.pallas_call(
        _moe_kernel,
        out_shape=jax.ShapeDtypeStruct((t, D_MODEL), F32),
        grid=(nt, N_EXPERTS),
        in_specs=[
            pl.BlockSpec((tm, D_MODEL), row),
            pl.BlockSpec((tm, D_MODEL), row),
            pl.BlockSpec((tm, LANES), row),
            pl.BlockSpec((1, D_MODEL, D_EXPERT), lambda i, e: (e, 0, 0)),
            pl.BlockSpec((1, D_MODEL, D_EXPERT), lambda i, e: (e, 0, 0)),
            pl.BlockSpec((1, D_EXPERT, D_MODEL), lambda i, e: (e, 0, 0)),
            pl.BlockSpec((1, D_MODEL), const),
            pl.BlockSpec((1, D_MODEL), const),
        ],
        out_specs=pl.BlockSpec((tm, D_MODEL), row),
        scratch_shapes=[pltpu.VMEM((tm, D_MODEL), F32)],
        compiler_params=_cparams(("parallel", "arbitrary")),
        name="moe_ffn",
    )(h1b, h1, comb, w1, w3, w2, ln_g.reshape(1, -1), ln_b.reshape(1, -1))


def _split_w_in(w):
    o = 0
    wq = w[:, o:o + DA_WIDTH]; o += DA_WIDTH
    wk = w[:, o:o + DA_WIDTH]; o += DA_WIDTH
    wv = w[:, o:o + DA_WIDTH]; o += DA_WIDTH
    wg = w[:, o:o + 4 * GDN_WIDTH]; o += 4 * GDN_WIDTH
    wbeta = w[:, o:o + GDN_HEADS]; o += GDN_HEADS
    wa = w[:, o:o + GDN_HEADS]; o += GDN_HEADS
    wc = w[:, o:o + S5_WIDTH]
    wsm = jnp.zeros((D_MODEL, LANES), w.dtype)
    wsm = wsm.at[:, BETA_LANE0:BETA_LANE0 + GDN_HEADS].set(wbeta)
    wsm = wsm.at[:, A_LANE0:A_LANE0 + GDN_HEADS].set(wa)
    kd = 2 * DA_HEAD_DIM
    wk_pad = jnp.zeros((D_MODEL, DA_HEADS, K_PAD), w.dtype)
    wk_pad = wk_pad.at[:, :, 0:kd].set(wk.reshape(D_MODEL, DA_HEADS, kd)).reshape(D_MODEL, DA_HEADS * K_PAD)
    kone = jnp.zeros((1, DA_HEADS, K_PAD), F32).at[:, :, kd].set(1.0).reshape(1, DA_HEADS * K_PAD)
    return (wq.T.astype(BF16), wv.T.astype(BF16), wk_pad.astype(BF16), kone, wg.astype(BF16),
            wsm.astype(BF16), wc.astype(BF16))


def kernel(x, ln_in_g, ln_in_b, w_in, w_out, lam_q1, lam_k1, lam_q2, lam_k2, diff_norm_g, dn_conv_w, dn_a_log, dn_dt_bias, dn_norm_g, s5_lambda_re, s5_lambda_im, s5_log_dt, s5_b_re, s5_b_im, s5_c_re, s5_c_im, s5_d, s5_w_glu, ln1_g, ln1_b, moe_w_grp, moe_b_grp, moe_w_exp, moe_b_exp, moe_w1, moe_w3, moe_w2, ln2_g, ln2_b):
    batch, seq, d = x.shape
    h = x.reshape(batch * seq, d)
    for l in range(DEPTH):
        lam_init = 0.8 - 0.6 * math.exp(-0.3 * l)
        wts = _split_w_in(w_in[l])
        outs = _in_projection(h, ln_in_g, ln_in_b, wts, batch=batch, seq=seq, apply_ln=(l == 0))
        if l == 0:
            h, qT, vT, k, gdn_in, small, cu = outs
        else:
            qT, vT, k, gdn_in, small, cu = outs
        lam = (jnp.exp(jnp.sum(lam_q1[l] * lam_k1[l])) - jnp.exp(jnp.sum(lam_q2[l] * lam_k2[l]))
               ).astype(F32) + lam_init
        yaT = _diff_attention(lam, qT, k, vT, diff_norm_g[l].astype(F32), lam_init=lam_init)
        yb = _gdn_mixer(gdn_in, small, dn_conv_w[l], dn_a_log[l], dn_dt_bias[l], dn_norm_g[l],
                        batch=batch, seq=seq)
        yc = _s5_mixer(cu, s5_lambda_re[l], s5_lambda_im[l], s5_log_dt[l], s5_b_re[l], s5_b_im[l],
                       s5_c_re[l], s5_c_im[l], s5_d[l], s5_w_glu[l], batch=batch, seq=seq)
        h1, h1b, comb = _out_projection(h, yaT, yb, yc, w_out[l], ln1_g[l], ln1_b[l],
                                        moe_w_grp[l], moe_b_grp[l], moe_w_exp[l], moe_b_exp[l],
                                        batch=batch, seq=seq)
        h = _moe(h1, h1b, comb, moe_w1[l].astype(BF16), moe_w3[l].astype(BF16),
                 moe_w2[l].astype(BF16), ln2_g[l], ln2_b[l])
    return h.reshape(batch, seq, d)
```

```python
import functools
import math

import jax
import jax.numpy as jnp
from jax import lax
from jax.experimental import pallas as pl
from jax.experimental.pallas import tpu as pltpu

F32 = jnp.float32
BF16 = jnp.bfloat16

D_MODEL = 1024
DEPTH = 2
CHUNK = 64
DA_HEADS = 6
DA_HEAD_DIM = 32
DA_V_DIM = 64
DA_WIDTH = 384
GDN_HEADS = 6
GDN_HEAD_DIM = 64
GDN_WIDTH = 384
CONV_K = 4
S5_GROUP_DIM = 16
S5_GROUPS = 16
S5_WIDTH = 256
S5_STATE = 64
S5_LANES = S5_GROUPS * S5_STATE
N_EXPERT_GROUPS = 4
EXPERTS_PER_GROUP = 4
N_EXPERTS = 16
D_EXPERT = 512
ALPHA = (2 * DEPTH) ** 0.25
LN_EPS = 1e-5
RMS_EPS = 1e-6
LOG2E = 1.4426950408889634

V7X_VMEM_LIMIT_BYTES = 56 * 1024 * 1024
SUBLANES = 8
LANES = 128
NEG_BIG = -1e30
FAST_MAX_LOG2 = 100.0
K_PAD = LANES

BETA_LANE0 = 0
A_LANE0 = 8


def _cparams(sem):
    return pltpu.CompilerParams(dimension_semantics=sem, vmem_limit_bytes=V7X_VMEM_LIMIT_BYTES)


def _layer_norm(x, g, b):
    mu = jnp.mean(x, axis=-1, keepdims=True)
    xc = x - mu
    var = jnp.mean(xc * xc, axis=-1, keepdims=True)
    return xc * lax.rsqrt(var + LN_EPS) * g + b


def _dot(a, b):
    return jnp.dot(a, b, preferred_element_type=F32)


def _dot_nt(a, b):
    return lax.dot_general(a, b, (((1,), (1,)), ((), ())), preferred_element_type=F32)


def _dot_tn(a, b):
    return lax.dot_general(a, b, (((0,), (0,)), ((), ())), preferred_element_type=F32)


def _proj_kernel(x_ref, g_ref, b_ref, wq_ref, wv_ref, wk_ref, kone_ref, wg_ref, wsm_ref, wc_ref,
                 *out_refs, apply_ln, q_scale):
    if apply_ln:
        h_ref, qT_ref, vT_ref, k_ref, gdn_ref, small_ref, cu_ref = out_refs
        h = _layer_norm(x_ref[...], g_ref[...], b_ref[...])
        h_ref[...] = h
    else:
        qT_ref, vT_ref, k_ref, gdn_ref, small_ref, cu_ref = out_refs
        h = x_ref[...]
    hb = h.astype(BF16)
    qT_ref[0] = (_dot(hb, wq_ref[...]) * q_scale).T.astype(BF16)
    vT_ref[0] = _dot(hb, wv_ref[...]).T.astype(BF16)
    k_ref[0] = (_dot(hb, wk_ref[...]) + kone_ref[...]).astype(BF16)
    gdn_ref[...] = _dot(hb, wg_ref[...])
    small_ref[...] = _dot(hb, wsm_ref[...])
    cu_ref[...] = _dot(hb, wc_ref[...])


def _in_projection(x2d, g, b, wts, *, batch, seq, apply_ln, tm=512):
    t = batch * seq
    nt = seq // tm
    wqT, wvT, wk, kone, wg, wsm, wc = wts
    kw = DA_HEADS * K_PAD
    q_scale = (DA_HEAD_DIM ** -0.5) * LOG2E
    row = lambda bi, i: (bi * nt + i, 0)
    const = lambda bi, i: (0, 0)
    out_shape = [
        jax.ShapeDtypeStruct((batch, DA_WIDTH, seq), BF16),
        jax.ShapeDtypeStruct((batch, DA_WIDTH, seq), BF16),
        jax.ShapeDtypeStruct((batch, seq, kw), BF16),
        jax.ShapeDtypeStruct((t, 4 * GDN_WIDTH), F32),
        jax.ShapeDtypeStruct((t, LANES), F32),
        jax.ShapeDtypeStruct((t, S5_WIDTH), F32),
    ]
    out_specs = [
        pl.BlockSpec((1, DA_WIDTH, tm), lambda bi, i: (bi, 0, i)),
        pl.BlockSpec((1, DA_WIDTH, tm), lambda bi, i: (bi, 0, i)),
        pl.BlockSpec((1, tm, kw), lambda bi, i: (bi, i, 0)),
        pl.BlockSpec((tm, 4 * GDN_WIDTH), row),
        pl.BlockSpec((tm, LANES), row),
        pl.BlockSpec((tm, S5_WIDTH), row),
    ]
    if apply_ln:
        out_shape = [jax.ShapeDtypeStruct((t, D_MODEL), F32)] + out_shape
        out_specs = [pl.BlockSpec((tm, D_MODEL), row)] + out_specs
    in_specs = [
        pl.BlockSpec((tm, D_MODEL), row),
        pl.BlockSpec((1, D_MODEL), const),
        pl.BlockSpec((1, D_MODEL), const),
        pl.BlockSpec(wqT.shape, const),
        pl.BlockSpec(wvT.shape, const),
        pl.BlockSpec(wk.shape, const),
        pl.BlockSpec(kone.shape, const),
        pl.BlockSpec(wg.shape, const),
        pl.BlockSpec(wsm.shape, const),
        pl.BlockSpec(wc.shape, const),
    ]
    return pl.pallas_call(
        functools.partial(_proj_kernel, apply_ln=apply_ln, q_scale=q_scale),
        out_shape=out_shape,
        grid=(batch, nt),
        in_specs=in_specs,
        out_specs=out_specs,
        compiler_params=_cparams(("parallel", "parallel")),
        name="in_projection_ln" if apply_ln else "in_projection",
    )(x2d, g.reshape(1, -1), b.reshape(1, -1), wqT, wvT, wk, kone, wg, wsm, wc)


def _attn_kernel(lam_ref, qT_ref, k_ref, vT_ref, g_ref, o_ref,
                 qbd_ref, pa_ref, pb_ref, m_ref, l_ref, acc_ref, cm_ref, *, tq, tk, out_scale):
    i = pl.program_id(2)
    dh = DA_HEAD_DIM
    ref_row = 2 * dh
    assert tq == 2 * tk

    def qk(j):
        start = pl.multiple_of(j * tk, tk)
        return _dot(k_ref[0, pl.ds(start, tk), :], qbd_ref[...])

    def pv(j, p):
        start = pl.multiple_of(j * tk, tk)
        return _dot(vT_ref[0, :, pl.ds(start, tk)], p.astype(BF16))

    def init_stats():
        m_ref[...] = jnp.full_like(m_ref, NEG_BIG)
        l_ref[...] = jnp.zeros_like(l_ref)
        acc_ref[...] = jnp.zeros_like(acc_ref)

    def exact_step(j, mask):
        s = qk(j)
        if mask is not None:
            s = jnp.where(mask, s, NEG_BIG)
        m_old = m_ref[...]
        m_new = jnp.maximum(m_old, jnp.max(s, axis=0, keepdims=True))
        alpha = jnp.exp2(m_old - m_new)
        p = jnp.exp2(s - m_new)
        l_ref[...] = alpha * l_ref[...] + jnp.sum(p, axis=0, keepdims=True)
        m_ref[...] = m_new
        acc_ref[...] = alpha * acc_ref[...] + pv(j, p)

    def score_exp(j, mask=None):
        s = qk(j)
        if mask is not None:
            s = jnp.where(mask, s, NEG_BIG)
        cm_ref[...] = jnp.maximum(cm_ref[...], jnp.max(s, axis=0, keepdims=True))
        p = jnp.exp2(s)
        l_ref[...] += jnp.sum(p, axis=0, keepdims=True)
        return p.astype(BF16)

    def accumulate(p_ref, j):
        acc_ref[...] += pv(j, p_ref[...])

    kc = lax.broadcasted_iota(jnp.int32, (tk, 1), 0) // CHUNK
    col = lax.broadcasted_iota(jnp.int32, (1, 2 * tq), 1)
    qc = jnp.where(col >= tq, col - tq, col) // CHUNK

    def diag_mask(d):
        return (kc + d * (tk // CHUNK)) <= qc

    qbd_ref[...] = jnp.zeros_like(qbd_ref)
    q = qT_ref[0]
    qbd_ref[0:dh, 0:tq] = q[0:dh]
    qbd_ref[dh:2 * dh, tq:2 * tq] = q[dh:2 * dh]
    init_stats()

    off_diag = i > 0
    exact_step(2 * i, diag_mask(0))

    m = m_ref[...]
    mref = m.astype(BF16)
    fix = jnp.exp2(m - mref.astype(F32))
    l_ref[...] = l_ref[...] * fix
    acc_ref[...] = acc_ref[...] * fix
    qbd_ref[ref_row:ref_row + 16, :] = jnp.broadcast_to(-mref, (16, 2 * tq))
    cm_ref[...] = jnp.zeros_like(cm_ref)

    pb_ref[...] = score_exp(2 * i + 1, diag_mask(1))
    accumulate(pb_ref, 2 * i + 1)

    @pl.when(off_diag)
    def _():
        pa_ref[...] = score_exp(0)

        def quad(t, lookahead):
            pb_ref[...] = score_exp(t + 1)
            accumulate(pa_ref, t)
            pa_ref[...] = score_exp(t + 2)
            accumulate(pb_ref, t + 1)
            pb_ref[...] = score_exp(t + 3)
            accumulate(pa_ref, t + 2)
            if lookahead:
                pa_ref[...] = score_exp(t + 4)
            accumulate(pb_ref, t + 3)

        def quad_body(r, carry):
            quad(4 * r, True)
            return carry

        n_quads = (i - 1) // 2
        lax.fori_loop(0, n_quads, quad_body, 0)
        t = 4 * n_quads

        @pl.when(i % 2 == 1)
        def _():
            pb_ref[...] = score_exp(t + 1)
            accumulate(pa_ref, t)
            accumulate(pb_ref, t + 1)

        @pl.when(i % 2 == 0)
        def _():
            quad(t, False)

    @pl.when(jnp.max(cm_ref[...]) > FAST_MAX_LOG2)
    def _():
        qbd_ref[ref_row:ref_row + 16, :] = jnp.zeros((16, 2 * tq), BF16)
        init_stats()

        def body(j, carry):
            exact_step(j, None)
            return carry

        lax.fori_loop(0, 2 * i, body, 0)
        for d in range(2):
            exact_step(2 * i + d, diag_mask(d))

    l = l_ref[...]
    acc = acc_ref[...]
    lam = lam_ref[0]
    o = acc[:, 0:tq] / l[:, 0:tq] - lam * (acc[:, tq:2 * tq] / l[:, tq:2 * tq])
    ms = jnp.mean(o * o, axis=0, keepdims=True)
    o_ref[0] = o * lax.rsqrt(ms + RMS_EPS) * g_ref[...] * out_scale


def _diff_attention(lam, qT, k, vT, norm_g, *, lam_init, tq=512, tk=256):
    batch, _, seq = qT.shape
    nq = seq // tq
    dv = DA_V_DIM
    return pl.pallas_call(
        functools.partial(_attn_kernel, tq=tq, tk=tk, out_scale=1.0 - lam_init),
        out_shape=jax.ShapeDtypeStruct((batch, DA_WIDTH, seq), F32),
        grid=(batch, DA_HEADS, nq),
        in_specs=[
            pl.BlockSpec(memory_space=pltpu.SMEM),
            pl.BlockSpec((1, dv, tq), lambda b, h, i: (b, h, i)),
            pl.BlockSpec((1, seq, LANES), lambda b, h, i: (b, 0, h)),
            pl.BlockSpec((1, dv, seq), lambda b, h, i: (b, h, 0)),
            pl.BlockSpec((dv, 1), lambda b, h, i: (0, 0)),
        ],
        out_specs=pl.BlockSpec((1, dv, tq), lambda b, h, i: (b, h, i)),
        scratch_shapes=[
            pltpu.VMEM((LANES, 2 * tq), BF16),
            pltpu.VMEM((tk, 2 * tq), BF16),
            pltpu.VMEM((tk, 2 * tq), BF16),
            pltpu.VMEM((1, 2 * tq), F32),
            pltpu.VMEM((1, 2 * tq), F32),
            pltpu.VMEM((dv, 2 * tq), F32),
            pltpu.VMEM((1, 2 * tq), F32),
        ],
        compiler_params=_cparams(("parallel", "parallel", "parallel")),
        name="diff_attention",
    )(lam.reshape(1), qT, k, vT, norm_g.reshape(dv, 1))


def _split_bf16(x):
    hi = x.astype(BF16)
    lo = (x - hi.astype(F32)).astype(BF16)
    return hi, lo


def _mm_rhs_split(a, b):
    n = b.shape[1]
    hi, lo = _split_bf16(b)
    r = _dot(a.astype(BF16), jnp.concatenate([hi, lo], axis=1))
    return r[:, 0:n] + r[:, n:2 * n]


def _gdn_kernel(qkv_ref, gate_ref, small_ref, convw_ref, gl_ref, ng_ref, o_ref,
                xbuf_ref, state_ref, *, rows):
    step_i = pl.program_id(1)
    dk = GDN_HEAD_DIM
    nch = rows // CHUNK
    halo = SUBLANES

    @pl.when(step_i == 0)
    def _():
        xbuf_ref[0:halo, :] = jnp.zeros((halo, 3 * GDN_WIDTH), F32)
        state_ref[...] = jnp.zeros_like(state_ref)

    xbuf_ref[halo:halo + rows, :] = qkv_ref[...]
    y = convw_ref[CONV_K - 1:CONV_K, :] * xbuf_ref[halo:halo + rows, :]
    for j in range(CONV_K - 1):
        off = halo - (CONV_K - 1) + j
        y = y + convw_ref[j:j + 1, :] * xbuf_ref[off:off + rows, :]
    xbuf_ref[0:halo, :] = xbuf_ref[rows:rows + halo, :]
    y = y * jax.nn.sigmoid(y)

    small = small_ref[...]
    beta_all = jax.nn.sigmoid(small)
    sp_in = small + gl_ref[1:2, :]
    softplus = jnp.maximum(sp_in, 0.0) + jnp.log(1.0 + jnp.exp(-jnp.abs(sp_in)))
    g_all = gl_ref[0:1, :] * softplus

    ri = lax.broadcasted_iota(jnp.int32, (CHUNK, CHUNK), 0)
    ci = lax.broadcasted_iota(jnp.int32, (CHUNK, CHUNK), 1)
    tri = ri >= ci
    strict = ri > ci
    tril_f = tri.astype(F32)

    gc_parts = []
    for c in range(nch):
        gch = g_all[c * CHUNK:(c + 1) * CHUNK, :]
        gc_parts.append(jnp.dot(tril_f, gch, preferred_element_type=F32,
                                precision=lax.Precision.HIGHEST))
    gc_all = jnp.concatenate(gc_parts, axis=0) if nch > 1 else gc_parts[0]
    pad = (-rows) % LANES
    gc_sq = jnp.concatenate([gc_all, jnp.zeros((pad, LANES), F32)], axis=0) if pad else gc_all
    gcT = gc_sq.T

    gate = gate_ref[...]
    ng = ng_ref[...]
    heads = range(GDN_HEADS)
    items = [(c, hh) for c in range(nch) for hh in heads]
    qs, ks, kbs, kbfs, rhss, decays, egs, glasts, gcols = {}, {}, {}, {}, {}, {}, {}, {}, {}
    for it in items:
        c, hh = it
        r0 = c * CHUNK
        q = y[r0:r0 + CHUNK, hh * dk:(hh + 1) * dk]
        k = y[r0:r0 + CHUNK, GDN_WIDTH + hh * dk:GDN_WIDTH + (hh + 1) * dk]
        v = y[r0:r0 + CHUNK, 2 * GDN_WIDTH + hh * dk:2 * GDN_WIDTH + (hh + 1) * dk]
        q = q * lax.rsqrt(jnp.sum(q * q, axis=-1, keepdims=True) + RMS_EPS) * (dk ** -0.5)
        k = k * lax.rsqrt(jnp.sum(k * k, axis=-1, keepdims=True) + RMS_EPS)
        beta = beta_all[r0:r0 + CHUNK, BETA_LANE0 + hh:BETA_LANE0 + hh + 1]
        gcol = gc_all[r0:r0 + CHUNK, A_LANE0 + hh:A_LANE0 + hh + 1]
        grow = gcT[A_LANE0 + hh:A_LANE0 + hh + 1, r0:r0 + CHUNK]
        glasts[it] = gcT[A_LANE0 + hh:A_LANE0 + hh + 1, r0 + CHUNK - 1:r0 + CHUNK]
        decays[it] = jnp.where(tri, jnp.exp(jnp.where(tri, gcol - grow, 0.0)), 0.0)
        eg = jnp.exp(gcol)
        kb = k * beta
        qs[it], ks[it], kbs[it], kbfs[it], egs[it], gcols[it] = q, k, kb, k.astype(BF16), eg, gcol
        rhss[it] = jnp.concatenate([v * beta, kb * eg], axis=1)

    kk = {it: _dot_nt(kbs[it].astype(BF16), kbfs[it]) for it in items}
    qk = {it: _dot_nt(qs[it].astype(BF16), kbfs[it]) for it in items}
    lm = {it: jnp.where(strict, kk[it] * decays[it], 0.0) for it in items}
    a_intra = {it: qk[it] * decays[it] for it in items}
    xs = {it: rhss[it] - _mm_rhs_split(lm[it], rhss[it]) for it in items}
    ps = lm
    for _ in range(5):
        ps = {it: _mm_rhs_split(ps[it], ps[it]) for it in items}
        xs = {it: xs[it] + _mm_rhs_split(ps[it], xs[it]) for it in items}

    state = [state_ref[hh] for hh in heads]
    for c in range(nch):
        r0 = c * CHUNK
        stb = [state[hh].astype(BF16) for hh in heads]
        ws = [_dot(xs[(c, hh)][:, dk:2 * dk].astype(BF16), stb[hh]) for hh in heads]
        qst = [_dot((qs[(c, hh)] * egs[(c, hh)]).astype(BF16), stb[hh]) for hh in heads]
        vn = [(xs[(c, hh)][:, 0:dk] - ws[hh]).astype(BF16) for hh in heads]
        av = [_dot(a_intra[(c, hh)].astype(BF16), vn[hh]) for hh in heads]
        kv = [_dot_tn((ks[(c, hh)] * jnp.exp(glasts[(c, hh)] - gcols[(c, hh)])).astype(BF16), vn[hh])
              for hh in heads]
        for hh in heads:
            state[hh] = state[hh] * jnp.exp(glasts[(c, hh)]) + kv[hh]
            o = qst[hh] + av[hh]
            ms = jnp.mean(o * o, axis=-1, keepdims=True)
            on = o * lax.rsqrt(ms + RMS_EPS) * ng
            gt = gate[r0:r0 + CHUNK, hh * dk:(hh + 1) * dk]
            o_ref[r0:r0 + CHUNK, hh * dk:(hh + 1) * dk] = on * (gt * jax.nn.sigmoid(gt))
    for hh in heads:
        state_ref[hh] = state[hh]


def _gdn_mixer(gdn_in, small, conv_w, a_log, dt_bias, norm_g, *, batch, seq, rows=128):
    t = batch * seq
    ns = seq // rows
    gl = jnp.zeros((SUBLANES, LANES), F32)
    gl = gl.at[0, A_LANE0:A_LANE0 + GDN_HEADS].set(-jnp.exp(a_log.astype(F32)))
    gl = gl.at[1, A_LANE0:A_LANE0 + GDN_HEADS].set(dt_bias.astype(F32))
    convw = jnp.zeros((SUBLANES, 3 * GDN_WIDTH), F32).at[0:CONV_K].set(conv_w.astype(F32))
    row = lambda b, i: (b * ns + i, 0)
    const = lambda b, i: (0, 0)
    return pl.pallas_call(
        functools.partial(_gdn_kernel, rows=rows),
        out_shape=jax.ShapeDtypeStruct((t, GDN_WIDTH), F32),
        grid=(batch, ns),
        in_specs=[
            pl.BlockSpec((rows, 3 * GDN_WIDTH), row),
            pl.BlockSpec((rows, GDN_WIDTH), lambda b, i: (b * ns + i, 3)),
            pl.BlockSpec((rows, LANES), row),
            pl.BlockSpec((SUBLANES, 3 * GDN_WIDTH), const),
            pl.BlockSpec((SUBLANES, LANES), const),
            pl.BlockSpec((1, GDN_HEAD_DIM), const),
        ],
        out_specs=pl.BlockSpec((rows, GDN_WIDTH), row),
        scratch_shapes=[
            pltpu.VMEM((rows + SUBLANES, 3 * GDN_WIDTH), F32),
            pltpu.VMEM((GDN_HEADS, GDN_HEAD_DIM, GDN_HEAD_DIM), F32),
        ],
        compiler_params=_cparams(("parallel", "arbitrary")),
        name="gated_deltanet",
    )(gdn_in, gdn_in, small, convw, gl, norm_g.reshape(1, -1).astype(F32))


def _cmul(ar, ai, br, bi):
    return ar * br - ai * bi, ar * bi + ai * br


def _s5_kernel(u_ref, bblk_ref, ccat_ref, apow_ref, d_ref, wglu_ref, o_ref,
               bu_ref, x_ref, carry_ref, *, tm):
    n = S5_LANES

    @pl.when(pl.program_id(1) == 0)
    def _():
        carry_ref[...] = jnp.zeros_like(carry_ref)

    u = u_ref[...]
    bu_ref[...] = _dot(u.astype(BF16), bblk_ref[...])

    def group(gidx, carry):
        c_re, c_im = carry
        r0 = pl.multiple_of(gidx * SUBLANES, SUBLANES)
        x_re = bu_ref[pl.ds(r0, SUBLANES), 0:n]
        x_im = bu_ref[pl.ds(r0, SUBLANES), n:2 * n]
        for lvl, d in enumerate((1, 2, 4)):
            a_re = apow_ref[lvl * 2 * SUBLANES:lvl * 2 * SUBLANES + SUBLANES, :]
            a_im = apow_ref[lvl * 2 * SUBLANES + SUBLANES:(lvl + 1) * 2 * SUBLANES, :]
            s_re = pltpu.roll(x_re, d, 0)
            s_im = pltpu.roll(x_im, d, 0)
            t_re, t_im = _cmul(a_re, a_im, s_re, s_im)
            x_re = x_re + t_re
            x_im = x_im + t_im
        p_re = apow_ref[6 * SUBLANES:7 * SUBLANES, :]
        p_im = apow_ref[7 * SUBLANES:8 * SUBLANES, :]
        t_re, t_im = _cmul(p_re, p_im, c_re, c_im)
        x_re = x_re + t_re
        x_im = x_im + t_im
        x_ref[pl.ds(r0, SUBLANES), 0:n] = x_re
        x_ref[pl.ds(r0, SUBLANES), n:2 * n] = x_im
        return x_re[SUBLANES - 1:SUBLANES, :], x_im[SUBLANES - 1:SUBLANES, :]

    c_re, c_im = lax.fori_loop(0, tm // SUBLANES, group,
                               (carry_ref[0:1, :], carry_ref[1:2, :]))
    carry_ref[0:1, :] = c_re
    carry_ref[1:2, :] = c_im

    yv = _dot(x_ref[...].astype(BF16), ccat_ref[...]) + d_ref[...] * u
    yv = 0.5 * yv * (1.0 + jnp.tanh(0.7978845608028654 * (yv + 0.044715 * (yv * yv * yv))))
    z = _dot(yv.astype(BF16), wglu_ref[...])
    o_ref[...] = yv * jax.nn.sigmoid(z)


def _s5_params(lam_re, lam_im, log_dt, b_re, b_im, c_re, c_im):
    f32 = F32
    lre, lim = lam_re.astype(f32), lam_im.astype(f32)
    dt = jnp.exp(log_dt.astype(f32))[:, None]
    mag = jnp.exp(lre * dt)
    ab_re, ab_im = mag * jnp.cos(lim * dt), mag * jnp.sin(lim * dt)
    num_re, num_im = ab_re - 1.0, ab_im
    den = lre * lre + lim * lim
    coef_re = (num_re * lre + num_im * lim) / den
    coef_im = (num_im * lre - num_re * lim) / den
    br, bi = b_re.astype(f32), b_im.astype(f32)
    bb_re = coef_re[..., None] * br - coef_im[..., None] * bi
    bb_im = coef_re[..., None] * bi + coef_im[..., None] * br
    eye = jnp.eye(S5_GROUPS, dtype=f32)
    blk_re = jnp.einsum('gph,gk->ghkp', bb_re, eye).reshape(S5_WIDTH, S5_LANES)
    blk_im = jnp.einsum('gph,gk->ghkp', bb_im, eye).reshape(S5_WIDTH, S5_LANES)
    bblk = jnp.concatenate([blk_re, blk_im], axis=1).astype(BF16)
    cb_re = jnp.einsum('ghp,gk->gpkh', c_re.astype(f32), eye).reshape(S5_LANES, S5_WIDTH)
    cb_im = jnp.einsum('ghp,gk->gpkh', c_im.astype(f32), eye).reshape(S5_LANES, S5_WIDTH)
    ccat = jnp.concatenate([cb_re, -cb_im], axis=0).astype(BF16)
    a1 = (ab_re.reshape(1, -1), ab_im.reshape(1, -1))
    pows = [a1]
    for _ in range(SUBLANES - 1):
        pows.append(_cmul(pows[-1][0], pows[-1][1], a1[0], a1[1]))
    rid = jnp.arange(SUBLANES)[:, None]
    rows = []
    for d in (1, 2, 4):
        mask = (rid >= d).astype(f32)
        rows.append(mask * pows[d - 1][0])
        rows.append(mask * pows[d - 1][1])
    rows.append(jnp.concatenate([pows[r][0] for r in range(SUBLANES)], axis=0))
    rows.append(jnp.concatenate([pows[r][1] for r in range(SUBLANES)], axis=0))
    apow = jnp.concatenate(rows, axis=0)
    return bblk, ccat, apow


def _s5_mixer(cu, lam_re, lam_im, log_dt, b_re, b_im, c_re, c_im, d, w_glu, *, batch, seq, tm=256):
    t = batch * seq
    ns = seq // tm
    bblk, ccat, apow = _s5_params(lam_re, lam_im, log_dt, b_re, b_im, c_re, c_im)
    row = lambda b, i: (b * ns + i, 0)
    const = lambda b, i: (0, 0)
    return pl.pallas_call(
        functools.partial(_s5_kernel, tm=tm),
        out_shape=jax.ShapeDtypeStruct((t, S5_WIDTH), F32),
        grid=(batch, ns),
        in_specs=[
            pl.BlockSpec((tm, S5_WIDTH), row),
            pl.BlockSpec(bblk.shape, const),
            pl.BlockSpec(ccat.shape, const),
            pl.BlockSpec(apow.shape, const),
            pl.BlockSpec((1, S5_WIDTH), const),
            pl.BlockSpec((S5_WIDTH, S5_WIDTH), const),
        ],
        out_specs=pl.BlockSpec((tm, S5_WIDTH), row),
        scratch_shapes=[
            pltpu.VMEM((tm, 2 * S5_LANES), F32),
            pltpu.VMEM((tm, 2 * S5_LANES), F32),
            pltpu.VMEM((SUBLANES, S5_LANES), F32),
        ],
        compiler_params=_cparams(("parallel", "arbitrary")),
        name="s5_mixer",
    )(cu, bblk, ccat, apow, d.reshape(1, -1).astype(F32), w_glu.astype(BF16))


def _route_rows(lt, n_tok):
    g = [lt[r:r + 1, :] for r in range(N_EXPERT_GROUPS)]
    gm = functools.reduce(jnp.maximum, g)
    gsum = functools.reduce(lambda a, b: a + b, [jnp.exp(x - gm) for x in g])
    g_p = 1.0 / gsum
    taken = jnp.zeros_like(gm) > 1.0
    g_hot = []
    for x in g:
        hit = jnp.logical_and(x == gm, jnp.logical_not(taken))
        g_hot.append(hit)
        taken = jnp.logical_or(taken, hit)
    e_sel = []
    for j in range(EXPERTS_PER_GROUP):
        acc = jnp.zeros_like(gm)
        for gi in range(N_EXPERT_GROUPS):
            r = 8 + gi * EXPERTS_PER_GROUP + j
            acc = acc + jnp.where(g_hot[gi], lt[r:r + 1, :], 0.0)
        e_sel.append(acc)
    m1 = functools.reduce(jnp.maximum, e_sel)
    taken = jnp.zeros_like(gm) > 1.0
    hot1 = []
    for x in e_sel:
        hit = jnp.logical_and(x == m1, jnp.logical_not(taken))
        hot1.append(hit)
        taken = jnp.logical_or(taken, hit)
    rest = [jnp.where(hh, NEG_BIG, x) for hh, x in zip(hot1, e_sel)]
    m2 = functools.reduce(jnp.maximum, rest)
    taken = jnp.zeros_like(gm) > 1.0
    hot2 = []
    for hh, x in zip(hot1, rest):
        hit = jnp.logical_and(jnp.logical_and(x == m2, jnp.logical_not(hh)), jnp.logical_not(taken))
        hot2.append(hit)
        taken = jnp.logical_or(taken, hit)
    e2 = jnp.exp(m2 - m1)
    w1 = g_p / (1.0 + e2)
    w2 = g_p * e2 / (1.0 + e2)
    rows = []
    for gi in range(N_EXPERT_GROUPS):
        for j in range(EXPERTS_PER_GROUP):
            val = jnp.where(hot1[j], w1, 0.0) + jnp.where(hot2[j], w2, 0.0)
            rows.append(jnp.where(g_hot[gi], val, 0.0))
    return jnp.concatenate(rows, axis=0)


def _outproj_kernel(h_ref, yaT_ref, yb_ref, yc_ref, wa_ref, wb_ref, wc_ref, g_ref, b_ref,
                    wrT_ref, br_ref, h1_ref, h1b_ref, comb_ref, *, tm):
    ya = yaT_ref[0].T
    mix = _dot(ya.astype(BF16), wa_ref[...])
    mix = mix + _dot(yb_ref[...].astype(BF16), wb_ref[...])
    mix = mix + _dot(yc_ref[...].astype(BF16), wc_ref[...])
    h1 = _layer_norm(ALPHA * h_ref[...] + mix, g_ref[...], b_ref[...])
    h1_ref[...] = h1
    h1b_ref[...] = h1.astype(BF16)
    h_hi, h_lo = _split_bf16(h1)
    w_hi = wrT_ref[0:LANES, :]
    w_lo = wrT_ref[LANES:2 * LANES, :]
    lt = _dot_nt(w_hi, h_hi) + _dot_nt(w_hi, h_lo) + _dot_nt(w_lo, h_hi) + br_ref[...]
    comb = _route_rows(lt, tm)
    combp = jnp.concatenate([comb, jnp.zeros((LANES - N_EXPERTS, tm), F32)], axis=0)
    comb_ref[...] = combp.T


def _out_projection(h, yaT, yb, yc, w_out, ln_g, ln_b, w_grp, b_grp, w_exp, b_exp,
                    *, batch, seq, tm=512):
    t = batch * seq
    nt = seq // tm
    wa = w_out[0:DA_WIDTH].astype(BF16)
    wb = w_out[DA_WIDTH:DA_WIDTH + GDN_WIDTH].astype(BF16)
    wc = w_out[DA_WIDTH + GDN_WIDTH:].astype(BF16)
    wr = jnp.zeros((D_MODEL, LANES), F32)
    wr = wr.at[:, 0:N_EXPERT_GROUPS].set(w_grp.astype(F32)).at[:, 8:8 + N_EXPERTS].set(w_exp.astype(F32))
    wrT = wr.T
    wr_hi = wrT.astype(BF16)
    wr_lo = (wrT - wr_hi.astype(F32)).astype(BF16)
    wr_cat = jnp.concatenate([wr_hi, wr_lo], axis=0)
    br = jnp.zeros((LANES, 1), F32)
    br = br.at[0:N_EXPERT_GROUPS, 0].set(b_grp.astype(F32)).at[8:8 + N_EXPERTS, 0].set(b_exp.astype(F32))
    row = lambda b, i: (b * nt + i, 0)
    const = lambda b, i: (0, 0)
    return pl.pallas_call(
        functools.partial(_outproj_kernel, tm=tm),
        out_shape=[
            jax.ShapeDtypeStruct((t, D_MODEL), F32),
            jax.ShapeDtypeStruct((t, D_MODEL), BF16),
            jax.ShapeDtypeStruct((t, LANES), F32),
        ],
        grid=(batch, nt),
        in_specs=[
            pl.BlockSpec((tm, D_MODEL), row),
            pl.BlockSpec((1, DA_WIDTH, tm), lambda b, i: (b, 0, i)),
            pl.BlockSpec((tm, GDN_WIDTH), row),
            pl.BlockSpec((tm, S5_WIDTH), row),
            pl.BlockSpec(wa.shape, const),
            pl.BlockSpec(wb.shape, const),
            pl.BlockSpec(wc.shape, const),
            pl.BlockSpec((1, D_MODEL), const),
            pl.BlockSpec((1, D_MODEL), const),
            pl.BlockSpec(wr_cat.shape, const),
            pl.BlockSpec((LANES, 1), const),
        ],
        out_specs=[
            pl.BlockSpec((tm, D_MODEL), row),
            pl.BlockSpec((tm, D_MODEL), row),
            pl.BlockSpec((tm, LANES), row),
        ],
        compiler_params=_cparams(("parallel", "parallel")),
        name="out_projection_router",
    )(h, yaT, yb, yc, wa, wb, wc, ln_g.reshape(1, -1), ln_b.reshape(1, -1), wr_cat, br)


def _moe_kernel(hb_ref, h1_ref, comb_ref, w1_ref, w3_ref, w2_ref, g_ref, b_ref, o_ref, acc_ref):
    e = pl.program_id(1)

    @pl.when(e == 0)
    def _():
        acc_ref[...] = jnp.zeros_like(acc_ref)

    x = hb_ref[...]
    lane = lax.broadcasted_iota(jnp.int32, (1, LANES), 1)
    c = jnp.sum(jnp.where(lane == e, comb_ref[...], 0.0), axis=1, keepdims=True)
    a = _dot(x, w1_ref[0])
    b = _dot(x, w3_ref[0])
    hid = a * jax.nn.sigmoid(a) * b
    acc_ref[...] += c * _dot(hid.astype(BF16), w2_ref[0])

    @pl.when(e == N_EXPERTS - 1)
    def _():
        o_ref[...] = _layer_norm(ALPHA * h1_ref[...] + acc_ref[...], g_ref[...], b_ref[...])


def _moe(h1, h1b, comb, w1, w3, w2, ln_g, ln_b, *, tm=1024):
    t = h1.shape[0]
    nt = t // tm
    row = lambda i, e: (i, 0)
    const = lambda i, e: (0, 0)
    return pl.pallas_call(
        _moe_kernel,
        out_shape=jax.ShapeDtypeStruct((t, D_MODEL), F32),
        grid=(nt, N_EXPERTS),
        in_specs=[
            pl.BlockSpec((tm, D_MODEL), row),
            pl.BlockSpec((tm, D_MODEL), row),
            pl.BlockSpec((tm, LANES), row),
            pl.BlockSpec((1, D_MODEL, D_EXPERT), lambda i, e: (e, 0, 0)),
            pl.BlockSpec((1, D_MODEL, D_EXPERT), lambda i, e: (e, 0, 0)),
            pl.BlockSpec((1, D_EXPERT, D_MODEL), lambda i, e: (e, 0, 0)),
            pl.BlockSpec((1, D_MODEL), const),
            pl.BlockSpec((1, D_MODEL), const),
        ],
        out_specs=pl.BlockSpec((tm, D_MODEL), row),
        scratch_shapes=[pltpu.VMEM((tm, D_MODEL), F32)],
        compiler_params=_cparams(("parallel", "arbitrary")),
        name="moe_ffn",
    )(h1b, h1, comb, w1, w3, w2, ln_g.reshape(1, -1), ln_b.reshape(1, -1))


def _split_w_in(w):
    o = 0
    wq = w[:, o:o + DA_WIDTH]; o += DA_WIDTH
    wk = w[:, o:o + DA_WIDTH]; o += DA_WIDTH
    wv = w[:, o:o + DA_WIDTH]; o += DA_WIDTH
    wg = w[:, o:o + 4 * GDN_WIDTH]; o += 4 * GDN_WIDTH
    wbeta = w[:, o:o + GDN_HEADS]; o += GDN_HEADS
    wa = w[:, o:o + GDN_HEADS]; o += GDN_HEADS
    wc = w[:, o:o + S5_WIDTH]
    wsm = jnp.zeros((D_MODEL, LANES), w.dtype)
    wsm = wsm.at[:, BETA_LANE0:BETA_LANE0 + GDN_HEADS].set(wbeta)
    wsm = wsm.at[:, A_LANE0:A_LANE0 + GDN_HEADS].set(wa)
    kd = 2 * DA_HEAD_DIM
    wk_pad = jnp.zeros((D_MODEL, DA_HEADS, K_PAD), w.dtype)
    wk_pad = wk_pad.at[:, :, 0:kd].set(wk.reshape(D_MODEL, DA_HEADS, kd)).reshape(D_MODEL, DA_HEADS * K_PAD)
    kone = jnp.zeros((1, DA_HEADS, K_PAD), F32).at[:, :, kd].set(1.0).reshape(1, DA_HEADS * K_PAD)
    return (wq.astype(BF16), wv.astype(BF16), wk_pad.astype(BF16), kone, wg.astype(BF16),
            wsm.astype(BF16), wc.astype(BF16))


def kernel(x, ln_in_g, ln_in_b, w_in, w_out, lam_q1, lam_k1, lam_q2, lam_k2, diff_norm_g, dn_conv_w, dn_a_log, dn_dt_bias, dn_norm_g, s5_lambda_re, s5_lambda_im, s5_log_dt, s5_b_re, s5_b_im, s5_c_re, s5_c_im, s5_d, s5_w_glu, ln1_g, ln1_b, moe_w_grp, moe_b_grp, moe_w_exp, moe_b_exp, moe_w1, moe_w3, moe_w2, ln2_g, ln2_b):
    batch, seq, d = x.shape
    h = x.reshape(batch * seq, d)
    for l in range(DEPTH):
        lam_init = 0.8 - 0.6 * math.exp(-0.3 * l)
        wts = _split_w_in(w_in[l])
        outs = _in_projection(h, ln_in_g, ln_in_b, wts, batch=batch, seq=seq, apply_ln=(l == 0))
        if l == 0:
            h, qT, vT, k, gdn_in, small, cu = outs
        else:
            qT, vT, k, gdn_in, small, cu = outs
        lam = (jnp.exp(jnp.sum(lam_q1[l] * lam_k1[l])) - jnp.exp(jnp.sum(lam_q2[l] * lam_k2[l]))
               ).astype(F32) + lam_init
        yaT = _diff_attention(lam, qT, k, vT, diff_norm_g[l].astype(F32), lam_init=lam_init)
        yb = _gdn_mixer(gdn_in, small, dn_conv_w[l], dn_a_log[l], dn_dt_bias[l], dn_norm_g[l],
                        batch=batch, seq=seq)
        yc = _s5_mixer(cu, s5_lambda_re[l], s5_lambda_im[l], s5_log_dt[l], s5_b_re[l], s5_b_im[l],
                       s5_c_re[l], s5_c_im[l], s5_d[l], s5_w_glu[l], batch=batch, seq=seq)
        h1, h1b, comb = _out_projection(h, yaT, yb, yc, w_out[l], ln1_g[l], ln1_b[l],
                                        moe_w_grp[l], moe_b_grp[l], moe_w_exp[l], moe_b_exp[l],
                                        batch=batch, seq=seq)
        h = _moe(h1, h1b, comb, moe_w1[l].astype(BF16), moe_w3[l].astype(BF16),
                 moe_w2[l].astype(BF16), ln2_g[l], ln2_b[l])
    return h.reshape(batch, seq, d)
```

```python
import functools
import math

import jax
import jax.numpy as jnp
from jax import lax
from jax.experimental import pallas as pl
from jax.experimental.pallas import tpu as pltpu

F32 = jnp.float32
BF16 = jnp.bfloat16

D_MODEL = 1024
DEPTH = 2
CHUNK = 64
DA_HEADS = 6
DA_HEAD_DIM = 32
DA_V_DIM = 64
DA_WIDTH = 384
GDN_HEADS = 6
GDN_HEAD_DIM = 64
GDN_WIDTH = 384
CONV_K = 4
S5_GROUP_DIM = 16
S5_GROUPS = 16
S5_WIDTH = 256
S5_STATE = 64
S5_LANES = S5_GROUPS * S5_STATE
N_EXPERT_GROUPS = 4
EXPERTS_PER_GROUP = 4
N_EXPERTS = 16
D_EXPERT = 512
ALPHA = (2 * DEPTH) ** 0.25
LN_EPS = 1e-5
RMS_EPS = 1e-6
LOG2E = 1.4426950408889634

V7X_VMEM_LIMIT_BYTES = 56 * 1024 * 1024
SUBLANES = 8
LANES = 128
NEG_BIG = -1e30
FAST_MAX_LOG2 = 100.0
K_PAD = LANES

BETA_LANE0 = 0
A_LANE0 = 8


def _cparams(sem):
    return pltpu.CompilerParams(dimension_semantics=sem, vmem_limit_bytes=V7X_VMEM_LIMIT_BYTES)


def _layer_norm(x, g, b):
    mu = jnp.mean(x, axis=-1, keepdims=True)
    xc = x - mu
    var = jnp.mean(xc * xc, axis=-1, keepdims=True)
    return xc * lax.rsqrt(var + LN_EPS) * g + b


def _dot(a, b):
    return jnp.dot(a, b, preferred_element_type=F32)


def _dot_nt(a, b):
    return lax.dot_general(a, b, (((1,), (1,)), ((), ())), preferred_element_type=F32)


def _dot_tn(a, b):
    return lax.dot_general(a, b, (((0,), (0,)), ((), ())), preferred_element_type=F32)


def _proj_kernel(x_ref, g_ref, b_ref, wq_ref, wv_ref, wk_ref, kone_ref, wg_ref, wsm_ref, wc_ref,
                 *out_refs, apply_ln, q_scale):
    if apply_ln:
        h_ref, qT_ref, vT_ref, k_ref, gdn_ref, small_ref, cu_ref = out_refs
        h = _layer_norm(x_ref[...], g_ref[...], b_ref[...])
        h_ref[...] = h
    else:
        qT_ref, vT_ref, k_ref, gdn_ref, small_ref, cu_ref = out_refs
        h = x_ref[...]
    hb = h.astype(BF16)
    qT_ref[0] = (_dot(hb, wq_ref[...]) * q_scale).T.astype(BF16)
    vT_ref[0] = _dot(hb, wv_ref[...]).T.astype(BF16)
    k_ref[0] = (_dot(hb, wk_ref[...]) + kone_ref[...]).astype(BF16)
    gdn_ref[...] = _dot(hb, wg_ref[...])
    small_ref[...] = _dot(hb, wsm_ref[...])
    cu_ref[...] = _dot(hb, wc_ref[...])


def _in_projection(x2d, g, b, wts, *, batch, seq, apply_ln, tm=512):
    t = batch * seq
    nt = seq // tm
    wqT, wvT, wk, kone, wg, wsm, wc = wts
    kw = DA_HEADS * K_PAD
    q_scale = (DA_HEAD_DIM ** -0.5) * LOG2E
    row = lambda bi, i: (bi * nt + i, 0)
    const = lambda bi, i: (0, 0)
    out_shape = [
        jax.ShapeDtypeStruct((batch, DA_WIDTH, seq), BF16),
        jax.ShapeDtypeStruct((batch, DA_WIDTH, seq), BF16),
        jax.ShapeDtypeStruct((batch, seq, kw), BF16),
        jax.ShapeDtypeStruct((t, 4 * GDN_WIDTH), F32),
        jax.ShapeDtypeStruct((t, LANES), F32),
        jax.ShapeDtypeStruct((t, S5_WIDTH), F32),
    ]
    out_specs = [
        pl.BlockSpec((1, DA_WIDTH, tm), lambda bi, i: (bi, 0, i)),
        pl.BlockSpec((1, DA_WIDTH, tm), lambda bi, i: (bi, 0, i)),
        pl.BlockSpec((1, tm, kw), lambda bi, i: (bi, i, 0)),
        pl.BlockSpec((tm, 4 * GDN_WIDTH), row),
        pl.BlockSpec((tm, LANES), row),
        pl.BlockSpec((tm, S5_WIDTH), row),
    ]
    if apply_ln:
        out_shape = [jax.ShapeDtypeStruct((t, D_MODEL), F32)] + out_shape
        out_specs = [pl.BlockSpec((tm, D_MODEL), row)] + out_specs
    in_specs = [
        pl.BlockSpec((tm, D_MODEL), row),
        pl.BlockSpec((1, D_MODEL), const),
        pl.BlockSpec((1, D_MODEL), const),
        pl.BlockSpec(wqT.shape, const),
        pl.BlockSpec(wvT.shape, const),
        pl.BlockSpec(wk.shape, const),
        pl.BlockSpec(kone.shape, const),
        pl.BlockSpec(wg.shape, const),
        pl.BlockSpec(wsm.shape, const),
        pl.BlockSpec(wc.shape, const),
    ]
    return pl.pallas_call(
        functools.partial(_proj_kernel, apply_ln=apply_ln, q_scale=q_scale),
        out_shape=out_shape,
        grid=(batch, nt),
        in_specs=in_specs,
        out_specs=out_specs,
        compiler_params=_cparams(("parallel", "parallel")),
        name="in_projection_ln" if apply_ln else "in_projection",
    )(x2d, g.reshape(1, -1), b.reshape(1, -1), wqT, wvT, wk, kone, wg, wsm, wc)


def _attn_kernel(lam_ref, qT_ref, k_ref, vT_ref, g_ref, o_ref,
                 qbd_ref, pa_ref, pb_ref, m_ref, l_ref, acc_ref, cm_ref, *, tq, tk, out_scale):
    i = pl.program_id(2)
    dh = DA_HEAD_DIM
    ref_row = 2 * dh
    n_diag = 4
    assert tq == n_diag * tk

    def qk(j):
        start = pl.multiple_of(j * tk, tk)
        return _dot(k_ref[0, pl.ds(start, tk), :], qbd_ref[...])

    def pv(j, p):
        start = pl.multiple_of(j * tk, tk)
        return _dot(vT_ref[0, :, pl.ds(start, tk)], p.astype(BF16))

    def init_stats():
        m_ref[...] = jnp.full_like(m_ref, NEG_BIG)
        l_ref[...] = jnp.zeros_like(l_ref)
        acc_ref[...] = jnp.zeros_like(acc_ref)

    def exact_step(j, mask):
        s = qk(j)
        if mask is not None:
            s = jnp.where(mask, s, NEG_BIG)
        m_old = m_ref[...]
        m_new = jnp.maximum(m_old, jnp.max(s, axis=0, keepdims=True))
        alpha = jnp.exp2(m_old - m_new)
        p = jnp.exp2(s - m_new)
        l_ref[...] = alpha * l_ref[...] + jnp.sum(p, axis=0, keepdims=True)
        m_ref[...] = m_new
        acc_ref[...] = alpha * acc_ref[...] + pv(j, p)

    def score_exp(j, mask=None):
        s = qk(j)
        if mask is not None:
            s = jnp.where(mask, s, NEG_BIG)
        cm_ref[...] = jnp.maximum(cm_ref[...], jnp.max(s, axis=0, keepdims=True))
        p = jnp.exp2(s)
        l_ref[...] += jnp.sum(p, axis=0, keepdims=True)
        return p.astype(BF16)

    def accumulate(p_ref, j):
        acc_ref[...] += pv(j, p_ref[...])

    kc = lax.broadcasted_iota(jnp.int32, (tk, 1), 0) // CHUNK
    col = lax.broadcasted_iota(jnp.int32, (1, 2 * tq), 1)
    qc = jnp.where(col >= tq, col - tq, col) // CHUNK

    def diag_mask(d):
        return (kc + d * (tk // CHUNK)) <= qc

    qbd_ref[...] = jnp.zeros_like(qbd_ref)
    q = qT_ref[0]
    qbd_ref[0:dh, 0:tq] = q[0:dh]
    qbd_ref[dh:2 * dh, tq:2 * tq] = q[dh:2 * dh]
    init_stats()

    off_diag = i > 0
    t_diag = n_diag * i
    exact_step(t_diag, diag_mask(0))

    m = m_ref[...]
    mref = m.astype(BF16)
    fix = jnp.exp2(m - mref.astype(F32))
    l_ref[...] = l_ref[...] * fix
    acc_ref[...] = acc_ref[...] * fix
    qbd_ref[ref_row:ref_row + 16, :] = jnp.broadcast_to(-mref, (16, 2 * tq))
    cm_ref[...] = jnp.zeros_like(cm_ref)

    pa_ref[...] = score_exp(t_diag + 1, diag_mask(1))
    pb_ref[...] = score_exp(t_diag + 2, diag_mask(2))
    accumulate(pa_ref, t_diag + 1)
    pa_ref[...] = score_exp(t_diag + 3, diag_mask(3))
    accumulate(pb_ref, t_diag + 2)
    accumulate(pa_ref, t_diag + 3)

    @pl.when(off_diag)
    def _():
        pa_ref[...] = score_exp(0)

        def quad(t, lookahead):
            pb_ref[...] = score_exp(t + 1)
            accumulate(pa_ref, t)
            pa_ref[...] = score_exp(t + 2)
            accumulate(pb_ref, t + 1)
            pb_ref[...] = score_exp(t + 3)
            accumulate(pa_ref, t + 2)
            if lookahead:
                pa_ref[...] = score_exp(t + 4)
            accumulate(pb_ref, t + 3)

        def quad_body(r, carry):
            quad(4 * r, True)
            return carry

        lax.fori_loop(0, i - 1, quad_body, 0)
        quad(4 * (i - 1), False)

    @pl.when(jnp.max(cm_ref[...]) > FAST_MAX_LOG2)
    def _():
        qbd_ref[ref_row:ref_row + 16, :] = jnp.zeros((16, 2 * tq), BF16)
        init_stats()

        def body(j, carry):
            exact_step(j, None)
            return carry

        lax.fori_loop(0, t_diag, body, 0)
        for d in range(n_diag):
            exact_step(t_diag + d, diag_mask(d))

    l = l_ref[...]
    acc = acc_ref[...]
    lam = lam_ref[0]
    o = acc[:, 0:tq] / l[:, 0:tq] - lam * (acc[:, tq:2 * tq] / l[:, tq:2 * tq])
    ms = jnp.mean(o * o, axis=0, keepdims=True)
    o_ref[0] = o * lax.rsqrt(ms + RMS_EPS) * g_ref[...] * out_scale


def _diff_attention(lam, qT, k, vT, norm_g, *, lam_init, tq=1024, tk=256):
    batch, _, seq = qT.shape
    nq = seq // tq
    dv = DA_V_DIM
    return pl.pallas_call(
        functools.partial(_attn_kernel, tq=tq, tk=tk, out_scale=1.0 - lam_init),
        out_shape=jax.ShapeDtypeStruct((batch, DA_WIDTH, seq), F32),
        grid=(batch, DA_HEADS, nq),
        in_specs=[
            pl.BlockSpec(memory_space=pltpu.SMEM),
            pl.BlockSpec((1, dv, tq), lambda b, h, i: (b, h, i)),
            pl.BlockSpec((1, seq, LANES), lambda b, h, i: (b, 0, h)),
            pl.BlockSpec((1, dv, seq), lambda b, h, i: (b, h, 0)),
            pl.BlockSpec((dv, 1), lambda b, h, i: (0, 0)),
        ],
        out_specs=pl.BlockSpec((1, dv, tq), lambda b, h, i: (b, h, i)),
        scratch_shapes=[
            pltpu.VMEM((LANES, 2 * tq), BF16),
            pltpu.VMEM((tk, 2 * tq), BF16),
            pltpu.VMEM((tk, 2 * tq), BF16),
            pltpu.VMEM((1, 2 * tq), F32),
            pltpu.VMEM((1, 2 * tq), F32),
            pltpu.VMEM((dv, 2 * tq), F32),
            pltpu.VMEM((1, 2 * tq), F32),
        ],
        compiler_params=_cparams(("parallel", "parallel", "parallel")),
        name="diff_attention",
    )(lam.reshape(1), qT, k, vT, norm_g.reshape(dv, 1))


def _split_bf16(x):
    hi = x.astype(BF16)
    lo = (x - hi.astype(F32)).astype(BF16)
    return hi, lo


def _mm_bf16(a, b):
    return _dot(a.astype(BF16), b.astype(BF16))


def _gdn_kernel(qkv_ref, gate_ref, small_ref, convw_ref, gl_ref, ng_ref, o_ref,
                xbuf_ref, state_ref, pb16_ref, pf32_ref, prhs_ref, pegl_ref, *, rows):
    step_i = pl.program_id(1)
    dk = GDN_HEAD_DIM
    nch = rows // CHUNK
    halo = SUBLANES
    heads = range(GDN_HEADS)
    items = [(c, hh) for c in range(nch) for hh in heads]
    cur = (step_i + 1) % 2
    nxt = step_i % 2

    @pl.when(step_i == 0)
    def _():
        xbuf_ref[0:halo, :] = jnp.zeros((halo, 3 * GDN_WIDTH), F32)
        state_ref[...] = jnp.zeros_like(state_ref)
        pb16_ref[...] = jnp.zeros_like(pb16_ref)
        pf32_ref[...] = jnp.zeros_like(pf32_ref)
        prhs_ref[...] = jnp.zeros_like(prhs_ref)
        pegl_ref[...] = jnp.zeros_like(pegl_ref)

    qb, kbb, kbf, qdec, kdec, decays, gsilu, rhss, egl = {}, {}, {}, {}, {}, {}, {}, {}, {}
    for n, it in enumerate(items):
        qb[it] = pb16_ref[cur, 0, n]
        kbb[it] = pb16_ref[cur, 1, n]
        kbf[it] = pb16_ref[cur, 2, n]
        qdec[it] = pb16_ref[cur, 3, n]
        kdec[it] = pb16_ref[cur, 4, n]
        decays[it] = pf32_ref[cur, 0, n]
        gsilu[it] = pf32_ref[cur, 1, n]
        rhss[it] = prhs_ref[cur, n]
        egl[it] = pegl_ref[cur, n][0:1, 0:1]

    ri = lax.broadcasted_iota(jnp.int32, (CHUNK, CHUNK), 0)
    ci = lax.broadcasted_iota(jnp.int32, (CHUNK, CHUNK), 1)
    tri = ri >= ci
    strict = ri > ci

    ng = ng_ref[...]
    kk = {it: _dot_nt(kbb[it], kbf[it]) for it in items}
    qk = {it: _dot_nt(qb[it], kbf[it]) for it in items}
    lm = {it: jnp.where(strict, kk[it] * decays[it], 0.0) for it in items}
    a_intra = {it: qk[it] * decays[it] for it in items}
    xs = {it: rhss[it] - _mm_bf16(lm[it], rhss[it]) for it in items}
    ps = lm
    for _ in range(5):
        ps = {it: _mm_bf16(ps[it], ps[it]) for it in items}
        xs = {it: xs[it] + _mm_bf16(ps[it], xs[it]) for it in items}

    state = [state_ref[hh] for hh in heads]
    for c in range(nch):
        r0 = c * CHUNK
        stb = [state[hh].astype(BF16) for hh in heads]
        ws = [_dot(xs[(c, hh)][:, dk:2 * dk].astype(BF16), stb[hh]) for hh in heads]
        qst = [_dot(qdec[(c, hh)], stb[hh]) for hh in heads]
        vn = [(xs[(c, hh)][:, 0:dk] - ws[hh]).astype(BF16) for hh in heads]
        av = [_dot(a_intra[(c, hh)].astype(BF16), vn[hh]) for hh in heads]
        kv = [_dot_tn(kdec[(c, hh)], vn[hh]) for hh in heads]
        for hh in heads:
            state[hh] = state[hh] * egl[(c, hh)] + kv[hh]
            o = qst[hh] + av[hh]
            ms = jnp.mean(o * o, axis=-1, keepdims=True)
            on = o * lax.rsqrt(ms + RMS_EPS) * ng
            o_ref[r0:r0 + CHUNK, hh * dk:(hh + 1) * dk] = on * gsilu[(c, hh)]
    for hh in heads:
        state_ref[hh] = state[hh]

    xbuf_ref[halo:halo + rows, :] = qkv_ref[...]
    y = convw_ref[CONV_K - 1:CONV_K, :] * xbuf_ref[halo:halo + rows, :]
    for j in range(CONV_K - 1):
        off = halo - (CONV_K - 1) + j
        y = y + convw_ref[j:j + 1, :] * xbuf_ref[off:off + rows, :]
    xbuf_ref[0:halo, :] = xbuf_ref[rows:rows + halo, :]
    y = y * jax.nn.sigmoid(y)

    small = small_ref[...]
    beta_all = jax.nn.sigmoid(small)
    sp_in = small + gl_ref[1:2, :]
    softplus = jnp.maximum(sp_in, 0.0) + jnp.log(1.0 + jnp.exp(-jnp.abs(sp_in)))
    g_all = gl_ref[0:1, :] * softplus
    tril_f = tri.astype(F32)
    gc_parts = []
    for c in range(nch):
        gch = g_all[c * CHUNK:(c + 1) * CHUNK, :]
        gc_parts.append(jnp.dot(tril_f, gch, preferred_element_type=F32,
                                precision=lax.Precision.HIGHEST))
    gc_all = jnp.concatenate(gc_parts, axis=0) if nch > 1 else gc_parts[0]
    pad = (-rows) % LANES
    gc_sq = jnp.concatenate([gc_all, jnp.zeros((pad, LANES), F32)], axis=0) if pad else gc_all
    gcT = gc_sq.T

    gate = gate_ref[...]
    for n, it in enumerate(items):
        c, hh = it
        r0 = c * CHUNK
        q = y[r0:r0 + CHUNK, hh * dk:(hh + 1) * dk]
        k = y[r0:r0 + CHUNK, GDN_WIDTH + hh * dk:GDN_WIDTH + (hh + 1) * dk]
        v = y[r0:r0 + CHUNK, 2 * GDN_WIDTH + hh * dk:2 * GDN_WIDTH + (hh + 1) * dk]
        q = q * lax.rsqrt(jnp.sum(q * q, axis=-1, keepdims=True) + RMS_EPS) * (dk ** -0.5)
        k = k * lax.rsqrt(jnp.sum(k * k, axis=-1, keepdims=True) + RMS_EPS)
        beta = beta_all[r0:r0 + CHUNK, BETA_LANE0 + hh:BETA_LANE0 + hh + 1]
        gcol = gc_all[r0:r0 + CHUNK, A_LANE0 + hh:A_LANE0 + hh + 1]
        grow = gcT[A_LANE0 + hh:A_LANE0 + hh + 1, r0:r0 + CHUNK]
        glast = gcT[A_LANE0 + hh:A_LANE0 + hh + 1, r0 + CHUNK - 1:r0 + CHUNK]
        eg = jnp.exp(gcol)
        kb = k * beta
        gt = gate[r0:r0 + CHUNK, hh * dk:(hh + 1) * dk]
        pb16_ref[nxt, 0, n] = q.astype(BF16)
        pb16_ref[nxt, 1, n] = kb.astype(BF16)
        pb16_ref[nxt, 2, n] = k.astype(BF16)
        pb16_ref[nxt, 3, n] = (q * eg).astype(BF16)
        pb16_ref[nxt, 4, n] = (k * jnp.exp(glast - gcol)).astype(BF16)
        pf32_ref[nxt, 0, n] = jnp.where(tri, jnp.exp(jnp.where(tri, gcol - grow, 0.0)), 0.0)
        pf32_ref[nxt, 1, n] = gt * jax.nn.sigmoid(gt)
        prhs_ref[nxt, n] = jnp.concatenate([v * beta, kb * eg], axis=1)
        pegl_ref[nxt, n] = jnp.broadcast_to(jnp.exp(glast), (SUBLANES, LANES))


def _gdn_mixer(gdn_in, small, conv_w, a_log, dt_bias, norm_g, *, batch, seq, rows=128):
    t = batch * seq
    ns = seq // rows
    gl = jnp.zeros((SUBLANES, LANES), F32)
    gl = gl.at[0, A_LANE0:A_LANE0 + GDN_HEADS].set(-jnp.exp(a_log.astype(F32)))
    gl = gl.at[1, A_LANE0:A_LANE0 + GDN_HEADS].set(dt_bias.astype(F32))
    convw = jnp.zeros((SUBLANES, 3 * GDN_WIDTH), F32).at[0:CONV_K].set(conv_w.astype(F32))
    nitems = (rows // CHUNK) * GDN_HEADS
    dk = GDN_HEAD_DIM
    rin = lambda b, i: (b * ns + jnp.minimum(i, ns - 1), 0)
    rout = lambda b, i: (b * ns + jnp.maximum(i - 1, 0), 0)
    const = lambda b, i: (0, 0)
    return pl.pallas_call(
        functools.partial(_gdn_kernel, rows=rows),
        out_shape=jax.ShapeDtypeStruct((t, GDN_WIDTH), F32),
        grid=(batch, ns + 1),
        in_specs=[
            pl.BlockSpec((rows, 3 * GDN_WIDTH), rin),
            pl.BlockSpec((rows, GDN_WIDTH), lambda b, i: (b * ns + jnp.minimum(i, ns - 1), 3)),
            pl.BlockSpec((rows, LANES), rin),
            pl.BlockSpec((SUBLANES, 3 * GDN_WIDTH), const),
            pl.BlockSpec((SUBLANES, LANES), const),
            pl.BlockSpec((1, GDN_HEAD_DIM), const),
        ],
        out_specs=pl.BlockSpec((rows, GDN_WIDTH), rout),
        scratch_shapes=[
            pltpu.VMEM((rows + SUBLANES, 3 * GDN_WIDTH), F32),
            pltpu.VMEM((GDN_HEADS, dk, dk), F32),
            pltpu.VMEM((2, 5, nitems, CHUNK, dk), BF16),
            pltpu.VMEM((2, 2, nitems, CHUNK, dk), F32),
            pltpu.VMEM((2, nitems, CHUNK, 2 * dk), F32),
            pltpu.VMEM((2, nitems, SUBLANES, LANES), F32),
        ],
        compiler_params=_cparams(("parallel", "arbitrary")),
        name="gated_deltanet",
    )(gdn_in, gdn_in, small, convw, gl, norm_g.reshape(1, -1).astype(F32))


def _cmul(ar, ai, br, bi):
    return ar * br - ai * bi, ar * bi + ai * br


def _s5_kernel(u_ref, bblk_ref, ccat_ref, apow_ref, d_ref, wglu_ref, o_ref,
               bu_ref, x_ref, carry_ref, *, tm):
    n = S5_LANES

    @pl.when(pl.program_id(1) == 0)
    def _():
        carry_ref[...] = jnp.zeros_like(carry_ref)

    u = u_ref[...]
    bu_ref[...] = _dot(u.astype(BF16), bblk_ref[...])

    def group(gidx, carry):
        c_re, c_im = carry
        r0 = pl.multiple_of(gidx * SUBLANES, SUBLANES)
        x_re = bu_ref[pl.ds(r0, SUBLANES), 0:n]
        x_im = bu_ref[pl.ds(r0, SUBLANES), n:2 * n]
        for lvl, d in enumerate((1, 2, 4)):
            a_re = apow_ref[lvl * 2 * SUBLANES:lvl * 2 * SUBLANES + SUBLANES, :]
            a_im = apow_ref[lvl * 2 * SUBLANES + SUBLANES:(lvl + 1) * 2 * SUBLANES, :]
            s_re = pltpu.roll(x_re, d, 0)
            s_im = pltpu.roll(x_im, d, 0)
            t_re, t_im = _cmul(a_re, a_im, s_re, s_im)
            x_re = x_re + t_re
            x_im = x_im + t_im
        p_re = apow_ref[6 * SUBLANES:7 * SUBLANES, :]
        p_im = apow_ref[7 * SUBLANES:8 * SUBLANES, :]
        t_re, t_im = _cmul(p_re, p_im, c_re, c_im)
        x_re = x_re + t_re
        x_im = x_im + t_im
        x_ref[pl.ds(r0, SUBLANES), 0:n] = x_re
        x_ref[pl.ds(r0, SUBLANES), n:2 * n] = x_im
        return x_re[SUBLANES - 1:SUBLANES, :], x_im[SUBLANES - 1:SUBLANES, :]

    c_re, c_im = lax.fori_loop(0, tm // SUBLANES, group,
                               (carry_ref[0:1, :], carry_ref[1:2, :]))
    carry_ref[0:1, :] = c_re
    carry_ref[1:2, :] = c_im

    yv = _dot(x_ref[...].astype(BF16), ccat_ref[...]) + d_ref[...] * u
    yv = 0.5 * yv * (1.0 + jnp.tanh(0.7978845608028654 * (yv + 0.044715 * (yv * yv * yv))))
    z = _dot(yv.astype(BF16), wglu_ref[...])
    o_ref[...] = yv * jax.nn.sigmoid(z)


def _s5_params(lam_re, lam_im, log_dt, b_re, b_im, c_re, c_im):
    f32 = F32
    lre, lim = lam_re.astype(f32), lam_im.astype(f32)
    dt = jnp.exp(log_dt.astype(f32))[:, None]
    mag = jnp.exp(lre * dt)
    ab_re, ab_im = mag * jnp.cos(lim * dt), mag * jnp.sin(lim * dt)
    num_re, num_im = ab_re - 1.0, ab_im
    den = lre * lre + lim * lim
    coef_re = (num_re * lre + num_im * lim) / den
    coef_im = (num_im * lre - num_re * lim) / den
    br, bi = b_re.astype(f32), b_im.astype(f32)
    bb_re = coef_re[..., None] * br - coef_im[..., None] * bi
    bb_im = coef_re[..., None] * bi + coef_im[..., None] * br
    eye = jnp.eye(S5_GROUPS, dtype=f32)
    blk_re = jnp.einsum('gph,gk->ghkp', bb_re, eye).reshape(S5_WIDTH, S5_LANES)
    blk_im = jnp.einsum('gph,gk->ghkp', bb_im, eye).reshape(S5_WIDTH, S5_LANES)
    bblk = jnp.concatenate([blk_re, blk_im], axis=1).astype(BF16)
    cb_re = jnp.einsum('ghp,gk->gpkh', c_re.astype(f32), eye).reshape(S5_LANES, S5_WIDTH)
    cb_im = jnp.einsum('ghp,gk->gpkh', c_im.astype(f32), eye).reshape(S5_LANES, S5_WIDTH)
    ccat = jnp.concatenate([cb_re, -cb_im], axis=0).astype(BF16)
    a1 = (ab_re.reshape(1, -1), ab_im.reshape(1, -1))
    pows = [a1]
    for _ in range(SUBLANES - 1):
        pows.append(_cmul(pows[-1][0], pows[-1][1], a1[0], a1[1]))
    rid = jnp.arange(SUBLANES)[:, None]
    rows = []
    for d in (1, 2, 4):
        mask = (rid >= d).astype(f32)
        rows.append(mask * pows[d - 1][0])
        rows.append(mask * pows[d - 1][1])
    rows.append(jnp.concatenate([pows[r][0] for r in range(SUBLANES)], axis=0))
    rows.append(jnp.concatenate([pows[r][1] for r in range(SUBLANES)], axis=0))
    apow = jnp.concatenate(rows, axis=0)
    return bblk, ccat, apow


def _s5_mixer(cu, lam_re, lam_im, log_dt, b_re, b_im, c_re, c_im, d, w_glu, *, batch, seq, tm=256):
    t = batch * seq
    ns = seq // tm
    bblk, ccat, apow = _s5_params(lam_re, lam_im, log_dt, b_re, b_im, c_re, c_im)
    row = lambda b, i: (b * ns + i, 0)
    const = lambda b, i: (0, 0)
    return pl.pallas_call(
        functools.partial(_s5_kernel, tm=tm),
        out_shape=jax.ShapeDtypeStruct((t, S5_WIDTH), F32),
        grid=(batch, ns),
        in_specs=[
            pl.BlockSpec((tm, S5_WIDTH), row),
            pl.BlockSpec(bblk.shape, const),
            pl.BlockSpec(ccat.shape, const),
            pl.BlockSpec(apow.shape, const),
            pl.BlockSpec((1, S5_WIDTH), const),
            pl.BlockSpec((S5_WIDTH, S5_WIDTH), const),
        ],
        out_specs=pl.BlockSpec((tm, S5_WIDTH), row),
        scratch_shapes=[
            pltpu.VMEM((tm, 2 * S5_LANES), F32),
            pltpu.VMEM((tm, 2 * S5_LANES), F32),
            pltpu.VMEM((SUBLANES, S5_LANES), F32),
        ],
        compiler_params=_cparams(("parallel", "arbitrary")),
        name="s5_mixer",
    )(cu, bblk, ccat, apow, d.reshape(1, -1).astype(F32), w_glu.astype(BF16))


def _route_rows(lt, n_tok):
    g = [lt[r:r + 1, :] for r in range(N_EXPERT_GROUPS)]
    gm = functools.reduce(jnp.maximum, g)
    gsum = functools.reduce(lambda a, b: a + b, [jnp.exp(x - gm) for x in g])
    g_p = 1.0 / gsum
    taken = jnp.zeros_like(gm) > 1.0
    g_hot = []
    for x in g:
        hit = jnp.logical_and(x == gm, jnp.logical_not(taken))
        g_hot.append(hit)
        taken = jnp.logical_or(taken, hit)
    e_sel = []
    for j in range(EXPERTS_PER_GROUP):
        acc = jnp.zeros_like(gm)
        for gi in range(N_EXPERT_GROUPS):
            r = 8 + gi * EXPERTS_PER_GROUP + j
            acc = acc + jnp.where(g_hot[gi], lt[r:r + 1, :], 0.0)
        e_sel.append(acc)
    m1 = functools.reduce(jnp.maximum, e_sel)
    taken = jnp.zeros_like(gm) > 1.0
    hot1 = []
    for x in e_sel:
        hit = jnp.logical_and(x == m1, jnp.logical_not(taken))
        hot1.append(hit)
        taken = jnp.logical_or(taken, hit)
    rest = [jnp.where(hh, NEG_BIG, x) for hh, x in zip(hot1, e_sel)]
    m2 = functools.reduce(jnp.maximum, rest)
    taken = jnp.zeros_like(gm) > 1.0
    hot2 = []
    for hh, x in zip(hot1, rest):
        hit = jnp.logical_and(jnp.logical_and(x == m2, jnp.logical_not(hh)), jnp.logical_not(taken))
        hot2.append(hit)
        taken = jnp.logical_or(taken, hit)
    e2 = jnp.exp(m2 - m1)
    w1 = g_p / (1.0 + e2)
    w2 = g_p * e2 / (1.0 + e2)
    rows = []
    for gi in range(N_EXPERT_GROUPS):
        for j in range(EXPERTS_PER_GROUP):
            val = jnp.where(hot1[j], w1, 0.0) + jnp.where(hot2[j], w2, 0.0)
            rows.append(jnp.where(g_hot[gi], val, 0.0))
    return jnp.concatenate(rows, axis=0)


def _outproj_kernel(h_ref, yaT_ref, yb_ref, yc_ref, wa_ref, wb_ref, wc_ref, g_ref, b_ref,
                    wrT_ref, br_ref, h1_ref, h1b_ref, comb_ref, *, tm):
    ya = yaT_ref[0].T
    mix = _dot(ya.astype(BF16), wa_ref[...])
    mix = mix + _dot(yb_ref[...].astype(BF16), wb_ref[...])
    mix = mix + _dot(yc_ref[...].astype(BF16), wc_ref[...])
    h1 = _layer_norm(ALPHA * h_ref[...] + mix, g_ref[...], b_ref[...])
    h1_ref[...] = h1
    h1b_ref[...] = h1.astype(BF16)
    h_hi, h_lo = _split_bf16(h1)
    w_hi = wrT_ref[0:LANES, :]
    w_lo = wrT_ref[LANES:2 * LANES, :]
    lt = _dot_nt(w_hi, h_hi) + _dot_nt(w_hi, h_lo) + _dot_nt(w_lo, h_hi) + br_ref[...]
    comb = _route_rows(lt, tm)
    combp = jnp.concatenate([comb, jnp.zeros((LANES - N_EXPERTS, tm), F32)], axis=0)
    comb_ref[...] = combp.T


def _out_projection(h, yaT, yb, yc, w_out, ln_g, ln_b, w_grp, b_grp, w_exp, b_exp,
                    *, batch, seq, tm=512):
    t = batch * seq
    nt = seq // tm
    wa = w_out[0:DA_WIDTH].astype(BF16)
    wb = w_out[DA_WIDTH:DA_WIDTH + GDN_WIDTH].astype(BF16)
    wc = w_out[DA_WIDTH + GDN_WIDTH:].astype(BF16)
    wr = jnp.zeros((D_MODEL, LANES), F32)
    wr = wr.at[:, 0:N_EXPERT_GROUPS].set(w_grp.astype(F32)).at[:, 8:8 + N_EXPERTS].set(w_exp.astype(F32))
    wrT = wr.T
    wr_hi = wrT.astype(BF16)
    wr_lo = (wrT - wr_hi.astype(F32)).astype(BF16)
    wr_cat = jnp.concatenate([wr_hi, wr_lo], axis=0)
    br = jnp.zeros((LANES, 1), F32)
    br = br.at[0:N_EXPERT_GROUPS, 0].set(b_grp.astype(F32)).at[8:8 + N_EXPERTS, 0].set(b_exp.astype(F32))
    row = lambda b, i: (b * nt + i, 0)
    const = lambda b, i: (0, 0)
    return pl.pallas_call(
        functools.partial(_outproj_kernel, tm=tm),
        out_shape=[
            jax.ShapeDtypeStruct((t, D_MODEL), F32),
            jax.ShapeDtypeStruct((t, D_MODEL), BF16),
            jax.ShapeDtypeStruct((t, LANES), F32),
        ],
        grid=(batch, nt),
        in_specs=[
            pl.BlockSpec((tm, D_MODEL), row),
            pl.BlockSpec((1, DA_WIDTH, tm), lambda b, i: (b, 0, i)),
            pl.BlockSpec((tm, GDN_WIDTH), row),
            pl.BlockSpec((tm, S5_WIDTH), row),
            pl.BlockSpec(wa.shape, const),
            pl.BlockSpec(wb.shape, const),
            pl.BlockSpec(wc.shape, const),
            pl.BlockSpec((1, D_MODEL), const),
            pl.BlockSpec((1, D_MODEL), const),
            pl.BlockSpec(wr_cat.shape, const),
            pl.BlockSpec((LANES, 1), const),
        ],
        out_specs=[
            pl.BlockSpec((tm, D_MODEL), row),
            pl.BlockSpec((tm, D_MODEL), row),
            pl.BlockSpec((tm, LANES), row),
        ],
        compiler_params=_cparams(("parallel", "parallel")),
        name="out_projection_router",
    )(h, yaT, yb, yc, wa, wb, wc, ln_g.reshape(1, -1), ln_b.reshape(1, -1), wr_cat, br)


def _moe_kernel(hb_ref, h1_ref, comb_ref, w1_ref, w3_ref, w2_ref, g_ref, b_ref, o_ref, acc_ref):
    e = pl.program_id(1)

    @pl.when(e == 0)
    def _():
        acc_ref[...] = jnp.zeros_like(acc_ref)

    x = hb_ref[...]
    lane = lax.broadcasted_iota(jnp.int32, (1, LANES), 1)
    c = jnp.sum(jnp.where(lane == e, comb_ref[...], 0.0), axis=1, keepdims=True)
    a = _dot(x, w1_ref[0])
    b = _dot(x, w3_ref[0])
    hid = a * jax.nn.sigmoid(a) * b
    acc_ref[...] += c * _dot(hid.astype(BF16), w2_ref[0])

    @pl.when(e == N_EXPERTS - 1)
    def _():
        o_ref[...] = _layer_norm(ALPHA * h1_ref[...] + acc_ref[...], g_ref[...], b_ref[...])


def _moe(h1, h1b, comb, w1, w3, w2, ln_g, ln_b, *, tm=1024):
    t = h1.shape[0]
    nt = t // tm
    row = lambda i, e: (i, 0)
    const = lambda i, e: (0, 0)
    return pl.pallas_call(
        _moe_kernel,
        out_shape=jax.ShapeDtypeStruct((t, D_MODEL), F32),
        grid=(nt, N_EXPERTS),
        in_specs=[
            pl.BlockSpec((tm, D_MODEL), row),
            pl.BlockSpec((tm, D_MODEL), row),
            pl.BlockSpec((tm, LANES), row),
            pl.BlockSpec((1, D_MODEL, D_EXPERT), lambda i, e: (e, 0, 0)),
            pl.BlockSpec((1, D_MODEL, D_EXPERT), lambda i, e: (e, 0, 0)),
            pl.BlockSpec((1, D_EXPERT, D_MODEL), lambda i, e: (e, 0, 0)),
            pl.BlockSpec((1, D_MODEL), const),
            pl.BlockSpec((1, D_MODEL), const),
        ],
        out_specs=pl.BlockSpec((tm, D_MODEL), row),
        scratch_shapes=[pltpu.VMEM((tm, D_MODEL), F32)],
        compiler_params=_cparams(("parallel", "arbitrary")),
        name="moe_ffn",
    )(h1b, h1, comb, w1, w3, w2, ln_g.reshape(1, -1), ln_b.reshape(1, -1))


def _split_w_in(w):
    o = 0
    wq = w[:, o:o + DA_WIDTH]; o += DA_WIDTH
    wk = w[:, o:o + DA_WIDTH]; o += DA_WIDTH
    wv = w[:, o:o + DA_WIDTH]; o += DA_WIDTH
    wg = w[:, o:o + 4 * GDN_WIDTH]; o += 4 * GDN_WIDTH
    wbeta = w[:, o:o + GDN_HEADS]; o += GDN_HEADS
    wa = w[:, o:o + GDN_HEADS]; o += GDN_HEADS
    wc = w[:, o:o + S5_WIDTH]
    wsm = jnp.zeros((D_MODEL, LANES), w.dtype)
    wsm = wsm.at[:, BETA_LANE0:BETA_LANE0 + GDN_HEADS].set(wbeta)
    wsm = wsm.at[:, A_LANE0:A_LANE0 + GDN_HEADS].set(wa)
    kd = 2 * DA_HEAD_DIM
    wk_pad = jnp.zeros((D_MODEL, DA_HEADS, K_PAD), w.dtype)
    wk_pad = wk_pad.at[:, :, 0:kd].set(wk.reshape(D_MODEL, DA_HEADS, kd)).reshape(D_MODEL, DA_HEADS * K_PAD)
    kone = jnp.zeros((1, DA_HEADS, K_PAD), F32).at[:, :, kd].set(1.0).reshape(1, DA_HEADS * K_PAD)
    return (wq.astype(BF16), wv.astype(BF16), wk_pad.astype(BF16), kone, wg.astype(BF16),
            wsm.astype(BF16), wc.astype(BF16))


def kernel(x, ln_in_g, ln_in_b, w_in, w_out, lam_q1, lam_k1, lam_q2, lam_k2, diff_norm_g, dn_conv_w, dn_a_log, dn_dt_bias, dn_norm_g, s5_lambda_re, s5_lambda_im, s5_log_dt, s5_b_re, s5_b_im, s5_c_re, s5_c_im, s5_d, s5_w_glu, ln1_g, ln1_b, moe_w_grp, moe_b_grp, moe_w_exp, moe_b_exp, moe_w1, moe_w3, moe_w2, ln2_g, ln2_b):
    batch, seq, d = x.shape
    h = x.reshape(batch * seq, d)
    for l in range(DEPTH):
        lam_init = 0.8 - 0.6 * math.exp(-0.3 * l)
        wts = _split_w_in(w_in[l])
        outs = _in_projection(h, ln_in_g, ln_in_b, wts, batch=batch, seq=seq, apply_ln=(l == 0))
        if l == 0:
            h, qT, vT, k, gdn_in, small, cu = outs
        else:
            qT, vT, k, gdn_in, small, cu = outs
        lam = (jnp.exp(jnp.sum(lam_q1[l] * lam_k1[l])) - jnp.exp(jnp.sum(lam_q2[l] * lam_k2[l]))
               ).astype(F32) + lam_init
        yaT = _diff_attention(lam, qT, k, vT, diff_norm_g[l].astype(F32), lam_init=lam_init)
        yb = _gdn_mixer(gdn_in, small, dn_conv_w[l], dn_a_log[l], dn_dt_bias[l], dn_norm_g[l],
                        batch=batch, seq=seq)
        yc = _s5_mixer(cu, s5_lambda_re[l], s5_lambda_im[l], s5_log_dt[l], s5_b_re[l], s5_b_im[l],
                       s5_c_re[l], s5_c_im[l], s5_d[l], s5_w_glu[l], batch=batch, seq=seq)
        h1, h1b, comb = _out_projection(h, yaT, yb, yc, w_out[l], ln1_g[l], ln1_b[l],
                                        moe_w_grp[l], moe_b_grp[l], moe_w_exp[l], moe_b_exp[l],
                                        batch=batch, seq=seq)
        h = _moe(h1, h1b, comb, moe_w1[l].astype(BF16), moe_w3[l].astype(BF16),
                 moe_w2[l].astype(BF16), ln2_g[l], ln2_b[l])
    return h.reshape(batch, seq, d)
```

```python
import functools
import math

import jax
import jax.numpy as jnp
from jax import lax
from jax.experimental import pallas as pl
from jax.experimental.pallas import tpu as pltpu

F32 = jnp.float32
BF16 = jnp.bfloat16

D_MODEL = 1024
DEPTH = 2
CHUNK = 64
DA_HEADS = 6
DA_HEAD_DIM = 32
DA_V_DIM = 64
DA_WIDTH = 384
GDN_HEADS = 6
GDN_HEAD_DIM = 64
GDN_WIDTH = 384
CONV_K = 4
S5_GROUP_DIM = 16
S5_GROUPS = 16
S5_WIDTH = 256
S5_STATE = 64
S5_LANES = S5_GROUPS * S5_STATE
N_EXPERT_GROUPS = 4
EXPERTS_PER_GROUP = 4
N_EXPERTS = 16
D_EXPERT = 512
ALPHA = (2 * DEPTH) ** 0.25
LN_EPS = 1e-5
RMS_EPS = 1e-6
LOG2E = 1.4426950408889634

V7X_VMEM_LIMIT_BYTES = 56 * 1024 * 1024
SUBLANES = 8
LANES = 128
NEG_BIG = -1e30
FAST_MAX_LOG2 = 100.0
K_PAD = LANES

BETA_LANE0 = 0
A_LANE0 = 8


def _cparams(sem):
    return pltpu.CompilerParams(dimension_semantics=sem, vmem_limit_bytes=V7X_VMEM_LIMIT_BYTES)


def _layer_norm(x, g, b):
    mu = jnp.mean(x, axis=-1, keepdims=True)
    xc = x - mu
    var = jnp.mean(xc * xc, axis=-1, keepdims=True)
    return xc * lax.rsqrt(var + LN_EPS) * g + b


def _dot(a, b):
    return jnp.dot(a, b, preferred_element_type=F32)


def _dot_nt(a, b):
    return lax.dot_general(a, b, (((1,), (1,)), ((), ())), preferred_element_type=F32)


def _dot_tn(a, b):
    return lax.dot_general(a, b, (((0,), (0,)), ((), ())), preferred_element_type=F32)


def _proj_kernel(x_ref, g_ref, b_ref, w_ref, kone_ref, *out_refs, apply_ln, q_scale):
    edges = (0, DA_WIDTH, 2 * DA_WIDTH, 2 * DA_WIDTH + DA_HEADS * K_PAD)
    edges = edges + (edges[-1] + 4 * GDN_WIDTH, edges[-1] + 4 * GDN_WIDTH + LANES,
                     edges[-1] + 4 * GDN_WIDTH + LANES + S5_WIDTH)
    wq_ref, wv_ref, wk_ref, wg_ref, wsm_ref, wc_ref = (
        w_ref.at[lo:hi, :] for lo, hi in zip(edges[:-1], edges[1:]))
    if apply_ln:
        h_ref, qT_ref, vT_ref, k_ref, gdn_ref, small_ref, cu_ref = out_refs
        h = _layer_norm(x_ref[...], g_ref[...], b_ref[...])
        h_ref[...] = h
    else:
        qT_ref, vT_ref, k_ref, gdn_ref, small_ref, cu_ref = out_refs
        h = x_ref[...]
    hb = h.astype(BF16)
    qT_ref[0] = (_dot_nt(wq_ref[...], hb) * q_scale).astype(BF16)
    vT_ref[0] = _dot_nt(wv_ref[...], hb).astype(BF16)
    k_ref[0] = (_dot_nt(hb, wk_ref[...]) + kone_ref[...]).astype(BF16)
    gdn_ref[...] = _dot_nt(hb, wg_ref[...])
    small_ref[...] = _dot_nt(hb, wsm_ref[...])
    cu_ref[...] = _dot_nt(hb, wc_ref[...])


def _in_projection(x2d, g, b, wts, *, batch, seq, apply_ln, tm=512):
    t = batch * seq
    nt = seq // tm
    w_all, kone = wts
    kw = DA_HEADS * K_PAD
    q_scale = (DA_HEAD_DIM ** -0.5) * LOG2E
    row = lambda bi, i: (bi * nt + i, 0)
    const = lambda bi, i: (0, 0)
    out_shape = [
        jax.ShapeDtypeStruct((batch, DA_WIDTH, seq), BF16),
        jax.ShapeDtypeStruct((batch, DA_WIDTH, seq), BF16),
        jax.ShapeDtypeStruct((batch, seq, kw), BF16),
        jax.ShapeDtypeStruct((t, 4 * GDN_WIDTH), F32),
        jax.ShapeDtypeStruct((t, LANES), F32),
        jax.ShapeDtypeStruct((t, S5_WIDTH), F32),
    ]
    out_specs = [
        pl.BlockSpec((1, DA_WIDTH, tm), lambda bi, i: (bi, 0, i)),
        pl.BlockSpec((1, DA_WIDTH, tm), lambda bi, i: (bi, 0, i)),
        pl.BlockSpec((1, tm, kw), lambda bi, i: (bi, i, 0)),
        pl.BlockSpec((tm, 4 * GDN_WIDTH), row),
        pl.BlockSpec((tm, LANES), row),
        pl.BlockSpec((tm, S5_WIDTH), row),
    ]
    if apply_ln:
        out_shape = [jax.ShapeDtypeStruct((t, D_MODEL), F32)] + out_shape
        out_specs = [pl.BlockSpec((tm, D_MODEL), row)] + out_specs
    in_specs = [
        pl.BlockSpec((tm, D_MODEL), row),
        pl.BlockSpec((1, D_MODEL), const),
        pl.BlockSpec((1, D_MODEL), const),
        pl.BlockSpec(w_all.shape, const),
        pl.BlockSpec(kone.shape, const),
    ]
    return pl.pallas_call(
        functools.partial(_proj_kernel, apply_ln=apply_ln, q_scale=q_scale),
        out_shape=out_shape,
        grid=(batch, nt),
        in_specs=in_specs,
        out_specs=out_specs,
        compiler_params=_cparams(("parallel", "parallel")),
        name="in_projection_ln" if apply_ln else "in_projection",
    )(x2d, g.reshape(1, -1), b.reshape(1, -1), w_all, kone)


def _attn_kernel(lam_ref, qT_ref, k_ref, vT_ref, g_ref, o_ref,
                 qbd_ref, pa_ref, pb_ref, m_ref, l_ref, acc_ref, cm_ref, *, tq, tk, out_scale):
    i = pl.program_id(2)
    dh = DA_HEAD_DIM
    ref_row = 2 * dh
    n_diag = 4
    assert tq == n_diag * tk

    def qk(j):
        start = pl.multiple_of(j * tk, tk)
        return _dot(k_ref[0, pl.ds(start, tk), :], qbd_ref[...])

    def pv(j, p):
        start = pl.multiple_of(j * tk, tk)
        return _dot(vT_ref[0, :, pl.ds(start, tk)], p.astype(BF16))

    def init_stats():
        m_ref[...] = jnp.full_like(m_ref, NEG_BIG)
        l_ref[...] = jnp.zeros_like(l_ref)
        acc_ref[...] = jnp.zeros_like(acc_ref)

    def exact_step(j, mask):
        s = qk(j)
        if mask is not None:
            s = jnp.where(mask, s, NEG_BIG)
        m_old = m_ref[...]
        m_new = jnp.maximum(m_old, jnp.max(s, axis=0, keepdims=True))
        alpha = jnp.exp2(m_old - m_new)
        p = jnp.exp2(s - m_new)
        l_ref[...] = alpha * l_ref[...] + jnp.sum(p, axis=0, keepdims=True)
        m_ref[...] = m_new
        acc_ref[...] = alpha * acc_ref[...] + pv(j, p)

    def score_exp(j, mask=None):
        s = qk(j)
        if mask is not None:
            s = jnp.where(mask, s, NEG_BIG)
        cm_ref[...] = jnp.maximum(cm_ref[...], jnp.max(s, axis=0, keepdims=True))
        p = jnp.exp2(s)
        l_ref[...] += jnp.sum(p, axis=0, keepdims=True)
        return p.astype(BF16)

    def accumulate(p_ref, j):
        acc_ref[...] += pv(j, p_ref[...])

    kc = lax.broadcasted_iota(jnp.int32, (tk, 1), 0) // CHUNK
    col = lax.broadcasted_iota(jnp.int32, (1, 2 * tq), 1)
    qc = jnp.where(col >= tq, col - tq, col) // CHUNK

    def diag_mask(d):
        return (kc + d * (tk // CHUNK)) <= qc

    qbd_ref[...] = jnp.zeros_like(qbd_ref)
    q = qT_ref[0]
    qbd_ref[0:dh, 0:tq] = q[0:dh]
    qbd_ref[dh:2 * dh, tq:2 * tq] = q[dh:2 * dh]
    init_stats()

    off_diag = i > 0
    t_diag = n_diag * i
    exact_step(t_diag, diag_mask(0))

    m = m_ref[...]
    mref = m.astype(BF16)
    fix = jnp.exp2(m - mref.astype(F32))
    l_ref[...] = l_ref[...] * fix
    acc_ref[...] = acc_ref[...] * fix
    qbd_ref[ref_row:ref_row + 16, :] = jnp.broadcast_to(-mref, (16, 2 * tq))
    cm_ref[...] = jnp.zeros_like(cm_ref)

    pa_ref[...] = score_exp(t_diag + 1, diag_mask(1))
    pb_ref[...] = score_exp(t_diag + 2, diag_mask(2))
    accumulate(pa_ref, t_diag + 1)
    pa_ref[...] = score_exp(t_diag + 3, diag_mask(3))
    accumulate(pb_ref, t_diag + 2)
    accumulate(pa_ref, t_diag + 3)

    @pl.when(off_diag)
    def _():
        pa_ref[...] = score_exp(0)

        def quad(t, lookahead):
            pb_ref[...] = score_exp(t + 1)
            accumulate(pa_ref, t)
            pa_ref[...] = score_exp(t + 2)
            accumulate(pb_ref, t + 1)
            pb_ref[...] = score_exp(t + 3)
            accumulate(pa_ref, t + 2)
            if lookahead:
                pa_ref[...] = score_exp(t + 4)
            accumulate(pb_ref, t + 3)

        def quad_body(r, carry):
            quad(4 * r, True)
            return carry

        lax.fori_loop(0, i - 1, quad_body, 0)
        quad(4 * (i - 1), False)

    @pl.when(jnp.max(cm_ref[...]) > FAST_MAX_LOG2)
    def _():
        qbd_ref[ref_row:ref_row + 16, :] = jnp.zeros((16, 2 * tq), BF16)
        init_stats()

        def body(j, carry):
            exact_step(j, None)
            return carry

        lax.fori_loop(0, t_diag, body, 0)
        for d in range(n_diag):
            exact_step(t_diag + d, diag_mask(d))

    l = l_ref[...]
    acc = acc_ref[...]
    lam = lam_ref[0]
    o = acc[:, 0:tq] / l[:, 0:tq] - lam * (acc[:, tq:2 * tq] / l[:, tq:2 * tq])
    ms = jnp.mean(o * o, axis=0, keepdims=True)
    o_ref[0] = o * lax.rsqrt(ms + RMS_EPS) * g_ref[...] * out_scale


def _diff_attention(lam, qT, k, vT, norm_g, *, lam_init, tq=1024, tk=256):
    batch, _, seq = qT.shape
    nq = seq // tq
    dv = DA_V_DIM
    return pl.pallas_call(
        functools.partial(_attn_kernel, tq=tq, tk=tk, out_scale=1.0 - lam_init),
        out_shape=jax.ShapeDtypeStruct((batch, DA_WIDTH, seq), F32),
        grid=(batch, DA_HEADS, nq),
        in_specs=[
            pl.BlockSpec(memory_space=pltpu.SMEM),
            pl.BlockSpec((1, dv, tq), lambda b, h, i: (b, h, i)),
            pl.BlockSpec((1, seq, LANES), lambda b, h, i: (b, 0, h)),
            pl.BlockSpec((1, dv, seq), lambda b, h, i: (b, h, 0)),
            pl.BlockSpec((dv, 1), lambda b, h, i: (0, 0)),
        ],
        out_specs=pl.BlockSpec((1, dv, tq), lambda b, h, i: (b, h, i)),
        scratch_shapes=[
            pltpu.VMEM((LANES, 2 * tq), BF16),
            pltpu.VMEM((tk, 2 * tq), BF16),
            pltpu.VMEM((tk, 2 * tq), BF16),
            pltpu.VMEM((1, 2 * tq), F32),
            pltpu.VMEM((1, 2 * tq), F32),
            pltpu.VMEM((dv, 2 * tq), F32),
            pltpu.VMEM((1, 2 * tq), F32),
        ],
        compiler_params=_cparams(("parallel", "parallel", "parallel")),
        name="diff_attention",
    )(lam.reshape(1), qT, k, vT, norm_g.reshape(dv, 1))


def _split_bf16(x):
    hi = x.astype(BF16)
    lo = (x - hi.astype(F32)).astype(BF16)
    return hi, lo


def _mm_bf16(a, b):
    return _dot(a.astype(BF16), b.astype(BF16))


def _gdn_kernel(qkv_ref, gate_ref, small_ref, convw_ref, gl_ref, ng_ref, o_ref,
                xbuf_ref, state_ref, pb16_ref, pf32_ref, prhs_ref, pegl_ref, *, rows):
    step_i = pl.program_id(1)
    dk = GDN_HEAD_DIM
    nch = rows // CHUNK
    halo = SUBLANES
    heads = range(GDN_HEADS)
    items = [(c, hh) for c in range(nch) for hh in heads]
    cur = (step_i + 1) % 2
    nxt = step_i % 2

    @pl.when(step_i == 0)
    def _():
        xbuf_ref[0:halo, :] = jnp.zeros((halo, 3 * GDN_WIDTH), F32)
        state_ref[...] = jnp.zeros_like(state_ref)
        pb16_ref[...] = jnp.zeros_like(pb16_ref)
        pf32_ref[...] = jnp.zeros_like(pf32_ref)
        prhs_ref[...] = jnp.zeros_like(prhs_ref)
        pegl_ref[...] = jnp.zeros_like(pegl_ref)

    qb, kbb, kbf, qdec, kdec, decays, gsilu, rhss, egl = {}, {}, {}, {}, {}, {}, {}, {}, {}
    for n, it in enumerate(items):
        qb[it] = pb16_ref[cur, 0, n]
        kbb[it] = pb16_ref[cur, 1, n]
        kbf[it] = pb16_ref[cur, 2, n]
        qdec[it] = pb16_ref[cur, 3, n]
        kdec[it] = pb16_ref[cur, 4, n]
        decays[it] = pf32_ref[cur, 0, n]
        gsilu[it] = pf32_ref[cur, 1, n]
        rhss[it] = prhs_ref[cur, n]
        egl[it] = pegl_ref[cur, n][0:1, 0:1]

    ri = lax.broadcasted_iota(jnp.int32, (CHUNK, CHUNK), 0)
    ci = lax.broadcasted_iota(jnp.int32, (CHUNK, CHUNK), 1)
    tri = ri >= ci
    strict = ri > ci

    ng = ng_ref[...]
    kk = {it: _dot_nt(kbb[it], kbf[it]) for it in items}
    qk = {it: _dot_nt(qb[it], kbf[it]) for it in items}
    lm = {it: jnp.where(strict, kk[it] * decays[it], 0.0) for it in items}
    a_intra = {it: qk[it] * decays[it] for it in items}
    xs = {it: rhss[it] - _mm_bf16(lm[it], rhss[it]) for it in items}
    ps = lm
    for _ in range(5):
        ps = {it: _mm_bf16(ps[it], ps[it]) for it in items}
        xs = {it: xs[it] + _mm_bf16(ps[it], xs[it]) for it in items}

    state = [state_ref[hh] for hh in heads]
    for c in range(nch):
        r0 = c * CHUNK
        stb = [state[hh].astype(BF16) for hh in heads]
        ws = [_dot(xs[(c, hh)][:, dk:2 * dk].astype(BF16), stb[hh]) for hh in heads]
        qst = [_dot(qdec[(c, hh)], stb[hh]) for hh in heads]
        vn = [(xs[(c, hh)][:, 0:dk] - ws[hh]).astype(BF16) for hh in heads]
        av = [_dot(a_intra[(c, hh)].astype(BF16), vn[hh]) for hh in heads]
        kv = [_dot_tn(kdec[(c, hh)], vn[hh]) for hh in heads]
        for hh in heads:
            state[hh] = state[hh] * egl[(c, hh)] + kv[hh]
            o = qst[hh] + av[hh]
            ms = jnp.mean(o * o, axis=-1, keepdims=True)
            on = o * lax.rsqrt(ms + RMS_EPS) * ng
            o_ref[r0:r0 + CHUNK, hh * dk:(hh + 1) * dk] = on * gsilu[(c, hh)]
    for hh in heads:
        state_ref[hh] = state[hh]

    xbuf_ref[halo:halo + rows, :] = qkv_ref[...]
    y = convw_ref[CONV_K - 1:CONV_K, :] * xbuf_ref[halo:halo + rows, :]
    for j in range(CONV_K - 1):
        off = halo - (CONV_K - 1) + j
        y = y + convw_ref[j:j + 1, :] * xbuf_ref[off:off + rows, :]
    xbuf_ref[0:halo, :] = xbuf_ref[rows:rows + halo, :]
    y = y * jax.nn.sigmoid(y)

    small = small_ref[...]
    beta_all = jax.nn.sigmoid(small)
    sp_in = small + gl_ref[1:2, :]
    softplus = jnp.maximum(sp_in, 0.0) + jnp.log(1.0 + jnp.exp(-jnp.abs(sp_in)))
    g_all = gl_ref[0:1, :] * softplus
    tril_f = tri.astype(F32)
    gc_parts = []
    for c in range(nch):
        gch = g_all[c * CHUNK:(c + 1) * CHUNK, :]
        gc_parts.append(jnp.dot(tril_f, gch, preferred_element_type=F32,
                                precision=lax.Precision.HIGHEST))
    gc_all = jnp.concatenate(gc_parts, axis=0) if nch > 1 else gc_parts[0]
    pad = (-rows) % LANES
    gc_sq = jnp.concatenate([gc_all, jnp.zeros((pad, LANES), F32)], axis=0) if pad else gc_all
    gcT = gc_sq.T

    gate = gate_ref[...]
    for n, it in enumerate(items):
        c, hh = it
        r0 = c * CHUNK
        q = y[r0:r0 + CHUNK, hh * dk:(hh + 1) * dk]
        k = y[r0:r0 + CHUNK, GDN_WIDTH + hh * dk:GDN_WIDTH + (hh + 1) * dk]
        v = y[r0:r0 + CHUNK, 2 * GDN_WIDTH + hh * dk:2 * GDN_WIDTH + (hh + 1) * dk]
        q = q * lax.rsqrt(jnp.sum(q * q, axis=-1, keepdims=True) + RMS_EPS) * (dk ** -0.5)
        k = k * lax.rsqrt(jnp.sum(k * k, axis=-1, keepdims=True) + RMS_EPS)
        beta = beta_all[r0:r0 + CHUNK, BETA_LANE0 + hh:BETA_LANE0 + hh + 1]
        gcol = gc_all[r0:r0 + CHUNK, A_LANE0 + hh:A_LANE0 + hh + 1]
        grow = gcT[A_LANE0 + hh:A_LANE0 + hh + 1, r0:r0 + CHUNK]
        glast = gcT[A_LANE0 + hh:A_LANE0 + hh + 1, r0 + CHUNK - 1:r0 + CHUNK]
        eg = jnp.exp(gcol)
        kb = k * beta
        gt = gate[r0:r0 + CHUNK, hh * dk:(hh + 1) * dk]
        pb16_ref[nxt, 0, n] = q.astype(BF16)
        pb16_ref[nxt, 1, n] = kb.astype(BF16)
        pb16_ref[nxt, 2, n] = k.astype(BF16)
        pb16_ref[nxt, 3, n] = (q * eg).astype(BF16)
        pb16_ref[nxt, 4, n] = (k * jnp.exp(glast - gcol)).astype(BF16)
        pf32_ref[nxt, 0, n] = jnp.where(tri, jnp.exp(jnp.where(tri, gcol - grow, 0.0)), 0.0)
        pf32_ref[nxt, 1, n] = gt * jax.nn.sigmoid(gt)
        prhs_ref[nxt, n] = jnp.concatenate([v * beta, kb * eg], axis=1)
        pegl_ref[nxt, n] = jnp.broadcast_to(jnp.exp(glast), (SUBLANES, LANES))


def _gdn_mixer(gdn_in, small, conv_w, a_log, dt_bias, norm_g, *, batch, seq, rows=128):
    t = batch * seq
    ns = seq // rows
    gl = jnp.zeros((SUBLANES, LANES), F32)
    gl = gl.at[0, A_LANE0:A_LANE0 + GDN_HEADS].set(-jnp.exp(a_log.astype(F32)))
    gl = gl.at[1, A_LANE0:A_LANE0 + GDN_HEADS].set(dt_bias.astype(F32))
    convw = jnp.zeros((SUBLANES, 3 * GDN_WIDTH), F32).at[0:CONV_K].set(conv_w.astype(F32))
    nitems = (rows // CHUNK) * GDN_HEADS
    dk = GDN_HEAD_DIM
    rin = lambda b, i: (b * ns + jnp.minimum(i, ns - 1), 0)
    rout = lambda b, i: (b * ns + jnp.maximum(i - 1, 0), 0)
    const = lambda b, i: (0, 0)
    return pl.pallas_call(
        functools.partial(_gdn_kernel, rows=rows),
        out_shape=jax.ShapeDtypeStruct((t, GDN_WIDTH), F32),
        grid=(batch, ns + 1),
        in_specs=[
            pl.BlockSpec((rows, 3 * GDN_WIDTH), rin),
            pl.BlockSpec((rows, GDN_WIDTH), lambda b, i: (b * ns + jnp.minimum(i, ns - 1), 3)),
            pl.BlockSpec((rows, LANES), rin),
            pl.BlockSpec((SUBLANES, 3 * GDN_WIDTH), const),
            pl.BlockSpec((SUBLANES, LANES), const),
            pl.BlockSpec((1, GDN_HEAD_DIM), const),
        ],
        out_specs=pl.BlockSpec((rows, GDN_WIDTH), rout),
        scratch_shapes=[
            pltpu.VMEM((rows + SUBLANES, 3 * GDN_WIDTH), F32),
            pltpu.VMEM((GDN_HEADS, dk, dk), F32),
            pltpu.VMEM((2, 5, nitems, CHUNK, dk), BF16),
            pltpu.VMEM((2, 2, nitems, CHUNK, dk), F32),
            pltpu.VMEM((2, nitems, CHUNK, 2 * dk), F32),
            pltpu.VMEM((2, nitems, SUBLANES, LANES), F32),
        ],
        compiler_params=_cparams(("parallel", "arbitrary")),
        name="gated_deltanet",
    )(gdn_in, gdn_in, small, convw, gl, norm_g.reshape(1, -1).astype(F32))


def _cmul(ar, ai, br, bi):
    return ar * br - ai * bi, ar * bi + ai * br


def _s5_kernel(u_ref, bblk_ref, ccat_ref, apow_ref, d_ref, wglu_ref, o_ref,
               bu_ref, x_ref, carry_ref, *, tm):
    n = S5_LANES

    @pl.when(pl.program_id(1) == 0)
    def _():
        carry_ref[...] = jnp.zeros_like(carry_ref)

    u = u_ref[...]
    bu_ref[...] = _dot(u.astype(BF16), bblk_ref[...])

    def group(gidx, carry):
        c_re, c_im = carry
        r0 = pl.multiple_of(gidx * SUBLANES, SUBLANES)
        x_re = bu_ref[pl.ds(r0, SUBLANES), 0:n]
        x_im = bu_ref[pl.ds(r0, SUBLANES), n:2 * n]
        for lvl, d in enumerate((1, 2, 4)):
            a_re = apow_ref[lvl * 2 * SUBLANES:lvl * 2 * SUBLANES + SUBLANES, :]
            a_im = apow_ref[lvl * 2 * SUBLANES + SUBLANES:(lvl + 1) * 2 * SUBLANES, :]
            s_re = pltpu.roll(x_re, d, 0)
            s_im = pltpu.roll(x_im, d, 0)
            t_re, t_im = _cmul(a_re, a_im, s_re, s_im)
            x_re = x_re + t_re
            x_im = x_im + t_im
        p_re = apow_ref[6 * SUBLANES:7 * SUBLANES, :]
        p_im = apow_ref[7 * SUBLANES:8 * SUBLANES, :]
        t_re, t_im = _cmul(p_re, p_im, c_re, c_im)
        x_re = x_re + t_re
        x_im = x_im + t_im
        x_ref[pl.ds(r0, SUBLANES), 0:n] = x_re
        x_ref[pl.ds(r0, SUBLANES), n:2 * n] = x_im
        return x_re[SUBLANES - 1:SUBLANES, :], x_im[SUBLANES - 1:SUBLANES, :]

    c_re, c_im = lax.fori_loop(0, tm // SUBLANES, group,
                               (carry_ref[0:1, :], carry_ref[1:2, :]))
    carry_ref[0:1, :] = c_re
    carry_ref[1:2, :] = c_im

    yv = _dot(x_ref[...].astype(BF16), ccat_ref[...]) + d_ref[...] * u
    yv = 0.5 * yv * (1.0 + jnp.tanh(0.7978845608028654 * (yv + 0.044715 * (yv * yv * yv))))
    z = _dot(yv.astype(BF16), wglu_ref[...])
    o_ref[...] = yv * jax.nn.sigmoid(z)


def _s5_params(lam_re, lam_im, log_dt, b_re, b_im, c_re, c_im):
    f32 = F32
    lre, lim = lam_re.astype(f32), lam_im.astype(f32)
    dt = jnp.exp(log_dt.astype(f32))[:, None]
    mag = jnp.exp(lre * dt)
    ab_re, ab_im = mag * jnp.cos(lim * dt), mag * jnp.sin(lim * dt)
    num_re, num_im = ab_re - 1.0, ab_im
    den = lre * lre + lim * lim
    coef_re = (num_re * lre + num_im * lim) / den
    coef_im = (num_im * lre - num_re * lim) / den
    br, bi = b_re.astype(f32), b_im.astype(f32)
    bb_re = coef_re[..., None] * br - coef_im[..., None] * bi
    bb_im = coef_re[..., None] * bi + coef_im[..., None] * br
    eye = jnp.eye(S5_GROUPS, dtype=f32)
    blk_re = jnp.einsum('gph,gk->ghkp', bb_re, eye).reshape(S5_WIDTH, S5_LANES)
    blk_im = jnp.einsum('gph,gk->ghkp', bb_im, eye).reshape(S5_WIDTH, S5_LANES)
    bblk = jnp.concatenate([blk_re, blk_im], axis=1).astype(BF16)
    cb_re = jnp.einsum('ghp,gk->gpkh', c_re.astype(f32), eye).reshape(S5_LANES, S5_WIDTH)
    cb_im = jnp.einsum('ghp,gk->gpkh', c_im.astype(f32), eye).reshape(S5_LANES, S5_WIDTH)
    ccat = jnp.concatenate([cb_re, -cb_im], axis=0).astype(BF16)
    a1 = (ab_re.reshape(1, -1), ab_im.reshape(1, -1))
    pows = [a1]
    for _ in range(SUBLANES - 1):
        pows.append(_cmul(pows[-1][0], pows[-1][1], a1[0], a1[1]))
    rid = jnp.arange(SUBLANES)[:, None]
    rows = []
    for d in (1, 2, 4):
        mask = (rid >= d).astype(f32)
        rows.append(mask * pows[d - 1][0])
        rows.append(mask * pows[d - 1][1])
    rows.append(jnp.concatenate([pows[r][0] for r in range(SUBLANES)], axis=0))
    rows.append(jnp.concatenate([pows[r][1] for r in range(SUBLANES)], axis=0))
    apow = jnp.concatenate(rows, axis=0)
    return bblk, ccat, apow


def _s5_mixer(cu, lam_re, lam_im, log_dt, b_re, b_im, c_re, c_im, d, w_glu, *, batch, seq, tm=256):
    t = batch * seq
    ns = seq // tm
    bblk, ccat, apow = _s5_params(lam_re, lam_im, log_dt, b_re, b_im, c_re, c_im)
    row = lambda b, i: (b * ns + i, 0)
    const = lambda b, i: (0, 0)
    return pl.pallas_call(
        functools.partial(_s5_kernel, tm=tm),
        out_shape=jax.ShapeDtypeStruct((t, S5_WIDTH), F32),
        grid=(batch, ns),
        in_specs=[
            pl.BlockSpec((tm, S5_WIDTH), row),
            pl.BlockSpec(bblk.shape, const),
            pl.BlockSpec(ccat.shape, const),
            pl.BlockSpec(apow.shape, const),
            pl.BlockSpec((1, S5_WIDTH), const),
            pl.BlockSpec((S5_WIDTH, S5_WIDTH), const),
        ],
        out_specs=pl.BlockSpec((tm, S5_WIDTH), row),
        scratch_shapes=[
            pltpu.VMEM((tm, 2 * S5_LANES), F32),
            pltpu.VMEM((tm, 2 * S5_LANES), F32),
            pltpu.VMEM((SUBLANES, S5_LANES), F32),
        ],
        compiler_params=_cparams(("parallel", "arbitrary")),
        name="s5_mixer",
    )(cu, bblk, ccat, apow, d.reshape(1, -1).astype(F32), w_glu.astype(BF16))


def _route_rows(lt, n_tok):
    g = [lt[r:r + 1, :] for r in range(N_EXPERT_GROUPS)]
    gm = functools.reduce(jnp.maximum, g)
    gsum = functools.reduce(lambda a, b: a + b, [jnp.exp(x - gm) for x in g])
    g_p = 1.0 / gsum
    taken = jnp.zeros_like(gm) > 1.0
    g_hot = []
    for x in g:
        hit = jnp.logical_and(x == gm, jnp.logical_not(taken))
        g_hot.append(hit)
        taken = jnp.logical_or(taken, hit)
    e_sel = []
    for j in range(EXPERTS_PER_GROUP):
        acc = jnp.zeros_like(gm)
        for gi in range(N_EXPERT_GROUPS):
            r = 8 + gi * EXPERTS_PER_GROUP + j
            acc = acc + jnp.where(g_hot[gi], lt[r:r + 1, :], 0.0)
        e_sel.append(acc)
    m1 = functools.reduce(jnp.maximum, e_sel)
    taken = jnp.zeros_like(gm) > 1.0
    hot1 = []
    for x in e_sel:
        hit = jnp.logical_and(x == m1, jnp.logical_not(taken))
        hot1.append(hit)
        taken = jnp.logical_or(taken, hit)
    rest = [jnp.where(hh, NEG_BIG, x) for hh, x in zip(hot1, e_sel)]
    m2 = functools.reduce(jnp.maximum, rest)
    taken = jnp.zeros_like(gm) > 1.0
    hot2 = []
    for hh, x in zip(hot1, rest):
        hit = jnp.logical_and(jnp.logical_and(x == m2, jnp.logical_not(hh)), jnp.logical_not(taken))
        hot2.append(hit)
        taken = jnp.logical_or(taken, hit)
    e2 = jnp.exp(m2 - m1)
    w1 = g_p / (1.0 + e2)
    w2 = g_p * e2 / (1.0 + e2)
    rows = []
    for gi in range(N_EXPERT_GROUPS):
        for j in range(EXPERTS_PER_GROUP):
            val = jnp.where(hot1[j], w1, 0.0) + jnp.where(hot2[j], w2, 0.0)
            rows.append(jnp.where(g_hot[gi], val, 0.0))
    return jnp.concatenate(rows, axis=0)


def _outproj_kernel(h_ref, yaT_ref, yb_ref, yc_ref, wa_ref, wb_ref, wc_ref, g_ref, b_ref,
                    wrT_ref, br_ref, h1_ref, h1b_ref, comb_ref, *, tm):
    ya = yaT_ref[0].T
    mix = _dot(ya.astype(BF16), wa_ref[...])
    mix = mix + _dot(yb_ref[...].astype(BF16), wb_ref[...])
    mix = mix + _dot(yc_ref[...].astype(BF16), wc_ref[...])
    h1 = _layer_norm(ALPHA * h_ref[...] + mix, g_ref[...], b_ref[...])
    h1_ref[...] = h1
    h1b_ref[...] = h1.astype(BF16)
    h_hi, h_lo = _split_bf16(h1)
    w_hi = wrT_ref[0:LANES, :]
    w_lo = wrT_ref[LANES:2 * LANES, :]
    lt = _dot_nt(w_hi, h_hi) + _dot_nt(w_hi, h_lo) + _dot_nt(w_lo, h_hi) + br_ref[...]
    comb = _route_rows(lt, tm)
    combp = jnp.concatenate([comb, jnp.zeros((LANES - N_EXPERTS, tm), F32)], axis=0)
    comb_ref[...] = combp.T


def _out_projection(h, yaT, yb, yc, w_out, ln_g, ln_b, w_grp, b_grp, w_exp, b_exp,
                    *, batch, seq, tm=512):
    t = batch * seq
    nt = seq // tm
    wa = w_out[0:DA_WIDTH].astype(BF16)
    wb = w_out[DA_WIDTH:DA_WIDTH + GDN_WIDTH].astype(BF16)
    wc = w_out[DA_WIDTH + GDN_WIDTH:].astype(BF16)
    wr = jnp.zeros((D_MODEL, LANES), F32)
    wr = wr.at[:, 0:N_EXPERT_GROUPS].set(w_grp.astype(F32)).at[:, 8:8 + N_EXPERTS].set(w_exp.astype(F32))
    wrT = wr.T
    wr_hi = wrT.astype(BF16)
    wr_lo = (wrT - wr_hi.astype(F32)).astype(BF16)
    wr_cat = jnp.concatenate([wr_hi, wr_lo], axis=0)
    br = jnp.zeros((LANES, 1), F32)
    br = br.at[0:N_EXPERT_GROUPS, 0].set(b_grp.astype(F32)).at[8:8 + N_EXPERTS, 0].set(b_exp.astype(F32))
    row = lambda b, i: (b * nt + i, 0)
    const = lambda b, i: (0, 0)
    return pl.pallas_call(
        functools.partial(_outproj_kernel, tm=tm),
        out_shape=[
            jax.ShapeDtypeStruct((t, D_MODEL), F32),
            jax.ShapeDtypeStruct((t, D_MODEL), BF16),
            jax.ShapeDtypeStruct((t, LANES), F32),
        ],
        grid=(batch, nt),
        in_specs=[
            pl.BlockSpec((tm, D_MODEL), row),
            pl.BlockSpec((1, DA_WIDTH, tm), lambda b, i: (b, 0, i)),
            pl.BlockSpec((tm, GDN_WIDTH), row),
            pl.BlockSpec((tm, S5_WIDTH), row),
            pl.BlockSpec(wa.shape, const),
            pl.BlockSpec(wb.shape, const),
            pl.BlockSpec(wc.shape, const),
            pl.BlockSpec((1, D_MODEL), const),
            pl.BlockSpec((1, D_MODEL), const),
            pl.BlockSpec(wr_cat.shape, const),
            pl.BlockSpec((LANES, 1), const),
        ],
        out_specs=[
            pl.BlockSpec((tm, D_MODEL), row),
            pl.BlockSpec((tm, D_MODEL), row),
            pl.BlockSpec((tm, LANES), row),
        ],
        compiler_params=_cparams(("parallel", "parallel")),
        name="out_projection_router",
    )(h, yaT, yb, yc, wa, wb, wc, ln_g.reshape(1, -1), ln_b.reshape(1, -1), wr_cat, br)


MOE_EXPERTS_PER_STEP = 2


def _moe_kernel(hb_ref, h1_ref, comb_ref, w1_ref, w3_ref, w2_ref, g_ref, b_ref, o_ref, acc_ref):
    s = pl.program_id(1)
    eps = MOE_EXPERTS_PER_STEP

    @pl.when(s == 0)
    def _():
        acc_ref[...] = jnp.zeros_like(acc_ref)

    x = hb_ref[...]
    lane = lax.broadcasted_iota(jnp.int32, (1, LANES), 1)
    comb = comb_ref[...]
    ups = [(_dot(x, w1_ref[j]), _dot(x, w3_ref[j])) for j in range(eps)]
    y = None
    for j, (a, b) in enumerate(ups):
        c = jnp.sum(jnp.where(lane == s * eps + j, comb, 0.0), axis=1, keepdims=True)
        hid = (a * jax.nn.sigmoid(a) * b * c).astype(BF16)
        part = _dot(hid, w2_ref[j])
        y = part if y is None else y + part
    acc_ref[...] += y

    @pl.when(s == N_EXPERTS // eps - 1)
    def _():
        o_ref[...] = _layer_norm(ALPHA * h1_ref[...] + acc_ref[...], g_ref[...], b_ref[...])


def _moe(h1, h1b, comb, w1, w3, w2, ln_g, ln_b, *, tm=1024):
    t = h1.shape[0]
    nt = t // tm
    eps = MOE_EXPERTS_PER_STEP
    row = lambda i, e: (i, 0)
    const = lambda i, e: (0, 0)
    return pl.pallas_call(
        _moe_kernel,
        out_shape=jax.ShapeDtypeStruct((t, D_MODEL), F32),
        grid=(nt, N_EXPERTS // eps),
        in_specs=[
            pl.BlockSpec((tm, D_MODEL), row),
            pl.BlockSpec((tm, D_MODEL), row),
            pl.BlockSpec((tm, LANES), row),
            pl.BlockSpec((eps, D_MODEL, D_EXPERT), lambda i, e: (e, 0, 0)),
            pl.BlockSpec((eps, D_MODEL, D_EXPERT), lambda i, e: (e, 0, 0)),
            pl.BlockSpec((eps, D_EXPERT, D_MODEL), lambda i, e: (e, 0, 0)),
            pl.BlockSpec((1, D_MODEL), const),
            pl.BlockSpec((1, D_MODEL), const),
        ],
        out_specs=pl.BlockSpec((tm, D_MODEL), row),
        scratch_shapes=[pltpu.VMEM((tm, D_MODEL), F32)],
        compiler_params=_cparams(("parallel", "arbitrary")),
        name="moe_ffn",
    )(h1b, h1, comb, w1, w3, w2, ln_g.reshape(1, -1), ln_b.reshape(1, -1))


def _split_w_in(w):
    wt = w.T
    o = 0
    wq = wt[o:o + DA_WIDTH]; o += DA_WIDTH
    wk = wt[o:o + DA_WIDTH]; o += DA_WIDTH
    wv = wt[o:o + DA_WIDTH]; o += DA_WIDTH
    wg = wt[o:o + 4 * GDN_WIDTH]; o += 4 * GDN_WIDTH
    wbeta = wt[o:o + GDN_HEADS]; o += GDN_HEADS
    wa = wt[o:o + GDN_HEADS]; o += GDN_HEADS
    wc = wt[o:o + S5_WIDTH]
    kd = 2 * DA_HEAD_DIM
    zrows = lambda n: jnp.zeros((n, D_MODEL), w.dtype)
    wk_pad = [p for hh in range(DA_HEADS) for p in (wk[hh * kd:(hh + 1) * kd], zrows(K_PAD - kd))]
    wsm = [zrows(BETA_LANE0), wbeta, zrows(A_LANE0 - BETA_LANE0 - GDN_HEADS), wa,
           zrows(LANES - A_LANE0 - GDN_HEADS)]
    kone = jnp.zeros((1, DA_HEADS, K_PAD), F32).at[:, :, kd].set(1.0).reshape(1, DA_HEADS * K_PAD)
    w_all = jnp.concatenate([wq, wv] + wk_pad + [wg] + wsm + [wc], axis=0).astype(BF16)
    return w_all, kone


def kernel(x, ln_in_g, ln_in_b, w_in, w_out, lam_q1, lam_k1, lam_q2, lam_k2, diff_norm_g, dn_conv_w, dn_a_log, dn_dt_bias, dn_norm_g, s5_lambda_re, s5_lambda_im, s5_log_dt, s5_b_re, s5_b_im, s5_c_re, s5_c_im, s5_d, s5_w_glu, ln1_g, ln1_b, moe_w_grp, moe_b_grp, moe_w_exp, moe_b_exp, moe_w1, moe_w3, moe_w2, ln2_g, ln2_b):
    batch, seq, d = x.shape
    h = x.reshape(batch * seq, d)
    for l in range(DEPTH):
        lam_init = 0.8 - 0.6 * math.exp(-0.3 * l)
        wts = _split_w_in(w_in[l])
        outs = _in_projection(h, ln_in_g, ln_in_b, wts, batch=batch, seq=seq, apply_ln=(l == 0))
        if l == 0:
            h, qT, vT, k, gdn_in, small, cu = outs
        else:
            qT, vT, k, gdn_in, small, cu = outs
        lam = (jnp.exp(jnp.sum(lam_q1[l] * lam_k1[l])) - jnp.exp(jnp.sum(lam_q2[l] * lam_k2[l]))
               ).astype(F32) + lam_init
        yaT = _diff_attention(lam, qT, k, vT, diff_norm_g[l].astype(F32), lam_init=lam_init)
        yb = _gdn_mixer(gdn_in, small, dn_conv_w[l], dn_a_log[l], dn_dt_bias[l], dn_norm_g[l],
                        batch=batch, seq=seq)
        yc = _s5_mixer(cu, s5_lambda_re[l], s5_lambda_im[l], s5_log_dt[l], s5_b_re[l], s5_b_im[l],
                       s5_c_re[l], s5_c_im[l], s5_d[l], s5_w_glu[l], batch=batch, seq=seq)
        h1, h1b, comb = _out_projection(h, yaT, yb, yc, w_out[l], ln1_g[l], ln1_b[l],
                                        moe_w_grp[l], moe_b_grp[l], moe_w_exp[l], moe_b_exp[l],
                                        batch=batch, seq=seq)
        h = _moe(h1, h1b, comb, moe_w1[l].astype(BF16), moe_w3[l].astype(BF16),
                 moe_w2[l].astype(BF16), ln2_g[l], ln2_b[l])
    return h.reshape(batch, seq, d)
```

```python
import functools
import math

import jax
import jax.numpy as jnp
from jax import lax
from jax.experimental import pallas as pl
from jax.experimental.pallas import tpu as pltpu

F32 = jnp.float32
BF16 = jnp.bfloat16

D_MODEL = 1024
DEPTH = 2
CHUNK = 64
DA_HEADS = 6
DA_HEAD_DIM = 32
DA_V_DIM = 64
DA_WIDTH = 384
GDN_HEADS = 6
GDN_HEAD_DIM = 64
GDN_WIDTH = 384
CONV_K = 4
S5_GROUP_DIM = 16
S5_GROUPS = 16
S5_WIDTH = 256
S5_STATE = 64
S5_LANES = S5_GROUPS * S5_STATE
N_EXPERT_GROUPS = 4
EXPERTS_PER_GROUP = 4
N_EXPERTS = 16
D_EXPERT = 512
ALPHA = (2 * DEPTH) ** 0.25
LN_EPS = 1e-5
RMS_EPS = 1e-6
LOG2E = 1.4426950408889634

V7X_VMEM_LIMIT_BYTES = 56 * 1024 * 1024
SUBLANES = 8
LANES = 128
NEG_BIG = -1e30
FAST_MAX_LOG2 = 100.0
K_PAD = LANES

BETA_LANE0 = 0
A_LANE0 = 8


def _cparams(sem):
    return pltpu.CompilerParams(dimension_semantics=sem, vmem_limit_bytes=V7X_VMEM_LIMIT_BYTES)


def _layer_norm(x, g, b):
    mu = jnp.mean(x, axis=-1, keepdims=True)
    xc = x - mu
    var = jnp.mean(xc * xc, axis=-1, keepdims=True)
    return xc * lax.rsqrt(var + LN_EPS) * g + b


def _dot(a, b):
    return jnp.dot(a, b, preferred_element_type=F32)


def _dot_nt(a, b):
    return lax.dot_general(a, b, (((1,), (1,)), ((), ())), preferred_element_type=F32)


def _dot_tn(a, b):
    return lax.dot_general(a, b, (((0,), (0,)), ((), ())), preferred_element_type=F32)


def _proj_kernel(x_ref, g_ref, b_ref, w_ref, kone_ref, *out_refs, apply_ln, q_scale):
    edges = (0, DA_WIDTH, 2 * DA_WIDTH, 2 * DA_WIDTH + DA_HEADS * K_PAD)
    edges = edges + (edges[-1] + 4 * GDN_WIDTH, edges[-1] + 4 * GDN_WIDTH + LANES,
                     edges[-1] + 4 * GDN_WIDTH + LANES + S5_WIDTH)
    wq_ref, wv_ref, wk_ref, wg_ref, wsm_ref, wc_ref = (
        w_ref.at[lo:hi, :] for lo, hi in zip(edges[:-1], edges[1:]))
    if apply_ln:
        h_ref, qT_ref, vT_ref, k_ref, gdn_ref, small_ref, cu_ref = out_refs
        h = _layer_norm(x_ref[...], g_ref[...], b_ref[...])
        h_ref[...] = h
    else:
        qT_ref, vT_ref, k_ref, gdn_ref, small_ref, cu_ref = out_refs
        h = x_ref[...]
    hb = h.astype(BF16)
    qT_ref[0] = (_dot_nt(wq_ref[...], hb) * q_scale).astype(BF16)
    vT_ref[0] = _dot_nt(wv_ref[...], hb).astype(BF16)
    k_ref[0] = (_dot_nt(hb, wk_ref[...]) + kone_ref[...]).astype(BF16)
    gdn_ref[...] = _dot_nt(hb, wg_ref[...])
    small_ref[...] = _dot_nt(hb, wsm_ref[...])
    cu_ref[...] = _dot_nt(hb, wc_ref[...])


def _in_projection(x2d, g, b, wts, *, batch, seq, apply_ln, tm=512):
    t = batch * seq
    nt = seq // tm
    w_all, kone = wts
    kw = DA_HEADS * K_PAD
    q_scale = (DA_HEAD_DIM ** -0.5) * LOG2E
    row = lambda bi, i: (bi * nt + i, 0)
    const = lambda bi, i: (0, 0)
    out_shape = [
        jax.ShapeDtypeStruct((batch, DA_WIDTH, seq), BF16),
        jax.ShapeDtypeStruct((batch, DA_WIDTH, seq), BF16),
        jax.ShapeDtypeStruct((batch, seq, kw), BF16),
        jax.ShapeDtypeStruct((t, 4 * GDN_WIDTH), F32),
        jax.ShapeDtypeStruct((t, LANES), F32),
        jax.ShapeDtypeStruct((t, S5_WIDTH), F32),
    ]
    out_specs = [
        pl.BlockSpec((1, DA_WIDTH, tm), lambda bi, i: (bi, 0, i)),
        pl.BlockSpec((1, DA_WIDTH, tm), lambda bi, i: (bi, 0, i)),
        pl.BlockSpec((1, tm, kw), lambda bi, i: (bi, i, 0)),
        pl.BlockSpec((tm, 4 * GDN_WIDTH), row),
        pl.BlockSpec((tm, LANES), row),
        pl.BlockSpec((tm, S5_WIDTH), row),
    ]
    if apply_ln:
        out_shape = [jax.ShapeDtypeStruct((t, D_MODEL), F32)] + out_shape
        out_specs = [pl.BlockSpec((tm, D_MODEL), row)] + out_specs
    in_specs = [
        pl.BlockSpec((tm, D_MODEL), row),
        pl.BlockSpec((1, D_MODEL), const),
        pl.BlockSpec((1, D_MODEL), const),
        pl.BlockSpec(w_all.shape, const),
        pl.BlockSpec(kone.shape, const),
    ]
    return pl.pallas_call(
        functools.partial(_proj_kernel, apply_ln=apply_ln, q_scale=q_scale),
        out_shape=out_shape,
        grid=(batch, nt),
        in_specs=in_specs,
        out_specs=out_specs,
        compiler_params=_cparams(("parallel", "parallel")),
        name="in_projection_ln" if apply_ln else "in_projection",
    )(x2d, g.reshape(1, -1), b.reshape(1, -1), w_all, kone)


def _attn_kernel(lam_ref, qT_ref, k_ref, vT_ref, g_ref, o_ref,
                 qbd_ref, pa_ref, pb_ref, m_ref, l_ref, acc_ref, cm_ref, *, tq, tk, out_scale):
    i = pl.program_id(2)
    dh = DA_HEAD_DIM
    ref_row = 2 * dh
    n_diag = 4
    assert tq == n_diag * tk

    def qk(j):
        start = pl.multiple_of(j * tk, tk)
        return _dot(k_ref[0, pl.ds(start, tk), :], qbd_ref[...])

    def pv(j, p):
        start = pl.multiple_of(j * tk, tk)
        return _dot(vT_ref[0, :, pl.ds(start, tk)], p.astype(BF16))

    def init_stats():
        m_ref[...] = jnp.full_like(m_ref, NEG_BIG)
        l_ref[...] = jnp.zeros_like(l_ref)
        acc_ref[...] = jnp.zeros_like(acc_ref)

    def exact_step(j, mask):
        s = qk(j)
        if mask is not None:
            s = jnp.where(mask, s, NEG_BIG)
        m_old = m_ref[...]
        m_new = jnp.maximum(m_old, jnp.max(s, axis=0, keepdims=True))
        alpha = jnp.exp2(m_old - m_new)
        p = jnp.exp2(s - m_new)
        l_ref[...] = alpha * l_ref[...] + jnp.sum(p, axis=0, keepdims=True)
        m_ref[...] = m_new
        acc_ref[...] = alpha * acc_ref[...] + pv(j, p)

    def score_exp(j, mask=None):
        s = qk(j)
        if mask is not None:
            s = jnp.where(mask, s, NEG_BIG)
        cm_ref[...] = jnp.maximum(cm_ref[...], jnp.max(s, axis=0, keepdims=True))
        p = jnp.exp2(s)
        l_ref[...] += jnp.sum(p, axis=0, keepdims=True)
        return p.astype(BF16)

    def accumulate(p_ref, j):
        acc_ref[...] += pv(j, p_ref[...])

    kc = lax.broadcasted_iota(jnp.int32, (tk, 1), 0) // CHUNK
    col = lax.broadcasted_iota(jnp.int32, (1, 2 * tq), 1)
    qc = jnp.where(col >= tq, col - tq, col) // CHUNK

    def diag_mask(d):
        return (kc + d * (tk // CHUNK)) <= qc

    qbd_ref[...] = jnp.zeros_like(qbd_ref)
    q = qT_ref[0]
    qbd_ref[0:dh, 0:tq] = q[0:dh]
    qbd_ref[dh:2 * dh, tq:2 * tq] = q[dh:2 * dh]
    init_stats()

    off_diag = i > 0
    t_diag = n_diag * i
    k0 = pl.multiple_of(t_diag * tk, tk)
    s0 = _dot(k_ref[0, pl.ds(k0, CHUNK), :], qbd_ref[...])
    mref = jnp.max(s0, axis=0, keepdims=True).astype(BF16)
    qbd_ref[ref_row:ref_row + 16, :] = jnp.broadcast_to(-mref, (16, 2 * tq))
    cm_ref[...] = jnp.zeros_like(cm_ref)

    pa_ref[...] = score_exp(t_diag, diag_mask(0))
    pb_ref[...] = score_exp(t_diag + 1, diag_mask(1))
    accumulate(pa_ref, t_diag)
    pa_ref[...] = score_exp(t_diag + 2, diag_mask(2))
    accumulate(pb_ref, t_diag + 1)
    pb_ref[...] = score_exp(t_diag + 3, diag_mask(3))
    accumulate(pa_ref, t_diag + 2)
    accumulate(pb_ref, t_diag + 3)

    @pl.when(off_diag)
    def _():
        pa_ref[...] = score_exp(0)

        def quad(t, lookahead):
            pb_ref[...] = score_exp(t + 1)
            accumulate(pa_ref, t)
            pa_ref[...] = score_exp(t + 2)
            accumulate(pb_ref, t + 1)
            pb_ref[...] = score_exp(t + 3)
            accumulate(pa_ref, t + 2)
            if lookahead:
                pa_ref[...] = score_exp(t + 4)
            accumulate(pb_ref, t + 3)

        def octet(t, lookahead):
            quad(t, True)
            quad(t + 4, lookahead)

        def octet_body(r, carry):
            octet(8 * r, True)
            return carry

        odd = i % 2
        n_loop = i // 2 - 1 + odd
        lax.fori_loop(0, n_loop, octet_body, 0)
        t = 8 * n_loop

        @pl.when(odd == 1)
        def _():
            quad(t, False)

        @pl.when(odd == 0)
        def _():
            octet(t, False)

    @pl.when(jnp.max(cm_ref[...]) > FAST_MAX_LOG2)
    def _():
        qbd_ref[ref_row:ref_row + 16, :] = jnp.zeros((16, 2 * tq), BF16)
        init_stats()

        def body(j, carry):
            exact_step(j, None)
            return carry

        lax.fori_loop(0, t_diag, body, 0)
        for d in range(n_diag):
            exact_step(t_diag + d, diag_mask(d))

    l = l_ref[...]
    acc = acc_ref[...]
    lam = lam_ref[0]
    o = acc[:, 0:tq] / l[:, 0:tq] - lam * (acc[:, tq:2 * tq] / l[:, tq:2 * tq])
    ms = jnp.mean(o * o, axis=0, keepdims=True)
    o_ref[0] = o * lax.rsqrt(ms + RMS_EPS) * g_ref[...] * out_scale


def _diff_attention(lam, qT, k, vT, norm_g, *, lam_init, tq=1024, tk=256):
    batch, _, seq = qT.shape
    nq = seq // tq
    dv = DA_V_DIM
    return pl.pallas_call(
        functools.partial(_attn_kernel, tq=tq, tk=tk, out_scale=1.0 - lam_init),
        out_shape=jax.ShapeDtypeStruct((batch, DA_WIDTH, seq), F32),
        grid=(batch, DA_HEADS, nq),
        in_specs=[
            pl.BlockSpec(memory_space=pltpu.SMEM),
            pl.BlockSpec((1, dv, tq), lambda b, h, i: (b, h, i)),
            pl.BlockSpec((1, seq, LANES), lambda b, h, i: (b, 0, h)),
            pl.BlockSpec((1, dv, seq), lambda b, h, i: (b, h, 0)),
            pl.BlockSpec((dv, 1), lambda b, h, i: (0, 0)),
        ],
        out_specs=pl.BlockSpec((1, dv, tq), lambda b, h, i: (b, h, i)),
        scratch_shapes=[
            pltpu.VMEM((LANES, 2 * tq), BF16),
            pltpu.VMEM((tk, 2 * tq), BF16),
            pltpu.VMEM((tk, 2 * tq), BF16),
            pltpu.VMEM((1, 2 * tq), F32),
            pltpu.VMEM((1, 2 * tq), F32),
            pltpu.VMEM((dv, 2 * tq), F32),
            pltpu.VMEM((1, 2 * tq), F32),
        ],
        compiler_params=_cparams(("parallel", "parallel", "parallel")),
        name="diff_attention",
    )(lam.reshape(1), qT, k, vT, norm_g.reshape(dv, 1))


def _split_bf16(x):
    hi = x.astype(BF16)
    lo = (x - hi.astype(F32)).astype(BF16)
    return hi, lo


def _mm_bf16(a, b):
    return _dot(a.astype(BF16), b.astype(BF16))


def _gdn_kernel(qkv_ref, gate_ref, small_ref, convw_ref, gl_ref, ng_ref, o_ref,
                xbuf_ref, state_ref, pb16_ref, pf32_ref, prhs_ref, pegl_ref, *, rows):
    step_i = pl.program_id(1)
    dk = GDN_HEAD_DIM
    nch = rows // CHUNK
    halo = SUBLANES
    heads = range(GDN_HEADS)
    items = [(c, hh) for c in range(nch) for hh in heads]
    cur = (step_i + 1) % 2
    nxt = step_i % 2

    @pl.when(step_i == 0)
    def _():
        xbuf_ref[0:halo, :] = jnp.zeros((halo, 3 * GDN_WIDTH), F32)
        state_ref[...] = jnp.zeros_like(state_ref)
        pb16_ref[...] = jnp.zeros_like(pb16_ref)
        pf32_ref[...] = jnp.zeros_like(pf32_ref)
        prhs_ref[...] = jnp.zeros_like(prhs_ref)
        pegl_ref[...] = jnp.zeros_like(pegl_ref)

    qb, kbb, kbf, qdec, kdec, decays, gsilu, rhss, egl = {}, {}, {}, {}, {}, {}, {}, {}, {}
    for n, it in enumerate(items):
        qb[it] = pb16_ref[cur, 0, n]
        kbb[it] = pb16_ref[cur, 1, n]
        kbf[it] = pb16_ref[cur, 2, n]
        qdec[it] = pb16_ref[cur, 3, n]
        kdec[it] = pb16_ref[cur, 4, n]
        decays[it] = pf32_ref[cur, 0, n]
        gsilu[it] = pf32_ref[cur, 1, n]
        rhss[it] = prhs_ref[cur, n]
        egl[it] = pegl_ref[cur, n][0:1, 0:1]

    ri = lax.broadcasted_iota(jnp.int32, (CHUNK, CHUNK), 0)
    ci = lax.broadcasted_iota(jnp.int32, (CHUNK, CHUNK), 1)
    tri = ri >= ci
    strict = ri > ci

    ng = ng_ref[...]
    kk = {it: _dot_nt(kbb[it], kbf[it]) for it in items}
    qk = {it: _dot_nt(qb[it], kbf[it]) for it in items}
    lm = {it: jnp.where(strict, kk[it] * decays[it], 0.0) for it in items}
    a_intra = {it: qk[it] * decays[it] for it in items}
    xs = {it: rhss[it] - _mm_bf16(lm[it], rhss[it]) for it in items}
    ps = lm
    for _ in range(5):
        ps = {it: _mm_bf16(ps[it], ps[it]) for it in items}
        xs = {it: xs[it] + _mm_bf16(ps[it], xs[it]) for it in items}

    state = [state_ref[hh] for hh in heads]
    for c in range(nch):
        r0 = c * CHUNK
        stb = [state[hh].astype(BF16) for hh in heads]
        ws = [_dot(xs[(c, hh)][:, dk:2 * dk].astype(BF16), stb[hh]) for hh in heads]
        qst = [_dot(qdec[(c, hh)], stb[hh]) for hh in heads]
        vn = [(xs[(c, hh)][:, 0:dk] - ws[hh]).astype(BF16) for hh in heads]
        av = [_dot(a_intra[(c, hh)].astype(BF16), vn[hh]) for hh in heads]
        kv = [_dot_tn(kdec[(c, hh)], vn[hh]) for hh in heads]
        for hh in heads:
            state[hh] = state[hh] * egl[(c, hh)] + kv[hh]
            o = qst[hh] + av[hh]
            ms = jnp.mean(o * o, axis=-1, keepdims=True)
            on = o * lax.rsqrt(ms + RMS_EPS) * ng
            o_ref[r0:r0 + CHUNK, hh * dk:(hh + 1) * dk] = on * gsilu[(c, hh)]
    for hh in heads:
        state_ref[hh] = state[hh]

    xbuf_ref[halo:halo + rows, :] = qkv_ref[...]
    y = convw_ref[CONV_K - 1:CONV_K, :] * xbuf_ref[halo:halo + rows, :]
    for j in range(CONV_K - 1):
        off = halo - (CONV_K - 1) + j
        y = y + convw_ref[j:j + 1, :] * xbuf_ref[off:off + rows, :]
    xbuf_ref[0:halo, :] = xbuf_ref[rows:rows + halo, :]
    y = y * jax.nn.sigmoid(y)

    small = small_ref[...]
    beta_all = jax.nn.sigmoid(small)
    sp_in = small + gl_ref[1:2, :]
    softplus = jnp.maximum(sp_in, 0.0) + jnp.log(1.0 + jnp.exp(-jnp.abs(sp_in)))
    g_all = gl_ref[0:1, :] * softplus
    tril_f = tri.astype(F32)
    gc_parts = []
    for c in range(nch):
        gch = g_all[c * CHUNK:(c + 1) * CHUNK, :]
        gc_parts.append(jnp.dot(tril_f, gch, preferred_element_type=F32,
                                precision=lax.Precision.HIGHEST))
    gc_all = jnp.concatenate(gc_parts, axis=0) if nch > 1 else gc_parts[0]
    pad = (-rows) % LANES
    gc_sq = jnp.concatenate([gc_all, jnp.zeros((pad, LANES), F32)], axis=0) if pad else gc_all
    gcT = gc_sq.T

    gate = gate_ref[...]
    for n, it in enumerate(items):
        c, hh = it
        r0 = c * CHUNK
        q = y[r0:r0 + CHUNK, hh * dk:(hh + 1) * dk]
        k = y[r0:r0 + CHUNK, GDN_WIDTH + hh * dk:GDN_WIDTH + (hh + 1) * dk]
        v = y[r0:r0 + CHUNK, 2 * GDN_WIDTH + hh * dk:2 * GDN_WIDTH + (hh + 1) * dk]
        q = q * lax.rsqrt(jnp.sum(q * q, axis=-1, keepdims=True) + RMS_EPS) * (dk ** -0.5)
        k = k * lax.rsqrt(jnp.sum(k * k, axis=-1, keepdims=True) + RMS_EPS)
        beta = beta_all[r0:r0 + CHUNK, BETA_LANE0 + hh:BETA_LANE0 + hh + 1]
        gcol = gc_all[r0:r0 + CHUNK, A_LANE0 + hh:A_LANE0 + hh + 1]
        grow = gcT[A_LANE0 + hh:A_LANE0 + hh + 1, r0:r0 + CHUNK]
        glast = gcT[A_LANE0 + hh:A_LANE0 + hh + 1, r0 + CHUNK - 1:r0 + CHUNK]
        eg = jnp.exp(gcol)
        kb = k * beta
        gt = gate[r0:r0 + CHUNK, hh * dk:(hh + 1) * dk]
        pb16_ref[nxt, 0, n] = q.astype(BF16)
        pb16_ref[nxt, 1, n] = kb.astype(BF16)
        pb16_ref[nxt, 2, n] = k.astype(BF16)
        pb16_ref[nxt, 3, n] = (q * eg).astype(BF16)
        pb16_ref[nxt, 4, n] = (k * jnp.exp(glast - gcol)).astype(BF16)
        pf32_ref[nxt, 0, n] = jnp.where(tri, jnp.exp(jnp.where(tri, gcol - grow, 0.0)), 0.0)
        pf32_ref[nxt, 1, n] = gt * jax.nn.sigmoid(gt)
        prhs_ref[nxt, n] = jnp.concatenate([v * beta, kb * eg], axis=1)
        pegl_ref[nxt, n] = jnp.broadcast_to(jnp.exp(glast), (SUBLANES, LANES))


def _gdn_mixer(gdn_in, small, conv_w, a_log, dt_bias, norm_g, *, batch, seq, rows=128):
    t = batch * seq
    ns = seq // rows
    gl = jnp.zeros((SUBLANES, LANES), F32)
    gl = gl.at[0, A_LANE0:A_LANE0 + GDN_HEADS].set(-jnp.exp(a_log.astype(F32)))
    gl = gl.at[1, A_LANE0:A_LANE0 + GDN_HEADS].set(dt_bias.astype(F32))
    convw = jnp.zeros((SUBLANES, 3 * GDN_WIDTH), F32).at[0:CONV_K].set(conv_w.astype(F32))
    nitems = (rows // CHUNK) * GDN_HEADS
    dk = GDN_HEAD_DIM
    rin = lambda b, i: (b * ns + jnp.minimum(i, ns - 1), 0)
    rout = lambda b, i: (b * ns + jnp.maximum(i - 1, 0), 0)
    const = lambda b, i: (0, 0)
    return pl.pallas_call(
        functools.partial(_gdn_kernel, rows=rows),
        out_shape=jax.ShapeDtypeStruct((t, GDN_WIDTH), F32),
        grid=(batch, ns + 1),
        in_specs=[
            pl.BlockSpec((rows, 3 * GDN_WIDTH), rin),
            pl.BlockSpec((rows, GDN_WIDTH), lambda b, i: (b * ns + jnp.minimum(i, ns - 1), 3)),
            pl.BlockSpec((rows, LANES), rin),
            pl.BlockSpec((SUBLANES, 3 * GDN_WIDTH), const),
            pl.BlockSpec((SUBLANES, LANES), const),
            pl.BlockSpec((1, GDN_HEAD_DIM), const),
        ],
        out_specs=pl.BlockSpec((rows, GDN_WIDTH), rout),
        scratch_shapes=[
            pltpu.VMEM((rows + SUBLANES, 3 * GDN_WIDTH), F32),
            pltpu.VMEM((GDN_HEADS, dk, dk), F32),
            pltpu.VMEM((2, 5, nitems, CHUNK, dk), BF16),
            pltpu.VMEM((2, 2, nitems, CHUNK, dk), F32),
            pltpu.VMEM((2, nitems, CHUNK, 2 * dk), F32),
            pltpu.VMEM((2, nitems, SUBLANES, LANES), F32),
        ],
        compiler_params=_cparams(("parallel", "arbitrary")),
        name="gated_deltanet",
    )(gdn_in, gdn_in, small, convw, gl, norm_g.reshape(1, -1).astype(F32))


def _cmul(ar, ai, br, bi):
    return ar * br - ai * bi, ar * bi + ai * br


def _s5_kernel(u_ref, bblk_ref, ccat_ref, apow_ref, d_ref, wglu_ref, o_ref,
               bu_ref, x_ref, carry_ref, *, tm):
    n = S5_LANES

    @pl.when(pl.program_id(1) == 0)
    def _():
        carry_ref[...] = jnp.zeros_like(carry_ref)

    u = u_ref[...]
    bu_ref[...] = _dot(u.astype(BF16), bblk_ref[...])

    def group(gidx, carry):
        c_re, c_im = carry
        r0 = pl.multiple_of(gidx * SUBLANES, SUBLANES)
        x_re = bu_ref[pl.ds(r0, SUBLANES), 0:n]
        x_im = bu_ref[pl.ds(r0, SUBLANES), n:2 * n]
        for lvl, d in enumerate((1, 2, 4)):
            a_re = apow_ref[lvl * 2 * SUBLANES:lvl * 2 * SUBLANES + SUBLANES, :]
            a_im = apow_ref[lvl * 2 * SUBLANES + SUBLANES:(lvl + 1) * 2 * SUBLANES, :]
            s_re = pltpu.roll(x_re, d, 0)
            s_im = pltpu.roll(x_im, d, 0)
            t_re, t_im = _cmul(a_re, a_im, s_re, s_im)
            x_re = x_re + t_re
            x_im = x_im + t_im
        p_re = apow_ref[6 * SUBLANES:7 * SUBLANES, :]
        p_im = apow_ref[7 * SUBLANES:8 * SUBLANES, :]
        t_re, t_im = _cmul(p_re, p_im, c_re, c_im)
        x_re = x_re + t_re
        x_im = x_im + t_im
        x_ref[pl.ds(r0, SUBLANES), 0:n] = x_re
        x_ref[pl.ds(r0, SUBLANES), n:2 * n] = x_im
        return x_re[SUBLANES - 1:SUBLANES, :], x_im[SUBLANES - 1:SUBLANES, :]

    c_re, c_im = lax.fori_loop(0, tm // SUBLANES, group,
                               (carry_ref[0:1, :], carry_ref[1:2, :]))
    carry_ref[0:1, :] = c_re
    carry_ref[1:2, :] = c_im

    yv = _dot(x_ref[...].astype(BF16), ccat_ref[...]) + d_ref[...] * u
    yv = 0.5 * yv * (1.0 + jnp.tanh(0.7978845608028654 * (yv + 0.044715 * (yv * yv * yv))))
    z = _dot(yv.astype(BF16), wglu_ref[...])
    o_ref[...] = yv * jax.nn.sigmoid(z)


def _s5_params(lam_re, lam_im, log_dt, b_re, b_im, c_re, c_im):
    f32 = F32
    lre, lim = lam_re.astype(f32), lam_im.astype(f32)
    dt = jnp.exp(log_dt.astype(f32))[:, None]
    mag = jnp.exp(lre * dt)
    ab_re, ab_im = mag * jnp.cos(lim * dt), mag * jnp.sin(lim * dt)
    num_re, num_im = ab_re - 1.0, ab_im
    den = lre * lre + lim * lim
    coef_re = (num_re * lre + num_im * lim) / den
    coef_im = (num_im * lre - num_re * lim) / den
    br, bi = b_re.astype(f32), b_im.astype(f32)
    bb_re = coef_re[..., None] * br - coef_im[..., None] * bi
    bb_im = coef_re[..., None] * bi + coef_im[..., None] * br
    eye = jnp.eye(S5_GROUPS, dtype=f32)
    blk_re = jnp.einsum('gph,gk->ghkp', bb_re, eye).reshape(S5_WIDTH, S5_LANES)
    blk_im = jnp.einsum('gph,gk->ghkp', bb_im, eye).reshape(S5_WIDTH, S5_LANES)
    bblk = jnp.concatenate([blk_re, blk_im], axis=1).astype(BF16)
    cb_re = jnp.einsum('ghp,gk->gpkh', c_re.astype(f32), eye).reshape(S5_LANES, S5_WIDTH)
    cb_im = jnp.einsum('ghp,gk->gpkh', c_im.astype(f32), eye).reshape(S5_LANES, S5_WIDTH)
    ccat = jnp.concatenate([cb_re, -cb_im], axis=0).astype(BF16)
    a1 = (ab_re.reshape(1, -1), ab_im.reshape(1, -1))
    pows = [a1]
    for _ in range(SUBLANES - 1):
        pows.append(_cmul(pows[-1][0], pows[-1][1], a1[0], a1[1]))
    rid = jnp.arange(SUBLANES)[:, None]
    rows = []
    for d in (1, 2, 4):
        mask = (rid >= d).astype(f32)
        rows.append(mask * pows[d - 1][0])
        rows.append(mask * pows[d - 1][1])
    rows.append(jnp.concatenate([pows[r][0] for r in range(SUBLANES)], axis=0))
    rows.append(jnp.concatenate([pows[r][1] for r in range(SUBLANES)], axis=0))
    apow = jnp.concatenate(rows, axis=0)
    return bblk, ccat, apow


def _s5_mixer(cu, lam_re, lam_im, log_dt, b_re, b_im, c_re, c_im, d, w_glu, *, batch, seq, tm=256):
    t = batch * seq
    ns = seq // tm
    bblk, ccat, apow = _s5_params(lam_re, lam_im, log_dt, b_re, b_im, c_re, c_im)
    row = lambda b, i: (b * ns + i, 0)
    const = lambda b, i: (0, 0)
    return pl.pallas_call(
        functools.partial(_s5_kernel, tm=tm),
        out_shape=jax.ShapeDtypeStruct((t, S5_WIDTH), F32),
        grid=(batch, ns),
        in_specs=[
            pl.BlockSpec((tm, S5_WIDTH), row),
            pl.BlockSpec(bblk.shape, const),
            pl.BlockSpec(ccat.shape, const),
            pl.BlockSpec(apow.shape, const),
            pl.BlockSpec((1, S5_WIDTH), const),
            pl.BlockSpec((S5_WIDTH, S5_WIDTH), const),
        ],
        out_specs=pl.BlockSpec((tm, S5_WIDTH), row),
        scratch_shapes=[
            pltpu.VMEM((tm, 2 * S5_LANES), F32),
            pltpu.VMEM((tm, 2 * S5_LANES), F32),
            pltpu.VMEM((SUBLANES, S5_LANES), F32),
        ],
        compiler_params=_cparams(("parallel", "arbitrary")),
        name="s5_mixer",
    )(cu, bblk, ccat, apow, d.reshape(1, -1).astype(F32), w_glu.astype(BF16))


def _route_rows(lt, n_tok):
    g = [lt[r:r + 1, :] for r in range(N_EXPERT_GROUPS)]
    gm = functools.reduce(jnp.maximum, g)
    gsum = functools.reduce(lambda a, b: a + b, [jnp.exp(x - gm) for x in g])
    g_p = 1.0 / gsum
    taken = jnp.zeros_like(gm) > 1.0
    g_hot = []
    for x in g:
        hit = jnp.logical_and(x == gm, jnp.logical_not(taken))
        g_hot.append(hit)
        taken = jnp.logical_or(taken, hit)
    e_sel = []
    for j in range(EXPERTS_PER_GROUP):
        acc = jnp.zeros_like(gm)
        for gi in range(N_EXPERT_GROUPS):
            r = 8 + gi * EXPERTS_PER_GROUP + j
            acc = acc + jnp.where(g_hot[gi], lt[r:r + 1, :], 0.0)
        e_sel.append(acc)
    m1 = functools.reduce(jnp.maximum, e_sel)
    taken = jnp.zeros_like(gm) > 1.0
    hot1 = []
    for x in e_sel:
        hit = jnp.logical_and(x == m1, jnp.logical_not(taken))
        hot1.append(hit)
        taken = jnp.logical_or(taken, hit)
    rest = [jnp.where(hh, NEG_BIG, x) for hh, x in zip(hot1, e_sel)]
    m2 = functools.reduce(jnp.maximum, rest)
    taken = jnp.zeros_like(gm) > 1.0
    hot2 = []
    for hh, x in zip(hot1, rest):
        hit = jnp.logical_and(jnp.logical_and(x == m2, jnp.logical_not(hh)), jnp.logical_not(taken))
        hot2.append(hit)
        taken = jnp.logical_or(taken, hit)
    e2 = jnp.exp(m2 - m1)
    w1 = g_p / (1.0 + e2)
    w2 = g_p * e2 / (1.0 + e2)
    rows = []
    for gi in range(N_EXPERT_GROUPS):
        for j in range(EXPERTS_PER_GROUP):
            val = jnp.where(hot1[j], w1, 0.0) + jnp.where(hot2[j], w2, 0.0)
            rows.append(jnp.where(g_hot[gi], val, 0.0))
    return jnp.concatenate(rows, axis=0)


def _outproj_kernel(h_ref, yaT_ref, yb_ref, yc_ref, wa_ref, wb_ref, wc_ref, g_ref, b_ref,
                    wrT_ref, br_ref, h1_ref, h1b_ref, comb_ref, *, tm):
    ya = yaT_ref[0].T
    mix = _dot(ya.astype(BF16), wa_ref[...])
    mix = mix + _dot(yb_ref[...].astype(BF16), wb_ref[...])
    mix = mix + _dot(yc_ref[...].astype(BF16), wc_ref[...])
    h1 = _layer_norm(ALPHA * h_ref[...] + mix, g_ref[...], b_ref[...])
    h1_ref[...] = h1
    h1b_ref[...] = h1.astype(BF16)
    h_hi, h_lo = _split_bf16(h1)
    w_hi = wrT_ref[0:LANES, :]
    w_lo = wrT_ref[LANES:2 * LANES, :]
    lt = _dot_nt(w_hi, h_hi) + _dot_nt(w_hi, h_lo) + _dot_nt(w_lo, h_hi) + br_ref[...]
    comb = _route_rows(lt, tm)
    combp = jnp.concatenate([comb, jnp.zeros((LANES - N_EXPERTS, tm), F32)], axis=0)
    comb_ref[...] = combp.T


def _out_projection(h, yaT, yb, yc, w_out, ln_g, ln_b, w_grp, b_grp, w_exp, b_exp,
                    *, batch, seq, tm=512):
    t = batch * seq
    nt = seq // tm
    wa = w_out[0:DA_WIDTH].astype(BF16)
    wb = w_out[DA_WIDTH:DA_WIDTH + GDN_WIDTH].astype(BF16)
    wc = w_out[DA_WIDTH + GDN_WIDTH:].astype(BF16)
    wr = jnp.zeros((D_MODEL, LANES), F32)
    wr = wr.at[:, 0:N_EXPERT_GROUPS].set(w_grp.astype(F32)).at[:, 8:8 + N_EXPERTS].set(w_exp.astype(F32))
    wrT = wr.T
    wr_hi = wrT.astype(BF16)
    wr_lo = (wrT - wr_hi.astype(F32)).astype(BF16)
    wr_cat = jnp.concatenate([wr_hi, wr_lo], axis=0)
    br = jnp.zeros((LANES, 1), F32)
    br = br.at[0:N_EXPERT_GROUPS, 0].set(b_grp.astype(F32)).at[8:8 + N_EXPERTS, 0].set(b_exp.astype(F32))
    row = lambda b, i: (b * nt + i, 0)
    const = lambda b, i: (0, 0)
    return pl.pallas_call(
        functools.partial(_outproj_kernel, tm=tm),
        out_shape=[
            jax.ShapeDtypeStruct((t, D_MODEL), F32),
            jax.ShapeDtypeStruct((t, D_MODEL), BF16),
            jax.ShapeDtypeStruct((t, LANES), F32),
        ],
        grid=(batch, nt),
        in_specs=[
            pl.BlockSpec((tm, D_MODEL), row),
            pl.BlockSpec((1, DA_WIDTH, tm), lambda b, i: (b, 0, i)),
            pl.BlockSpec((tm, GDN_WIDTH), row),
            pl.BlockSpec((tm, S5_WIDTH), row),
            pl.BlockSpec(wa.shape, const),
            pl.BlockSpec(wb.shape, const),
            pl.BlockSpec(wc.shape, const),
            pl.BlockSpec((1, D_MODEL), const),
            pl.BlockSpec((1, D_MODEL), const),
            pl.BlockSpec(wr_cat.shape, const),
            pl.BlockSpec((LANES, 1), const),
        ],
        out_specs=[
            pl.BlockSpec((tm, D_MODEL), row),
            pl.BlockSpec((tm, D_MODEL), row),
            pl.BlockSpec((tm, LANES), row),
        ],
        compiler_params=_cparams(("parallel", "parallel")),
        name="out_projection_router",
    )(h, yaT, yb, yc, wa, wb, wc, ln_g.reshape(1, -1), ln_b.reshape(1, -1), wr_cat, br)


MOE_EXPERTS_PER_STEP = 2


def _moe_kernel(hb_ref, h1_ref, comb_ref, w1_ref, w3_ref, w2_ref, g_ref, b_ref, o_ref, acc_ref):
    s = pl.program_id(1)
    eps = MOE_EXPERTS_PER_STEP

    @pl.when(s == 0)
    def _():
        acc_ref[...] = jnp.zeros_like(acc_ref)

    x = hb_ref[...]
    lane = lax.broadcasted_iota(jnp.int32, (1, LANES), 1)
    comb = comb_ref[...]
    ups = [(_dot(x, w1_ref[j]), _dot(x, w3_ref[j])) for j in range(eps)]
    y = None
    for j, (a, b) in enumerate(ups):
        c = jnp.sum(jnp.where(lane == s * eps + j, comb, 0.0), axis=1, keepdims=True)
        hid = (a * jax.nn.sigmoid(a) * b * c).astype(BF16)
        part = _dot(hid, w2_ref[j])
        y = part if y is None else y + part
    acc_ref[...] += y

    @pl.when(s == N_EXPERTS // eps - 1)
    def _():
        o_ref[...] = _layer_norm(ALPHA * h1_ref[...] + acc_ref[...], g_ref[...], b_ref[...])


def _moe(h1, h1b, comb, w1, w3, w2, ln_g, ln_b, *, tm=1024):
    t = h1.shape[0]
    nt = t // tm
    eps = MOE_EXPERTS_PER_STEP
    row = lambda i, e: (i, 0)
    const = lambda i, e: (0, 0)
    return pl.pallas_call(
        _moe_kernel,
        out_shape=jax.ShapeDtypeStruct((t, D_MODEL), F32),
        grid=(nt, N_EXPERTS // eps),
        in_specs=[
            pl.BlockSpec((tm, D_MODEL), row),
            pl.BlockSpec((tm, D_MODEL), row),
            pl.BlockSpec((tm, LANES), row),
            pl.BlockSpec((eps, D_MODEL, D_EXPERT), lambda i, e: (e, 0, 0)),
            pl.BlockSpec((eps, D_MODEL, D_EXPERT), lambda i, e: (e, 0, 0)),
            pl.BlockSpec((eps, D_EXPERT, D_MODEL), lambda i, e: (e, 0, 0)),
            pl.BlockSpec((1, D_MODEL), const),
            pl.BlockSpec((1, D_MODEL), const),
        ],
        out_specs=pl.BlockSpec((tm, D_MODEL), row),
        scratch_shapes=[pltpu.VMEM((tm, D_MODEL), F32)],
        compiler_params=_cparams(("parallel", "arbitrary")),
        name="moe_ffn",
    )(h1b, h1, comb, w1, w3, w2, ln_g.reshape(1, -1), ln_b.reshape(1, -1))


def _split_w_in(w):
    wt = w.T
    o = 0
    wq = wt[o:o + DA_WIDTH]; o += DA_WIDTH
    wk = wt[o:o + DA_WIDTH]; o += DA_WIDTH
    wv = wt[o:o + DA_WIDTH]; o += DA_WIDTH
    wg = wt[o:o + 4 * GDN_WIDTH]; o += 4 * GDN_WIDTH
    wbeta = wt[o:o + GDN_HEADS]; o += GDN_HEADS
    wa = wt[o:o + GDN_HEADS]; o += GDN_HEADS
    wc = wt[o:o + S5_WIDTH]
    kd = 2 * DA_HEAD_DIM
    zrows = lambda n: jnp.zeros((n, D_MODEL), w.dtype)
    wk_pad = [p for hh in range(DA_HEADS) for p in (wk[hh * kd:(hh + 1) * kd], zrows(K_PAD - kd))]
    wsm = [zrows(BETA_LANE0), wbeta, zrows(A_LANE0 - BETA_LANE0 - GDN_HEADS), wa,
           zrows(LANES - A_LANE0 - GDN_HEADS)]
    kone = jnp.zeros((1, DA_HEADS, K_PAD), F32).at[:, :, kd].set(1.0).reshape(1, DA_HEADS * K_PAD)
    w_all = jnp.concatenate([wq, wv] + wk_pad + [wg] + wsm + [wc], axis=0).astype(BF16)
    return w_all, kone


def kernel(x, ln_in_g, ln_in_b, w_in, w_out, lam_q1, lam_k1, lam_q2, lam_k2, diff_norm_g, dn_conv_w, dn_a_log, dn_dt_bias, dn_norm_g, s5_lambda_re, s5_lambda_im, s5_log_dt, s5_b_re, s5_b_im, s5_c_re, s5_c_im, s5_d, s5_w_glu, ln1_g, ln1_b, moe_w_grp, moe_b_grp, moe_w_exp, moe_b_exp, moe_w1, moe_w3, moe_w2, ln2_g, ln2_b):
    batch, seq, d = x.shape
    h = x.reshape(batch * seq, d)
    for l in range(DEPTH):
        lam_init = 0.8 - 0.6 * math.exp(-0.3 * l)
        wts = _split_w_in(w_in[l])
        outs = _in_projection(h, ln_in_g, ln_in_b, wts, batch=batch, seq=seq, apply_ln=(l == 0))
        if l == 0:
            h, qT, vT, k, gdn_in, small, cu = outs
        else:
            qT, vT, k, gdn_in, small, cu = outs
        lam = (jnp.exp(jnp.sum(lam_q1[l] * lam_k1[l])) - jnp.exp(jnp.sum(lam_q2[l] * lam_k2[l]))
               ).astype(F32) + lam_init
        yaT = _diff_attention(lam, qT, k, vT, diff_norm_g[l].astype(F32), lam_init=lam_init)
        yb = _gdn_mixer(gdn_in, small, dn_conv_w[l], dn_a_log[l], dn_dt_bias[l], dn_norm_g[l],
                        batch=batch, seq=seq)
        yc = _s5_mixer(cu, s5_lambda_re[l], s5_lambda_im[l], s5_log_dt[l], s5_b_re[l], s5_b_im[l],
                       s5_c_re[l], s5_c_im[l], s5_d[l], s5_w_glu[l], batch=batch, seq=seq)
        h1, h1b, comb = _out_projection(h, yaT, yb, yc, w_out[l], ln1_g[l], ln1_b[l],
                                        moe_w_grp[l], moe_b_grp[l], moe_w_exp[l], moe_b_exp[l],
                                        batch=batch, seq=seq)
        h = _moe(h1, h1b, comb, moe_w1[l].astype(BF16), moe_w3[l].astype(BF16),
                 moe_w2[l].astype(BF16), ln2_g[l], ln2_b[l])
    return h.reshape(batch, seq, d)
```

```python
import functools
import math

import jax
import jax.numpy as jnp
from jax import lax
from jax.experimental import pallas as pl
from jax.experimental.pallas import tpu as pltpu

F32 = jnp.float32
BF16 = jnp.bfloat16

D_MODEL = 1024
DEPTH = 2
CHUNK = 64
DA_HEADS = 6
DA_HEAD_DIM = 32
DA_V_DIM = 64
DA_WIDTH = 384
GDN_HEADS = 6
GDN_HEAD_DIM = 64
GDN_WIDTH = 384
CONV_K = 4
S5_GROUP_DIM = 16
S5_GROUPS = 16
S5_WIDTH = 256
S5_STATE = 64
S5_LANES = S5_GROUPS * S5_STATE
N_EXPERT_GROUPS = 4
EXPERTS_PER_GROUP = 4
N_EXPERTS = 16
D_EXPERT = 512
ALPHA = (2 * DEPTH) ** 0.25
LN_EPS = 1e-5
RMS_EPS = 1e-6
LOG2E = 1.4426950408889634

V7X_VMEM_LIMIT_BYTES = 56 * 1024 * 1024
SUBLANES = 8
BF16_SUBLANES = 16
LANES = 128
NEG_BIG = -1e30


class _Tiles:
    proj_tm = 512
    attn_tq = 1024
    attn_tk = 256
    gdn_rows = 128
    s5_tm = 256
    out_tm = 512
    moe_tm = 1024
    moe_experts = 2


ROUTER_EXPERT_ROW0 = 8
FAST_MAX_LOG2 = 100.0
K_PAD = LANES

BETA_LANE0 = 0
A_LANE0 = 8


def _cparams(sem):
    return pltpu.CompilerParams(dimension_semantics=sem, vmem_limit_bytes=V7X_VMEM_LIMIT_BYTES)


def _layer_norm(x, g, b):
    mu = jnp.mean(x, axis=-1, keepdims=True)
    xc = x - mu
    var = jnp.mean(xc * xc, axis=-1, keepdims=True)
    return xc * lax.rsqrt(var + LN_EPS) * g + b


def _dot(a, b):
    return jnp.dot(a, b, preferred_element_type=F32)


def _dot_nt(a, b):
    return lax.dot_general(a, b, (((1,), (1,)), ((), ())), preferred_element_type=F32)


def _dot_tn(a, b):
    return lax.dot_general(a, b, (((0,), (0,)), ((), ())), preferred_element_type=F32)


def _proj_kernel(x_ref, g_ref, b_ref, w_ref, kone_ref, *out_refs, apply_ln, q_scale):
    edges = (0, DA_WIDTH, 2 * DA_WIDTH, 2 * DA_WIDTH + DA_HEADS * K_PAD)
    edges = edges + (edges[-1] + 4 * GDN_WIDTH, edges[-1] + 4 * GDN_WIDTH + LANES,
                     edges[-1] + 4 * GDN_WIDTH + LANES + S5_WIDTH)
    wq_ref, wv_ref, wk_ref, wg_ref, wsm_ref, wc_ref = (
        w_ref.at[lo:hi, :] for lo, hi in zip(edges[:-1], edges[1:]))
    if apply_ln:
        h_ref, qT_ref, vT_ref, k_ref, gdn_ref, small_ref, cu_ref = out_refs
        h = _layer_norm(x_ref[...], g_ref[...], b_ref[...])
        h_ref[...] = h
    else:
        qT_ref, vT_ref, k_ref, gdn_ref, small_ref, cu_ref = out_refs
        h = x_ref[...]
    hb = h.astype(BF16)
    qT_ref[0] = (_dot_nt(wq_ref[...], hb) * q_scale).astype(BF16)
    vT_ref[0] = _dot_nt(wv_ref[...], hb).astype(BF16)
    k_ref[0] = (_dot_nt(hb, wk_ref[...]) + kone_ref[...]).astype(BF16)
    gdn_ref[...] = _dot_nt(hb, wg_ref[...])
    small_ref[...] = _dot_nt(hb, wsm_ref[...])
    cu_ref[...] = _dot_nt(hb, wc_ref[...])


def _in_projection(x2d, g, b, wts, *, batch, seq, apply_ln, tm=_Tiles.proj_tm):
    t = batch * seq
    nt = seq // tm
    w_all, kone = wts
    kw = DA_HEADS * K_PAD
    q_scale = (DA_HEAD_DIM ** -0.5) * LOG2E
    row = lambda bi, i: (bi * nt + i, 0)
    const = lambda bi, i: (0, 0)
    out_shape = [
        jax.ShapeDtypeStruct((batch, DA_WIDTH, seq), BF16),
        jax.ShapeDtypeStruct((batch, DA_WIDTH, seq), BF16),
        jax.ShapeDtypeStruct((batch, seq, kw), BF16),
        jax.ShapeDtypeStruct((t, 4 * GDN_WIDTH), F32),
        jax.ShapeDtypeStruct((t, LANES), F32),
        jax.ShapeDtypeStruct((t, S5_WIDTH), F32),
    ]
    out_specs = [
        pl.BlockSpec((1, DA_WIDTH, tm), lambda bi, i: (bi, 0, i)),
        pl.BlockSpec((1, DA_WIDTH, tm), lambda bi, i: (bi, 0, i)),
        pl.BlockSpec((1, tm, kw), lambda bi, i: (bi, i, 0)),
        pl.BlockSpec((tm, 4 * GDN_WIDTH), row),
        pl.BlockSpec((tm, LANES), row),
        pl.BlockSpec((tm, S5_WIDTH), row),
    ]
    if apply_ln:
        out_shape = [jax.ShapeDtypeStruct((t, D_MODEL), F32)] + out_shape
        out_specs = [pl.BlockSpec((tm, D_MODEL), row)] + out_specs
    in_specs = [
        pl.BlockSpec((tm, D_MODEL), row),
        pl.BlockSpec((1, D_MODEL), const),
        pl.BlockSpec((1, D_MODEL), const),
        pl.BlockSpec(w_all.shape, const),
        pl.BlockSpec(kone.shape, const),
    ]
    return pl.pallas_call(
        functools.partial(_proj_kernel, apply_ln=apply_ln, q_scale=q_scale),
        out_shape=out_shape,
        grid=(batch, nt),
        in_specs=in_specs,
        out_specs=out_specs,
        compiler_params=_cparams(("parallel", "parallel")),
        name="in_projection_ln" if apply_ln else "in_projection",
    )(x2d, g.reshape(1, -1), b.reshape(1, -1), w_all, kone)


def _attn_kernel(lam_ref, qT_ref, k_ref, vT_ref, g_ref, o_ref,
                 qbd_ref, pa_ref, pb_ref, m_ref, l_ref, acc_ref, cm_ref, *, tq, tk, out_scale):
    i = pl.program_id(2)
    dh = DA_HEAD_DIM
    ref_row = 2 * dh
    n_diag = 4
    assert tq == n_diag * tk

    def qk(j):
        start = pl.multiple_of(j * tk, tk)
        return _dot(k_ref[0, pl.ds(start, tk), :], qbd_ref[...])

    def pv(j, p):
        start = pl.multiple_of(j * tk, tk)
        return _dot(vT_ref[0, :, pl.ds(start, tk)], p.astype(BF16))

    def init_stats():
        m_ref[...] = jnp.full_like(m_ref, NEG_BIG)
        l_ref[...] = jnp.zeros_like(l_ref)
        acc_ref[...] = jnp.zeros_like(acc_ref)

    def exact_step(j, mask):
        s = qk(j)
        if mask is not None:
            s = jnp.where(mask, s, NEG_BIG)
        m_old = m_ref[...]
        m_new = jnp.maximum(m_old, jnp.max(s, axis=0, keepdims=True))
        alpha = jnp.exp2(m_old - m_new)
        p = jnp.exp2(s - m_new)
        l_ref[...] = alpha * l_ref[...] + jnp.sum(p, axis=0, keepdims=True)
        m_ref[...] = m_new
        acc_ref[...] = alpha * acc_ref[...] + pv(j, p)

    def score_exp(j, mask=None):
        s = qk(j)
        if mask is not None:
            s = jnp.where(mask, s, NEG_BIG)
        cm_ref[...] = jnp.maximum(cm_ref[...], jnp.max(s, axis=0, keepdims=True))
        p = jnp.exp2(s)
        l_ref[...] += jnp.sum(p, axis=0, keepdims=True)
        return p.astype(BF16)

    def accumulate(p_ref, j):
        acc_ref[...] += pv(j, p_ref[...])

    kc = lax.broadcasted_iota(jnp.int32, (tk, 1), 0) // CHUNK
    col = lax.broadcasted_iota(jnp.int32, (1, 2 * tq), 1)
    qc = jnp.where(col >= tq, col - tq, col) // CHUNK

    def diag_mask(d):
        return (kc + d * (tk // CHUNK)) <= qc

    qbd_ref[...] = jnp.zeros_like(qbd_ref)
    q = qT_ref[0]
    qbd_ref[0:dh, 0:tq] = q[0:dh]
    qbd_ref[dh:2 * dh, tq:2 * tq] = q[dh:2 * dh]
    init_stats()

    off_diag = i > 0
    t_diag = n_diag * i
    k0 = pl.multiple_of(t_diag * tk, tk)
    s0 = _dot(k_ref[0, pl.ds(k0, CHUNK), :], qbd_ref[...])
    mref = jnp.max(s0, axis=0, keepdims=True).astype(BF16)
    qbd_ref[ref_row:ref_row + BF16_SUBLANES, :] = jnp.broadcast_to(-mref, (BF16_SUBLANES, 2 * tq))
    cm_ref[...] = jnp.zeros_like(cm_ref)

    pa_ref[...] = score_exp(t_diag, diag_mask(0))
    pb_ref[...] = score_exp(t_diag + 1, diag_mask(1))
    accumulate(pa_ref, t_diag)
    pa_ref[...] = score_exp(t_diag + 2, diag_mask(2))
    accumulate(pb_ref, t_diag + 1)
    pb_ref[...] = score_exp(t_diag + 3, diag_mask(3))
    accumulate(pa_ref, t_diag + 2)
    accumulate(pb_ref, t_diag + 3)

    @pl.when(off_diag)
    def _():
        pa_ref[...] = score_exp(0)

        def quad(t, lookahead):
            pb_ref[...] = score_exp(t + 1)
            accumulate(pa_ref, t)
            pa_ref[...] = score_exp(t + 2)
            accumulate(pb_ref, t + 1)
            pb_ref[...] = score_exp(t + 3)
            accumulate(pa_ref, t + 2)
            if lookahead:
                pa_ref[...] = score_exp(t + 4)
            accumulate(pb_ref, t + 3)

        def octet(t, lookahead):
            quad(t, True)
            quad(t + 4, lookahead)

        def octet_body(r, carry):
            octet(8 * r, True)
            return carry

        odd = i % 2
        n_loop = i // 2 - 1 + odd
        lax.fori_loop(0, n_loop, octet_body, 0)
        t = 8 * n_loop

        @pl.when(odd == 1)
        def _():
            quad(t, False)

        @pl.when(odd == 0)
        def _():
            octet(t, False)

    @pl.when(jnp.max(cm_ref[...]) > FAST_MAX_LOG2)
    def _():
        qbd_ref[ref_row:ref_row + BF16_SUBLANES, :] = jnp.zeros((BF16_SUBLANES, 2 * tq), BF16)
        init_stats()

        def body(j, carry):
            exact_step(j, None)
            return carry

        lax.fori_loop(0, t_diag, body, 0)
        for d in range(n_diag):
            exact_step(t_diag + d, diag_mask(d))

    l = l_ref[...]
    acc = acc_ref[...]
    lam = lam_ref[0]
    o = acc[:, 0:tq] / l[:, 0:tq] - lam * (acc[:, tq:2 * tq] / l[:, tq:2 * tq])
    ms = jnp.mean(o * o, axis=0, keepdims=True)
    o_ref[0] = o * lax.rsqrt(ms + RMS_EPS) * g_ref[...] * out_scale


def _diff_attention(lam, qT, k, vT, norm_g, *, lam_init, tq=_Tiles.attn_tq, tk=_Tiles.attn_tk):
    batch, _, seq = qT.shape
    nq = seq // tq
    dv = DA_V_DIM
    return pl.pallas_call(
        functools.partial(_attn_kernel, tq=tq, tk=tk, out_scale=1.0 - lam_init),
        out_shape=jax.ShapeDtypeStruct((batch, DA_WIDTH, seq), F32),
        grid=(batch, DA_HEADS, nq),
        in_specs=[
            pl.BlockSpec(memory_space=pltpu.SMEM),
            pl.BlockSpec((1, dv, tq), lambda b, h, i: (b, h, i)),
            pl.BlockSpec((1, seq, LANES), lambda b, h, i: (b, 0, h)),
            pl.BlockSpec((1, dv, seq), lambda b, h, i: (b, h, 0)),
            pl.BlockSpec((dv, 1), lambda b, h, i: (0, 0)),
        ],
        out_specs=pl.BlockSpec((1, dv, tq), lambda b, h, i: (b, h, i)),
        scratch_shapes=[
            pltpu.VMEM((LANES, 2 * tq), BF16),
            pltpu.VMEM((tk, 2 * tq), BF16),
            pltpu.VMEM((tk, 2 * tq), BF16),
            pltpu.VMEM((1, 2 * tq), F32),
            pltpu.VMEM((1, 2 * tq), F32),
            pltpu.VMEM((dv, 2 * tq), F32),
            pltpu.VMEM((1, 2 * tq), F32),
        ],
        compiler_params=_cparams(("parallel", "parallel", "parallel")),
        name="diff_attention",
    )(lam.reshape(1), qT, k, vT, norm_g.reshape(dv, 1))


def _split_bf16(x):
    hi = x.astype(BF16)
    lo = (x - hi.astype(F32)).astype(BF16)
    return hi, lo


def _mm_bf16(a, b):
    return _dot(a.astype(BF16), b.astype(BF16))


def _gdn_kernel(qkv_ref, gate_ref, small_ref, convw_ref, gl_ref, ng_ref, o_ref,
                xbuf_ref, state_ref, pb16_ref, pf32_ref, prhs_ref, pegl_ref, *, rows):
    step_i = pl.program_id(1)
    dk = GDN_HEAD_DIM
    nch = rows // CHUNK
    halo = SUBLANES
    heads = range(GDN_HEADS)
    items = [(c, hh) for c in range(nch) for hh in heads]
    cur = (step_i + 1) % 2
    nxt = step_i % 2

    @pl.when(step_i == 0)
    def _():
        xbuf_ref[0:halo, :] = jnp.zeros((halo, 3 * GDN_WIDTH), F32)
        state_ref[...] = jnp.zeros_like(state_ref)
        pb16_ref[...] = jnp.zeros_like(pb16_ref)
        pf32_ref[...] = jnp.zeros_like(pf32_ref)
        prhs_ref[...] = jnp.zeros_like(prhs_ref)
        pegl_ref[...] = jnp.zeros_like(pegl_ref)

    qb, kbb, kbf, qdec, kdec, decays, gsilu, rhss, egl = {}, {}, {}, {}, {}, {}, {}, {}, {}
    for n, it in enumerate(items):
        qb[it] = pb16_ref[cur, 0, n]
        kbb[it] = pb16_ref[cur, 1, n]
        kbf[it] = pb16_ref[cur, 2, n]
        qdec[it] = pb16_ref[cur, 3, n]
        kdec[it] = pb16_ref[cur, 4, n]
        decays[it] = pf32_ref[cur, 0, n]
        gsilu[it] = pf32_ref[cur, 1, n]
        rhss[it] = prhs_ref[cur, n]
        egl[it] = pegl_ref[cur, n][0:1, 0:1]

    ri = lax.broadcasted_iota(jnp.int32, (CHUNK, CHUNK), 0)
    ci = lax.broadcasted_iota(jnp.int32, (CHUNK, CHUNK), 1)
    tri = ri >= ci
    strict = ri > ci

    def prepare_block():
        xbuf_ref[halo:halo + rows, :] = qkv_ref[...]
        y = convw_ref[CONV_K - 1:CONV_K, :] * xbuf_ref[halo:halo + rows, :]
        for j in range(CONV_K - 1):
            off = halo - (CONV_K - 1) + j
            y = y + convw_ref[j:j + 1, :] * xbuf_ref[off:off + rows, :]
        xbuf_ref[0:halo, :] = xbuf_ref[rows:rows + halo, :]
        y = y * jax.nn.sigmoid(y)

        small = small_ref[...]
        beta_all = jax.nn.sigmoid(small)
        sp_in = small + gl_ref[1:2, :]
        softplus = jnp.maximum(sp_in, 0.0) + jnp.log(1.0 + jnp.exp(-jnp.abs(sp_in)))
        g_all = gl_ref[0:1, :] * softplus
        tril_f = tri.astype(F32)
        gc_parts = []
        for c in range(nch):
            gch = g_all[c * CHUNK:(c + 1) * CHUNK, :]
            gc_parts.append(jnp.dot(tril_f, gch, preferred_element_type=F32,
                                    precision=lax.Precision.HIGHEST))
        gc_all = jnp.concatenate(gc_parts, axis=0) if nch > 1 else gc_parts[0]
        pad = (-rows) % LANES
        gc_sq = jnp.concatenate([gc_all, jnp.zeros((pad, LANES), F32)], axis=0) if pad else gc_all
        gcT = gc_sq.T

        gate = gate_ref[...]
        for n, it in enumerate(items):
            c, hh = it
            r0 = c * CHUNK
            q = y[r0:r0 + CHUNK, hh * dk:(hh + 1) * dk]
            k = y[r0:r0 + CHUNK, GDN_WIDTH + hh * dk:GDN_WIDTH + (hh + 1) * dk]
            v = y[r0:r0 + CHUNK, 2 * GDN_WIDTH + hh * dk:2 * GDN_WIDTH + (hh + 1) * dk]
            q = q * lax.rsqrt(jnp.sum(q * q, axis=-1, keepdims=True) + RMS_EPS) * (dk ** -0.5)
            k = k * lax.rsqrt(jnp.sum(k * k, axis=-1, keepdims=True) + RMS_EPS)
            beta = beta_all[r0:r0 + CHUNK, BETA_LANE0 + hh:BETA_LANE0 + hh + 1]
            gcol = gc_all[r0:r0 + CHUNK, A_LANE0 + hh:A_LANE0 + hh + 1]
            grow = gcT[A_LANE0 + hh:A_LANE0 + hh + 1, r0:r0 + CHUNK]
            glast = gcT[A_LANE0 + hh:A_LANE0 + hh + 1, r0 + CHUNK - 1:r0 + CHUNK]
            eg = jnp.exp(gcol)
            kb = k * beta
            gt = gate[r0:r0 + CHUNK, hh * dk:(hh + 1) * dk]
            pb16_ref[nxt, 0, n] = q.astype(BF16)
            pb16_ref[nxt, 1, n] = kb.astype(BF16)
            pb16_ref[nxt, 2, n] = k.astype(BF16)
            pb16_ref[nxt, 3, n] = (q * eg).astype(BF16)
            pb16_ref[nxt, 4, n] = (k * jnp.exp(glast - gcol)).astype(BF16)
            pf32_ref[nxt, 0, n] = jnp.where(tri, jnp.exp(jnp.where(tri, gcol - grow, 0.0)), 0.0)
            pf32_ref[nxt, 1, n] = gt * jax.nn.sigmoid(gt)
            prhs_ref[nxt, n] = jnp.concatenate([v * beta, kb * eg], axis=1)
            pegl_ref[nxt, n] = jnp.broadcast_to(jnp.exp(glast), (SUBLANES, LANES))

    ng = ng_ref[...]
    kk = {it: _dot_nt(kbb[it], kbf[it]) for it in items}
    qk = {it: _dot_nt(qb[it], kbf[it]) for it in items}
    lm = {it: jnp.where(strict, kk[it] * decays[it], 0.0) for it in items}
    a_intra = {it: qk[it] * decays[it] for it in items}
    xs = {it: rhss[it] - _mm_bf16(lm[it], rhss[it]) for it in items}
    ps = lm
    for _ in range(5):
        ps = {it: _mm_bf16(ps[it], ps[it]) for it in items}
        xs = {it: xs[it] + _mm_bf16(ps[it], xs[it]) for it in items}

    state = [state_ref[hh] for hh in heads]
    for c in range(nch):
        r0 = c * CHUNK
        stb = [state[hh].astype(BF16) for hh in heads]
        ws = [_dot(xs[(c, hh)][:, dk:2 * dk].astype(BF16), stb[hh]) for hh in heads]
        qst = [_dot(qdec[(c, hh)], stb[hh]) for hh in heads]
        vn = [(xs[(c, hh)][:, 0:dk] - ws[hh]).astype(BF16) for hh in heads]
        av = [_dot(a_intra[(c, hh)].astype(BF16), vn[hh]) for hh in heads]
        kv = [_dot_tn(kdec[(c, hh)], vn[hh]) for hh in heads]
        for hh in heads:
            state[hh] = state[hh] * egl[(c, hh)] + kv[hh]
            o = qst[hh] + av[hh]
            ms = jnp.mean(o * o, axis=-1, keepdims=True)
            on = o * lax.rsqrt(ms + RMS_EPS) * ng
            o_ref[r0:r0 + CHUNK, hh * dk:(hh + 1) * dk] = on * gsilu[(c, hh)]
    for hh in heads:
        state_ref[hh] = state[hh]

    prepare_block()


def _gdn_mixer(gdn_in, small, conv_w, a_log, dt_bias, norm_g, *, batch, seq, rows=_Tiles.gdn_rows):
    t = batch * seq
    ns = seq // rows
    gl = jnp.zeros((SUBLANES, LANES), F32)
    gl = gl.at[0, A_LANE0:A_LANE0 + GDN_HEADS].set(-jnp.exp(a_log.astype(F32)))
    gl = gl.at[1, A_LANE0:A_LANE0 + GDN_HEADS].set(dt_bias.astype(F32))
    convw = jnp.zeros((SUBLANES, 3 * GDN_WIDTH), F32).at[0:CONV_K].set(conv_w.astype(F32))
    nitems = (rows // CHUNK) * GDN_HEADS
    dk = GDN_HEAD_DIM
    rin = lambda b, i: (b * ns + jnp.minimum(i, ns - 1), 0)
    rout = lambda b, i: (b * ns + jnp.maximum(i - 1, 0), 0)
    const = lambda b, i: (0, 0)
    return pl.pallas_call(
        functools.partial(_gdn_kernel, rows=rows),
        out_shape=jax.ShapeDtypeStruct((t, GDN_WIDTH), F32),
        grid=(batch, ns + 1),
        in_specs=[
            pl.BlockSpec((rows, 3 * GDN_WIDTH), rin),
            pl.BlockSpec((rows, GDN_WIDTH), lambda b, i: (b * ns + jnp.minimum(i, ns - 1), 3)),
            pl.BlockSpec((rows, LANES), rin),
            pl.BlockSpec((SUBLANES, 3 * GDN_WIDTH), const),
            pl.BlockSpec((SUBLANES, LANES), const),
            pl.BlockSpec((1, GDN_HEAD_DIM), const),
        ],
        out_specs=pl.BlockSpec((rows, GDN_WIDTH), rout),
        scratch_shapes=[
            pltpu.VMEM((rows + SUBLANES, 3 * GDN_WIDTH), F32),
            pltpu.VMEM((GDN_HEADS, dk, dk), F32),
            pltpu.VMEM((2, 5, nitems, CHUNK, dk), BF16),
            pltpu.VMEM((2, 2, nitems, CHUNK, dk), F32),
            pltpu.VMEM((2, nitems, CHUNK, 2 * dk), F32),
            pltpu.VMEM((2, nitems, SUBLANES, LANES), F32),
        ],
        compiler_params=_cparams(("parallel", "arbitrary")),
        name="gated_deltanet",
    )(gdn_in, gdn_in, small, convw, gl, norm_g.reshape(1, -1).astype(F32))


def _cmul(ar, ai, br, bi):
    return ar * br - ai * bi, ar * bi + ai * br


def _s5_kernel(u_ref, bblk_ref, ccat_ref, apow_ref, d_ref, wglu_ref, o_ref,
               bu_ref, x_ref, carry_ref, *, tm):
    n = S5_LANES

    @pl.when(pl.program_id(1) == 0)
    def _():
        carry_ref[...] = jnp.zeros_like(carry_ref)

    u = u_ref[...]
    bu_ref[...] = _dot(u.astype(BF16), bblk_ref[...])

    def group(gidx, carry):
        c_re, c_im = carry
        r0 = pl.multiple_of(gidx * SUBLANES, SUBLANES)
        x_re = bu_ref[pl.ds(r0, SUBLANES), 0:n]
        x_im = bu_ref[pl.ds(r0, SUBLANES), n:2 * n]
        for lvl, d in enumerate((1, 2, 4)):
            a_re = apow_ref[lvl * 2 * SUBLANES:lvl * 2 * SUBLANES + SUBLANES, :]
            a_im = apow_ref[lvl * 2 * SUBLANES + SUBLANES:(lvl + 1) * 2 * SUBLANES, :]
            s_re = pltpu.roll(x_re, d, 0)
            s_im = pltpu.roll(x_im, d, 0)
            t_re, t_im = _cmul(a_re, a_im, s_re, s_im)
            x_re = x_re + t_re
            x_im = x_im + t_im
        p_re = apow_ref[6 * SUBLANES:7 * SUBLANES, :]
        p_im = apow_ref[7 * SUBLANES:8 * SUBLANES, :]
        t_re, t_im = _cmul(p_re, p_im, c_re, c_im)
        x_re = x_re + t_re
        x_im = x_im + t_im
        x_ref[pl.ds(r0, SUBLANES), 0:n] = x_re
        x_ref[pl.ds(r0, SUBLANES), n:2 * n] = x_im
        return x_re[SUBLANES - 1:SUBLANES, :], x_im[SUBLANES - 1:SUBLANES, :]

    c_re, c_im = lax.fori_loop(0, tm // SUBLANES, group,
                               (carry_ref[0:1, :], carry_ref[1:2, :]))
    carry_ref[0:1, :] = c_re
    carry_ref[1:2, :] = c_im

    yv = _dot(x_ref[...].astype(BF16), ccat_ref[...]) + d_ref[...] * u
    yv = 0.5 * yv * (1.0 + jnp.tanh(0.7978845608028654 * (yv + 0.044715 * (yv * yv * yv))))
    z = _dot(yv.astype(BF16), wglu_ref[...])
    o_ref[...] = yv * jax.nn.sigmoid(z)


def _s5_params(lam_re, lam_im, log_dt, b_re, b_im, c_re, c_im):
    f32 = F32
    lre, lim = lam_re.astype(f32), lam_im.astype(f32)
    dt = jnp.exp(log_dt.astype(f32))[:, None]
    mag = jnp.exp(lre * dt)
    ab_re, ab_im = mag * jnp.cos(lim * dt), mag * jnp.sin(lim * dt)
    num_re, num_im = ab_re - 1.0, ab_im
    den = lre * lre + lim * lim
    coef_re = (num_re * lre + num_im * lim) / den
    coef_im = (num_im * lre - num_re * lim) / den
    br, bi = b_re.astype(f32), b_im.astype(f32)
    bb_re = coef_re[..., None] * br - coef_im[..., None] * bi
    bb_im = coef_re[..., None] * bi + coef_im[..., None] * br
    eye = jnp.eye(S5_GROUPS, dtype=f32)
    blk_re = jnp.einsum('gph,gk->ghkp', bb_re, eye).reshape(S5_WIDTH, S5_LANES)
    blk_im = jnp.einsum('gph,gk->ghkp', bb_im, eye).reshape(S5_WIDTH, S5_LANES)
    bblk = jnp.concatenate([blk_re, blk_im], axis=1).astype(BF16)
    cb_re = jnp.einsum('ghp,gk->gpkh', c_re.astype(f32), eye).reshape(S5_LANES, S5_WIDTH)
    cb_im = jnp.einsum('ghp,gk->gpkh', c_im.astype(f32), eye).reshape(S5_LANES, S5_WIDTH)
    ccat = jnp.concatenate([cb_re, -cb_im], axis=0).astype(BF16)
    a1 = (ab_re.reshape(1, -1), ab_im.reshape(1, -1))
    pows = [a1]
    for _ in range(SUBLANES - 1):
        pows.append(_cmul(pows[-1][0], pows[-1][1], a1[0], a1[1]))
    rid = jnp.arange(SUBLANES)[:, None]
    rows = []
    for d in (1, 2, 4):
        mask = (rid >= d).astype(f32)
        rows.append(mask * pows[d - 1][0])
        rows.append(mask * pows[d - 1][1])
    rows.append(jnp.concatenate([pows[r][0] for r in range(SUBLANES)], axis=0))
    rows.append(jnp.concatenate([pows[r][1] for r in range(SUBLANES)], axis=0))
    apow = jnp.concatenate(rows, axis=0)
    return bblk, ccat, apow


def _s5_mixer(cu, lam_re, lam_im, log_dt, b_re, b_im, c_re, c_im, d, w_glu, *, batch, seq,
              tm=_Tiles.s5_tm):
    t = batch * seq
    ns = seq // tm
    bblk, ccat, apow = _s5_params(lam_re, lam_im, log_dt, b_re, b_im, c_re, c_im)
    row = lambda b, i: (b * ns + i, 0)
    const = lambda b, i: (0, 0)
    return pl.pallas_call(
        functools.partial(_s5_kernel, tm=tm),
        out_shape=jax.ShapeDtypeStruct((t, S5_WIDTH), F32),
        grid=(batch, ns),
        in_specs=[
            pl.BlockSpec((tm, S5_WIDTH), row),
            pl.BlockSpec(bblk.shape, const),
            pl.BlockSpec(ccat.shape, const),
            pl.BlockSpec(apow.shape, const),
            pl.BlockSpec((1, S5_WIDTH), const),
            pl.BlockSpec((S5_WIDTH, S5_WIDTH), const),
        ],
        out_specs=pl.BlockSpec((tm, S5_WIDTH), row),
        scratch_shapes=[
            pltpu.VMEM((tm, 2 * S5_LANES), F32),
            pltpu.VMEM((tm, 2 * S5_LANES), F32),
            pltpu.VMEM((SUBLANES, S5_LANES), F32),
        ],
        compiler_params=_cparams(("parallel", "arbitrary")),
        name="s5_mixer",
    )(cu, bblk, ccat, apow, d.reshape(1, -1).astype(F32), w_glu.astype(BF16))


def _route_rows(lt, n_tok):
    g = [lt[r:r + 1, :] for r in range(N_EXPERT_GROUPS)]
    gm = functools.reduce(jnp.maximum, g)
    gsum = functools.reduce(lambda a, b: a + b, [jnp.exp(x - gm) for x in g])
    g_p = 1.0 / gsum
    taken = jnp.zeros_like(gm) > 1.0
    g_hot = []
    for x in g:
        hit = jnp.logical_and(x == gm, jnp.logical_not(taken))
        g_hot.append(hit)
        taken = jnp.logical_or(taken, hit)
    e_sel = []
    for j in range(EXPERTS_PER_GROUP):
        acc = jnp.zeros_like(gm)
        for gi in range(N_EXPERT_GROUPS):
            r = ROUTER_EXPERT_ROW0 + gi * EXPERTS_PER_GROUP + j
            acc = acc + jnp.where(g_hot[gi], lt[r:r + 1, :], 0.0)
        e_sel.append(acc)
    m1 = functools.reduce(jnp.maximum, e_sel)
    taken = jnp.zeros_like(gm) > 1.0
    hot1 = []
    for x in e_sel:
        hit = jnp.logical_and(x == m1, jnp.logical_not(taken))
        hot1.append(hit)
        taken = jnp.logical_or(taken, hit)
    rest = [jnp.where(hh, NEG_BIG, x) for hh, x in zip(hot1, e_sel)]
    m2 = functools.reduce(jnp.maximum, rest)
    taken = jnp.zeros_like(gm) > 1.0
    hot2 = []
    for hh, x in zip(hot1, rest):
        hit = jnp.logical_and(jnp.logical_and(x == m2, jnp.logical_not(hh)), jnp.logical_not(taken))
        hot2.append(hit)
        taken = jnp.logical_or(taken, hit)
    e2 = jnp.exp(m2 - m1)
    w1 = g_p / (1.0 + e2)
    w2 = g_p * e2 / (1.0 + e2)
    rows = []
    for gi in range(N_EXPERT_GROUPS):
        for j in range(EXPERTS_PER_GROUP):
            val = jnp.where(hot1[j], w1, 0.0) + jnp.where(hot2[j], w2, 0.0)
            rows.append(jnp.where(g_hot[gi], val, 0.0))
    return jnp.concatenate(rows, axis=0)


def _outproj_kernel(h_ref, yaT_ref, yb_ref, yc_ref, wa_ref, wb_ref, wc_ref, g_ref, b_ref,
                    wrT_ref, br_ref, h1_ref, h1b_ref, comb_ref, *, tm):
    ya = yaT_ref[0].T
    mix = _dot(ya.astype(BF16), wa_ref[...])
    mix = mix + _dot(yb_ref[...].astype(BF16), wb_ref[...])
    mix = mix + _dot(yc_ref[...].astype(BF16), wc_ref[...])
    h1 = _layer_norm(ALPHA * h_ref[...] + mix, g_ref[...], b_ref[...])
    h1_ref[...] = h1
    h1b_ref[...] = h1.astype(BF16)
    h_hi, h_lo = _split_bf16(h1)
    w_hi = wrT_ref[0:LANES, :]
    w_lo = wrT_ref[LANES:2 * LANES, :]
    lt = _dot_nt(w_hi, h_hi) + _dot_nt(w_hi, h_lo) + _dot_nt(w_lo, h_hi) + br_ref[...]
    comb = _route_rows(lt, tm)
    combp = jnp.concatenate([comb, jnp.zeros((LANES - N_EXPERTS, tm), F32)], axis=0)
    comb_ref[...] = combp.T


def _out_projection(h, yaT, yb, yc, w_out, ln_g, ln_b, w_grp, b_grp, w_exp, b_exp,
                    *, batch, seq, tm=_Tiles.out_tm):
    t = batch * seq
    nt = seq // tm
    wa = w_out[0:DA_WIDTH].astype(BF16)
    wb = w_out[DA_WIDTH:DA_WIDTH + GDN_WIDTH].astype(BF16)
    wc = w_out[DA_WIDTH + GDN_WIDTH:].astype(BF16)
    wr = jnp.zeros((D_MODEL, LANES), F32)
    e0 = ROUTER_EXPERT_ROW0
    wr = wr.at[:, 0:N_EXPERT_GROUPS].set(w_grp.astype(F32)).at[:, e0:e0 + N_EXPERTS].set(w_exp.astype(F32))
    wrT = wr.T
    wr_hi = wrT.astype(BF16)
    wr_lo = (wrT - wr_hi.astype(F32)).astype(BF16)
    wr_cat = jnp.concatenate([wr_hi, wr_lo], axis=0)
    br = jnp.zeros((LANES, 1), F32)
    br = br.at[0:N_EXPERT_GROUPS, 0].set(b_grp.astype(F32)).at[e0:e0 + N_EXPERTS, 0].set(b_exp.astype(F32))
    row = lambda b, i: (b * nt + i, 0)
    const = lambda b, i: (0, 0)
    return pl.pallas_call(
        functools.partial(_outproj_kernel, tm=tm),
        out_shape=[
            jax.ShapeDtypeStruct((t, D_MODEL), F32),
            jax.ShapeDtypeStruct((t, D_MODEL), BF16),
            jax.ShapeDtypeStruct((t, LANES), F32),
        ],
        grid=(batch, nt),
        in_specs=[
            pl.BlockSpec((tm, D_MODEL), row),
            pl.BlockSpec((1, DA_WIDTH, tm), lambda b, i: (b, 0, i)),
            pl.BlockSpec((tm, GDN_WIDTH), row),
            pl.BlockSpec((tm, S5_WIDTH), row),
            pl.BlockSpec(wa.shape, const),
            pl.BlockSpec(wb.shape, const),
            pl.BlockSpec(wc.shape, const),
            pl.BlockSpec((1, D_MODEL), const),
            pl.BlockSpec((1, D_MODEL), const),
            pl.BlockSpec(wr_cat.shape, const),
            pl.BlockSpec((LANES, 1), const),
        ],
        out_specs=[
            pl.BlockSpec((tm, D_MODEL), row),
            pl.BlockSpec((tm, D_MODEL), row),
            pl.BlockSpec((tm, LANES), row),
        ],
        compiler_params=_cparams(("parallel", "parallel")),
        name="out_projection_router",
    )(h, yaT, yb, yc, wa, wb, wc, ln_g.reshape(1, -1), ln_b.reshape(1, -1), wr_cat, br)


MOE_EXPERTS_PER_STEP = _Tiles.moe_experts


def _moe_kernel(hb_ref, h1_ref, comb_ref, w1_ref, w3_ref, w2_ref, g_ref, b_ref, o_ref, acc_ref):
    s = pl.program_id(1)
    eps = MOE_EXPERTS_PER_STEP

    @pl.when(s == 0)
    def _():
        acc_ref[...] = jnp.zeros_like(acc_ref)

    x = hb_ref[...]
    lane = lax.broadcasted_iota(jnp.int32, (1, LANES), 1)
    comb = comb_ref[...]
    ups = [(_dot(x, w1_ref[j]), _dot(x, w3_ref[j])) for j in range(eps)]
    y = None
    for j, (a, b) in enumerate(ups):
        c = jnp.sum(jnp.where(lane == s * eps + j, comb, 0.0), axis=1, keepdims=True)
        hid = (a * jax.nn.sigmoid(a) * b * c).astype(BF16)
        part = _dot(hid, w2_ref[j])
        y = part if y is None else y + part
    acc_ref[...] += y

    @pl.when(s == N_EXPERTS // eps - 1)
    def _():
        o_ref[...] = _layer_norm(ALPHA * h1_ref[...] + acc_ref[...], g_ref[...], b_ref[...])


def _moe(h1, h1b, comb, w1, w3, w2, ln_g, ln_b, *, tm=_Tiles.moe_tm):
    t = h1.shape[0]
    nt = t // tm
    eps = MOE_EXPERTS_PER_STEP
    row = lambda i, e: (i, 0)
    const = lambda i, e: (0, 0)
    return pl.pallas_call(
        _moe_kernel,
        out_shape=jax.ShapeDtypeStruct((t, D_MODEL), F32),
        grid=(nt, N_EXPERTS // eps),
        in_specs=[
            pl.BlockSpec((tm, D_MODEL), row),
            pl.BlockSpec((tm, D_MODEL), row),
            pl.BlockSpec((tm, LANES), row),
            pl.BlockSpec((eps, D_MODEL, D_EXPERT), lambda i, e: (e, 0, 0)),
            pl.BlockSpec((eps, D_MODEL, D_EXPERT), lambda i, e: (e, 0, 0)),
            pl.BlockSpec((eps, D_EXPERT, D_MODEL), lambda i, e: (e, 0, 0)),
            pl.BlockSpec((1, D_MODEL), const),
            pl.BlockSpec((1, D_MODEL), const),
        ],
        out_specs=pl.BlockSpec((tm, D_MODEL), row),
        scratch_shapes=[pltpu.VMEM((tm, D_MODEL), F32)],
        compiler_params=_cparams(("parallel", "arbitrary")),
        name="moe_ffn",
    )(h1b, h1, comb, w1, w3, w2, ln_g.reshape(1, -1), ln_b.reshape(1, -1))


def _split_w_in(w):
    wt = w.T
    o = 0
    wq = wt[o:o + DA_WIDTH]; o += DA_WIDTH
    wk = wt[o:o + DA_WIDTH]; o += DA_WIDTH
    wv = wt[o:o + DA_WIDTH]; o += DA_WIDTH
    wg = wt[o:o + 4 * GDN_WIDTH]; o += 4 * GDN_WIDTH
    wbeta = wt[o:o + GDN_HEADS]; o += GDN_HEADS
    wa = wt[o:o + GDN_HEADS]; o += GDN_HEADS
    wc = wt[o:o + S5_WIDTH]
    kd = 2 * DA_HEAD_DIM
    zrows = lambda n: jnp.zeros((n, D_MODEL), w.dtype)
    wk_pad = [p for hh in range(DA_HEADS) for p in (wk[hh * kd:(hh + 1) * kd], zrows(K_PAD - kd))]
    wsm = [zrows(BETA_LANE0), wbeta, zrows(A_LANE0 - BETA_LANE0 - GDN_HEADS), wa,
           zrows(LANES - A_LANE0 - GDN_HEADS)]
    kone = jnp.zeros((1, DA_HEADS, K_PAD), F32).at[:, :, kd].set(1.0).reshape(1, DA_HEADS * K_PAD)
    w_all = jnp.concatenate([wq, wv] + wk_pad + [wg] + wsm + [wc], axis=0).astype(BF16)
    return w_all, kone


def kernel(x, ln_in_g, ln_in_b, w_in, w_out, lam_q1, lam_k1, lam_q2, lam_k2, diff_norm_g, dn_conv_w, dn_a_log, dn_dt_bias, dn_norm_g, s5_lambda_re, s5_lambda_im, s5_log_dt, s5_b_re, s5_b_im, s5_c_re, s5_c_im, s5_d, s5_w_glu, ln1_g, ln1_b, moe_w_grp, moe_b_grp, moe_w_exp, moe_b_exp, moe_w1, moe_w3, moe_w2, ln2_g, ln2_b):
    batch, seq, d = x.shape
    h = x.reshape(batch * seq, d)
    for l in range(DEPTH):
        lam_init = 0.8 - 0.6 * math.exp(-0.3 * l)
        wts = _split_w_in(w_in[l])
        outs = _in_projection(h, ln_in_g, ln_in_b, wts, batch=batch, seq=seq, apply_ln=(l == 0))
        if l == 0:
            h, qT, vT, k, gdn_in, small, cu = outs
        else:
            qT, vT, k, gdn_in, small, cu = outs
        lam = (jnp.exp(jnp.sum(lam_q1[l] * lam_k1[l])) - jnp.exp(jnp.sum(lam_q2[l] * lam_k2[l]))
               ).astype(F32) + lam_init
        yaT = _diff_attention(lam, qT, k, vT, diff_norm_g[l].astype(F32), lam_init=lam_init)
        yb = _gdn_mixer(gdn_in, small, dn_conv_w[l], dn_a_log[l], dn_dt_bias[l], dn_norm_g[l],
                        batch=batch, seq=seq)
        yc = _s5_mixer(cu, s5_lambda_re[l], s5_lambda_im[l], s5_log_dt[l], s5_b_re[l], s5_b_im[l],
                       s5_c_re[l], s5_c_im[l], s5_d[l], s5_w_glu[l], batch=batch, seq=seq)
        h1, h1b, comb = _out_projection(h, yaT, yb, yc, w_out[l], ln1_g[l], ln1_b[l],
                                        moe_w_grp[l], moe_b_grp[l], moe_w_exp[l], moe_b_exp[l],
                                        batch=batch, seq=seq)
        h = _moe(h1, h1b, comb, moe_w1[l].astype(BF16), moe_w3[l].astype(BF16),
                 moe_w2[l].astype(BF16), ln2_g[l], ln2_b[l])
    return h.reshape(batch, seq, d)
```

```python
import functools
import math

import jax
import jax.numpy as jnp
from jax import lax
from jax.experimental import pallas as pl
from jax.experimental.pallas import tpu as pltpu

F32 = jnp.float32
BF16 = jnp.bfloat16

D_MODEL = 1024
DEPTH = 2
CHUNK = 64
DA_HEADS = 6
DA_HEAD_DIM = 32
DA_V_DIM = 64
DA_WIDTH = 384
GDN_HEADS = 6
GDN_HEAD_DIM = 64
GDN_WIDTH = 384
CONV_K = 4
S5_GROUP_DIM = 16
S5_GROUPS = 16
S5_WIDTH = 256
S5_STATE = 64
S5_LANES = S5_GROUPS * S5_STATE
N_EXPERT_GROUPS = 4
EXPERTS_PER_GROUP = 4
N_EXPERTS = 16
D_EXPERT = 512
ALPHA = (2 * DEPTH) ** 0.25
LN_EPS = 1e-5
RMS_EPS = 1e-6
LOG2E = 1.4426950408889634

V7X_VMEM_LIMIT_BYTES = 56 * 1024 * 1024
SUBLANES = 8
BF16_SUBLANES = 16
LANES = 128
NEG_BIG = -1e30


class _Tiles:
    proj_tm = 512
    attn_tq = 1024
    attn_tk = 256
    gdn_rows = 128
    s5_tm = 256
    out_tm = 512
    moe_tm = 1024
    moe_experts = 2


ROUTER_EXPERT_ROW0 = 8
FAST_MAX_LOG2 = 100.0
K_PAD = LANES

BETA_LANE0 = 0
A_LANE0 = 8


def _cparams(sem):
    return pltpu.CompilerParams(dimension_semantics=sem, vmem_limit_bytes=V7X_VMEM_LIMIT_BYTES)


def _layer_norm(x, g, b):
    mu = jnp.mean(x, axis=-1, keepdims=True)
    xc = x - mu
    var = jnp.mean(xc * xc, axis=-1, keepdims=True)
    return xc * lax.rsqrt(var + LN_EPS) * g + b


def _dot(a, b):
    return jnp.dot(a, b, preferred_element_type=F32)


def _dot_nt(a, b):
    return lax.dot_general(a, b, (((1,), (1,)), ((), ())), preferred_element_type=F32)


def _dot_tn(a, b):
    return lax.dot_general(a, b, (((0,), (0,)), ((), ())), preferred_element_type=F32)


def _proj_kernel(x_ref, g_ref, b_ref, w_ref, kone_ref, *out_refs, apply_ln, q_scale):
    edges = (0, DA_WIDTH, 2 * DA_WIDTH, 2 * DA_WIDTH + DA_HEADS * K_PAD)
    edges = edges + (edges[-1] + 4 * GDN_WIDTH, edges[-1] + 4 * GDN_WIDTH + LANES,
                     edges[-1] + 4 * GDN_WIDTH + LANES + S5_WIDTH)
    wq_ref, wv_ref, wk_ref, wg_ref, wsm_ref, wc_ref = (
        w_ref.at[lo:hi, :] for lo, hi in zip(edges[:-1], edges[1:]))
    if apply_ln:
        h_ref, qT_ref, vT_ref, k_ref, gdn_ref, small_ref, cu_ref = out_refs
        h = _layer_norm(x_ref[...], g_ref[...], b_ref[...])
        h_ref[...] = h
    else:
        qT_ref, vT_ref, k_ref, gdn_ref, small_ref, cu_ref = out_refs
        h = x_ref[...]
    hb = h.astype(BF16)
    qT_ref[0] = (_dot_nt(wq_ref[...], hb) * q_scale).astype(BF16)
    vT_ref[0] = _dot_nt(wv_ref[...], hb).astype(BF16)
    k_ref[0] = (_dot_nt(hb, wk_ref[...]) + kone_ref[...]).astype(BF16)
    gdn_ref[...] = _dot_nt(hb, wg_ref[...])
    small_ref[...] = _dot_nt(hb, wsm_ref[...])
    cu_ref[...] = _dot_nt(hb, wc_ref[...])


def _in_projection(x2d, g, b, wts, *, batch, seq, apply_ln, tm=_Tiles.proj_tm):
    t = batch * seq
    nt = seq // tm
    w_all, kone = wts
    kw = DA_HEADS * K_PAD
    q_scale = (DA_HEAD_DIM ** -0.5) * LOG2E
    row = lambda bi, i: (bi * nt + i, 0)
    const = lambda bi, i: (0, 0)
    out_shape = [
        jax.ShapeDtypeStruct((batch, DA_WIDTH, seq), BF16),
        jax.ShapeDtypeStruct((batch, DA_WIDTH, seq), BF16),
        jax.ShapeDtypeStruct((batch, seq, kw), BF16),
        jax.ShapeDtypeStruct((t, 4 * GDN_WIDTH), F32),
        jax.ShapeDtypeStruct((t, LANES), F32),
        jax.ShapeDtypeStruct((t, S5_WIDTH), F32),
    ]
    out_specs = [
        pl.BlockSpec((1, DA_WIDTH, tm), lambda bi, i: (bi, 0, i)),
        pl.BlockSpec((1, DA_WIDTH, tm), lambda bi, i: (bi, 0, i)),
        pl.BlockSpec((1, tm, kw), lambda bi, i: (bi, i, 0)),
        pl.BlockSpec((tm, 4 * GDN_WIDTH), row),
        pl.BlockSpec((tm, LANES), row),
        pl.BlockSpec((tm, S5_WIDTH), row),
    ]
    if apply_ln:
        out_shape = [jax.ShapeDtypeStruct((t, D_MODEL), F32)] + out_shape
        out_specs = [pl.BlockSpec((tm, D_MODEL), row)] + out_specs
    in_specs = [
        pl.BlockSpec((tm, D_MODEL), row),
        pl.BlockSpec((1, D_MODEL), const),
        pl.BlockSpec((1, D_MODEL), const),
        pl.BlockSpec(w_all.shape, const),
        pl.BlockSpec(kone.shape, const),
    ]
    return pl.pallas_call(
        functools.partial(_proj_kernel, apply_ln=apply_ln, q_scale=q_scale),
        out_shape=out_shape,
        grid=(batch, nt),
        in_specs=in_specs,
        out_specs=out_specs,
        compiler_params=_cparams(("parallel", "parallel")),
        name="in_projection_ln" if apply_ln else "in_projection",
    )(x2d, g.reshape(1, -1), b.reshape(1, -1), w_all, kone)


def _attn_kernel(lam_ref, qT_ref, k_ref, vT_ref, g_ref, o_ref,
                 qbd_ref, pa_ref, pb_ref, m_ref, l_ref, acc_ref, cm_ref, *, tq, tk, out_scale):
    i = pl.program_id(2)
    dh = DA_HEAD_DIM
    ref_row = 2 * dh
    n_diag = 4
    assert tq == n_diag * tk

    def qk(j):
        start = pl.multiple_of(j * tk, tk)
        return _dot(k_ref[0, pl.ds(start, tk), :], qbd_ref[...])

    def pv(j, p):
        start = pl.multiple_of(j * tk, tk)
        return _dot(vT_ref[0, :, pl.ds(start, tk)], p.astype(BF16))

    def init_stats():
        m_ref[...] = jnp.full_like(m_ref, NEG_BIG)
        l_ref[...] = jnp.zeros_like(l_ref)
        acc_ref[...] = jnp.zeros_like(acc_ref)

    def exact_step(j, mask):
        s = qk(j)
        if mask is not None:
            s = jnp.where(mask, s, NEG_BIG)
        m_old = m_ref[...]
        m_new = jnp.maximum(m_old, jnp.max(s, axis=0, keepdims=True))
        alpha = jnp.exp2(m_old - m_new)
        p = jnp.exp2(s - m_new)
        l_ref[...] = alpha * l_ref[...] + jnp.sum(p, axis=0, keepdims=True)
        m_ref[...] = m_new
        acc_ref[...] = alpha * acc_ref[...] + pv(j, p)

    def score_exp(j):
        s = qk(j)
        cm_ref[...] = jnp.maximum(cm_ref[...], jnp.max(s, axis=0, keepdims=True))
        p = jnp.exp2(s)
        l_ref[...] += jnp.sum(p, axis=0, keepdims=True)
        return p.astype(BF16)

    def accumulate(p_ref, j):
        acc_ref[...] += pv(j, p_ref[...])

    kc = lax.broadcasted_iota(jnp.int32, (tk, 1), 0) // CHUNK
    col = lax.broadcasted_iota(jnp.int32, (1, 2 * tq), 1)
    qc = jnp.where(col >= tq, col - tq, col) // CHUNK

    def diag_mask(d):
        return (kc + d * (tk // CHUNK)) <= qc

    qbd_ref[...] = jnp.zeros_like(qbd_ref)
    q = qT_ref[0]
    qbd_ref[0:dh, 0:tq] = q[0:dh]
    qbd_ref[dh:2 * dh, tq:2 * tq] = q[dh:2 * dh]
    init_stats()

    off_diag = i > 0
    t_diag = n_diag * i
    k0 = pl.multiple_of(t_diag * tk, tk)
    s0 = _dot(k_ref[0, pl.ds(k0, CHUNK), :], qbd_ref[...])
    mref = jnp.max(s0, axis=0, keepdims=True).astype(BF16)
    qbd_ref[ref_row:ref_row + BF16_SUBLANES, :] = jnp.broadcast_to(-mref, (BF16_SUBLANES, 2 * tq))
    cm_ref[...] = jnp.zeros_like(cm_ref)

    def visible_columns(d):
        return ((d * tk, tq), (tq + d * tk, 2 * tq))

    def diag_score_exp(p_ref, d):
        start = pl.multiple_of((t_diag + d) * tk, tk)
        kt = k_ref[0, pl.ds(start, tk), :]
        mask = diag_mask(d)
        for lo, hi in visible_columns(d):
            s = jnp.where(mask[:, lo:hi], _dot(kt, qbd_ref[:, lo:hi]), NEG_BIG)
            cm_ref[:, lo:hi] = jnp.maximum(cm_ref[:, lo:hi], jnp.max(s, axis=0, keepdims=True))
            p = jnp.exp2(s)
            l_ref[:, lo:hi] += jnp.sum(p, axis=0, keepdims=True)
            p_ref[:, lo:hi] = p.astype(BF16)

    def diag_accumulate(p_ref, d):
        start = pl.multiple_of((t_diag + d) * tk, tk)
        vt = vT_ref[0, :, pl.ds(start, tk)]
        for lo, hi in visible_columns(d):
            acc_ref[:, lo:hi] += _dot(vt, p_ref[:, lo:hi])

    diag_score_exp(pa_ref, 0)
    diag_score_exp(pb_ref, 1)
    diag_accumulate(pa_ref, 0)
    diag_score_exp(pa_ref, 2)
    diag_accumulate(pb_ref, 1)
    diag_score_exp(pb_ref, 3)
    diag_accumulate(pa_ref, 2)
    diag_accumulate(pb_ref, 3)

    @pl.when(off_diag)
    def _():
        pa_ref[...] = score_exp(0)

        def quad(t, lookahead):
            pb_ref[...] = score_exp(t + 1)
            accumulate(pa_ref, t)
            pa_ref[...] = score_exp(t + 2)
            accumulate(pb_ref, t + 1)
            pb_ref[...] = score_exp(t + 3)
            accumulate(pa_ref, t + 2)
            if lookahead:
                pa_ref[...] = score_exp(t + 4)
            accumulate(pb_ref, t + 3)

        def octet(t, lookahead):
            quad(t, True)
            quad(t + 4, lookahead)

        def octet_body(r, carry):
            octet(8 * r, True)
            return carry

        odd = i % 2
        n_loop = i // 2 - 1 + odd
        lax.fori_loop(0, n_loop, octet_body, 0)
        t = 8 * n_loop

        @pl.when(odd == 1)
        def _():
            quad(t, False)

        @pl.when(odd == 0)
        def _():
            octet(t, False)

    @pl.when(jnp.max(cm_ref[...]) > FAST_MAX_LOG2)
    def _():
        qbd_ref[ref_row:ref_row + BF16_SUBLANES, :] = jnp.zeros((BF16_SUBLANES, 2 * tq), BF16)
        init_stats()

        def body(j, carry):
            exact_step(j, None)
            return carry

        lax.fori_loop(0, t_diag, body, 0)
        for d in range(n_diag):
            exact_step(t_diag + d, diag_mask(d))

    l = l_ref[...]
    acc = acc_ref[...]
    lam = lam_ref[0]
    o = acc[:, 0:tq] / l[:, 0:tq] - lam * (acc[:, tq:2 * tq] / l[:, tq:2 * tq])
    ms = jnp.mean(o * o, axis=0, keepdims=True)
    o_ref[0] = o * lax.rsqrt(ms + RMS_EPS) * g_ref[...] * out_scale


def _diff_attention(lam, qT, k, vT, norm_g, *, lam_init, tq=_Tiles.attn_tq, tk=_Tiles.attn_tk):
    batch, _, seq = qT.shape
    nq = seq // tq
    dv = DA_V_DIM
    return pl.pallas_call(
        functools.partial(_attn_kernel, tq=tq, tk=tk, out_scale=1.0 - lam_init),
        out_shape=jax.ShapeDtypeStruct((batch, DA_WIDTH, seq), F32),
        grid=(batch, DA_HEADS, nq),
        in_specs=[
            pl.BlockSpec(memory_space=pltpu.SMEM),
            pl.BlockSpec((1, dv, tq), lambda b, h, i: (b, h, i)),
            pl.BlockSpec((1, seq, LANES), lambda b, h, i: (b, 0, h)),
            pl.BlockSpec((1, dv, seq), lambda b, h, i: (b, h, 0)),
            pl.BlockSpec((dv, 1), lambda b, h, i: (0, 0)),
        ],
        out_specs=pl.BlockSpec((1, dv, tq), lambda b, h, i: (b, h, i)),
        scratch_shapes=[
            pltpu.VMEM((LANES, 2 * tq), BF16),
            pltpu.VMEM((tk, 2 * tq), BF16),
            pltpu.VMEM((tk, 2 * tq), BF16),
            pltpu.VMEM((1, 2 * tq), F32),
            pltpu.VMEM((1, 2 * tq), F32),
            pltpu.VMEM((dv, 2 * tq), F32),
            pltpu.VMEM((1, 2 * tq), F32),
        ],
        compiler_params=_cparams(("parallel", "parallel", "parallel")),
        name="diff_attention",
    )(lam.reshape(1), qT, k, vT, norm_g.reshape(dv, 1))


def _split_bf16(x):
    hi = x.astype(BF16)
    lo = (x - hi.astype(F32)).astype(BF16)
    return hi, lo


def _mm_bf16(a, b):
    return _dot(a.astype(BF16), b.astype(BF16))


def _gdn_kernel(qkv_ref, gate_ref, small_ref, convw_ref, gl_ref, ng_ref, o_ref,
                xbuf_ref, state_ref, pb16_ref, pf32_ref, prhs_ref, pegl_ref, *, rows):
    step_i = pl.program_id(1)
    dk = GDN_HEAD_DIM
    nch = rows // CHUNK
    halo = SUBLANES
    heads = range(GDN_HEADS)
    items = [(c, hh) for c in range(nch) for hh in heads]
    cur = (step_i + 1) % 2
    nxt = step_i % 2

    @pl.when(step_i == 0)
    def _():
        xbuf_ref[0:halo, :] = jnp.zeros((halo, 3 * GDN_WIDTH), F32)
        state_ref[...] = jnp.zeros_like(state_ref)
        pb16_ref[...] = jnp.zeros_like(pb16_ref)
        pf32_ref[...] = jnp.zeros_like(pf32_ref)
        prhs_ref[...] = jnp.zeros_like(prhs_ref)
        pegl_ref[...] = jnp.zeros_like(pegl_ref)

    qb, kbb, kbf, qdec, kdec, decays, gsilu, rhss, egl = {}, {}, {}, {}, {}, {}, {}, {}, {}
    for n, it in enumerate(items):
        qb[it] = pb16_ref[cur, 0, n]
        kbb[it] = pb16_ref[cur, 1, n]
        kbf[it] = pb16_ref[cur, 2, n]
        qdec[it] = pb16_ref[cur, 3, n]
        kdec[it] = pb16_ref[cur, 4, n]
        decays[it] = pf32_ref[cur, 0, n]
        gsilu[it] = pf32_ref[cur, 1, n]
        rhss[it] = prhs_ref[cur, n]
        egl[it] = pegl_ref[cur, n][0:1, 0:1]

    ri = lax.broadcasted_iota(jnp.int32, (CHUNK, CHUNK), 0)
    ci = lax.broadcasted_iota(jnp.int32, (CHUNK, CHUNK), 1)
    tri = ri >= ci
    strict = ri > ci

    def prepare_block():
        xbuf_ref[halo:halo + rows, :] = qkv_ref[...]
        y = convw_ref[CONV_K - 1:CONV_K, :] * xbuf_ref[halo:halo + rows, :]
        for j in range(CONV_K - 1):
            off = halo - (CONV_K - 1) + j
            y = y + convw_ref[j:j + 1, :] * xbuf_ref[off:off + rows, :]
        xbuf_ref[0:halo, :] = xbuf_ref[rows:rows + halo, :]
        y = y * jax.nn.sigmoid(y)

        small = small_ref[...]
        beta_all = jax.nn.sigmoid(small)
        sp_in = small + gl_ref[1:2, :]
        softplus = jnp.maximum(sp_in, 0.0) + jnp.log(1.0 + jnp.exp(-jnp.abs(sp_in)))
        g_all = gl_ref[0:1, :] * softplus
        tril_f = tri.astype(F32)
        gc_parts = []
        for c in range(nch):
            gch = g_all[c * CHUNK:(c + 1) * CHUNK, :]
            gc_parts.append(jnp.dot(tril_f, gch, preferred_element_type=F32,
                                    precision=lax.Precision.HIGHEST))
        gc_all = jnp.concatenate(gc_parts, axis=0) if nch > 1 else gc_parts[0]
        pad = (-rows) % LANES
        gc_sq = jnp.concatenate([gc_all, jnp.zeros((pad, LANES), F32)], axis=0) if pad else gc_all
        gcT = gc_sq.T

        gate = gate_ref[...]
        for n, it in enumerate(items):
            c, hh = it
            r0 = c * CHUNK
            q = y[r0:r0 + CHUNK, hh * dk:(hh + 1) * dk]
            k = y[r0:r0 + CHUNK, GDN_WIDTH + hh * dk:GDN_WIDTH + (hh + 1) * dk]
            v = y[r0:r0 + CHUNK, 2 * GDN_WIDTH + hh * dk:2 * GDN_WIDTH + (hh + 1) * dk]
            q = q * lax.rsqrt(jnp.sum(q * q, axis=-1, keepdims=True) + RMS_EPS) * (dk ** -0.5)
            k = k * lax.rsqrt(jnp.sum(k * k, axis=-1, keepdims=True) + RMS_EPS)
            beta = beta_all[r0:r0 + CHUNK, BETA_LANE0 + hh:BETA_LANE0 + hh + 1]
            gcol = gc_all[r0:r0 + CHUNK, A_LANE0 + hh:A_LANE0 + hh + 1]
            grow = gcT[A_LANE0 + hh:A_LANE0 + hh + 1, r0:r0 + CHUNK]
            glast = gcT[A_LANE0 + hh:A_LANE0 + hh + 1, r0 + CHUNK - 1:r0 + CHUNK]
            eg = jnp.exp(gcol)
            kb = k * beta
            gt = gate[r0:r0 + CHUNK, hh * dk:(hh + 1) * dk]
            pb16_ref[nxt, 0, n] = q.astype(BF16)
            pb16_ref[nxt, 1, n] = kb.astype(BF16)
            pb16_ref[nxt, 2, n] = k.astype(BF16)
            pb16_ref[nxt, 3, n] = (q * eg).astype(BF16)
            pb16_ref[nxt, 4, n] = (k * jnp.exp(glast - gcol)).astype(BF16)
            pf32_ref[nxt, 0, n] = jnp.where(tri, jnp.exp(jnp.where(tri, gcol - grow, 0.0)), 0.0)
            pf32_ref[nxt, 1, n] = gt * jax.nn.sigmoid(gt)
            prhs_ref[nxt, n] = jnp.concatenate([v * beta, kb * eg], axis=1)
            pegl_ref[nxt, n] = jnp.broadcast_to(jnp.exp(glast), (SUBLANES, LANES))

    ng = ng_ref[...]
    kk = {it: _dot_nt(kbb[it], kbf[it]) for it in items}
    qk = {it: _dot_nt(qb[it], kbf[it]) for it in items}
    lm = {it: jnp.where(strict, kk[it] * decays[it], 0.0) for it in items}
    a_intra = {it: qk[it] * decays[it] for it in items}
    xs = {it: rhss[it] - _mm_bf16(lm[it], rhss[it]) for it in items}
    ps = lm
    for _ in range(5):
        ps = {it: _mm_bf16(ps[it], ps[it]) for it in items}
        xs = {it: xs[it] + _mm_bf16(ps[it], xs[it]) for it in items}

    state = [state_ref[hh] for hh in heads]
    for c in range(nch):
        r0 = c * CHUNK
        stb = [state[hh].astype(BF16) for hh in heads]
        ws = [_dot(xs[(c, hh)][:, dk:2 * dk].astype(BF16), stb[hh]) for hh in heads]
        qst = [_dot(qdec[(c, hh)], stb[hh]) for hh in heads]
        vn = [(xs[(c, hh)][:, 0:dk] - ws[hh]).astype(BF16) for hh in heads]
        av = [_dot(a_intra[(c, hh)].astype(BF16), vn[hh]) for hh in heads]
        kv = [_dot_tn(kdec[(c, hh)], vn[hh]) for hh in heads]
        for hh in heads:
            state[hh] = state[hh] * egl[(c, hh)] + kv[hh]
            o = qst[hh] + av[hh]
            ms = jnp.mean(o * o, axis=-1, keepdims=True)
            on = o * lax.rsqrt(ms + RMS_EPS) * ng
            o_ref[r0:r0 + CHUNK, hh * dk:(hh + 1) * dk] = on * gsilu[(c, hh)]
    for hh in heads:
        state_ref[hh] = state[hh]

    prepare_block()


def _gdn_mixer(gdn_in, small, conv_w, a_log, dt_bias, norm_g, *, batch, seq, rows=_Tiles.gdn_rows):
    t = batch * seq
    ns = seq // rows
    gl = jnp.zeros((SUBLANES, LANES), F32)
    gl = gl.at[0, A_LANE0:A_LANE0 + GDN_HEADS].set(-jnp.exp(a_log.astype(F32)))
    gl = gl.at[1, A_LANE0:A_LANE0 + GDN_HEADS].set(dt_bias.astype(F32))
    convw = jnp.zeros((SUBLANES, 3 * GDN_WIDTH), F32).at[0:CONV_K].set(conv_w.astype(F32))
    nitems = (rows // CHUNK) * GDN_HEADS
    dk = GDN_HEAD_DIM
    rin = lambda b, i: (b * ns + jnp.minimum(i, ns - 1), 0)
    rout = lambda b, i: (b * ns + jnp.maximum(i - 1, 0), 0)
    const = lambda b, i: (0, 0)
    return pl.pallas_call(
        functools.partial(_gdn_kernel, rows=rows),
        out_shape=jax.ShapeDtypeStruct((t, GDN_WIDTH), F32),
        grid=(batch, ns + 1),
        in_specs=[
            pl.BlockSpec((rows, 3 * GDN_WIDTH), rin),
            pl.BlockSpec((rows, GDN_WIDTH), lambda b, i: (b * ns + jnp.minimum(i, ns - 1), 3)),
            pl.BlockSpec((rows, LANES), rin),
            pl.BlockSpec((SUBLANES, 3 * GDN_WIDTH), const),
            pl.BlockSpec((SUBLANES, LANES), const),
            pl.BlockSpec((1, GDN_HEAD_DIM), const),
        ],
        out_specs=pl.BlockSpec((rows, GDN_WIDTH), rout),
        scratch_shapes=[
            pltpu.VMEM((rows + SUBLANES, 3 * GDN_WIDTH), F32),
            pltpu.VMEM((GDN_HEADS, dk, dk), F32),
            pltpu.VMEM((2, 5, nitems, CHUNK, dk), BF16),
            pltpu.VMEM((2, 2, nitems, CHUNK, dk), F32),
            pltpu.VMEM((2, nitems, CHUNK, 2 * dk), F32),
            pltpu.VMEM((2, nitems, SUBLANES, LANES), F32),
        ],
        compiler_params=_cparams(("parallel", "arbitrary")),
        name="gated_deltanet",
    )(gdn_in, gdn_in, small, convw, gl, norm_g.reshape(1, -1).astype(F32))


def _cmul(ar, ai, br, bi):
    return ar * br - ai * bi, ar * bi + ai * br


def _s5_kernel(u_ref, bblk_ref, ccat_ref, apow_ref, d_ref, wglu_ref, o_ref,
               bu_ref, x_ref, carry_ref, *, tm):
    n = S5_LANES

    @pl.when(pl.program_id(1) == 0)
    def _():
        carry_ref[...] = jnp.zeros_like(carry_ref)

    u = u_ref[...]
    bu_ref[...] = _dot(u.astype(BF16), bblk_ref[...])

    def group(gidx, carry):
        c_re, c_im = carry
        r0 = pl.multiple_of(gidx * SUBLANES, SUBLANES)
        x_re = bu_ref[pl.ds(r0, SUBLANES), 0:n]
        x_im = bu_ref[pl.ds(r0, SUBLANES), n:2 * n]
        for lvl, d in enumerate((1, 2, 4)):
            a_re = apow_ref[lvl * 2 * SUBLANES:lvl * 2 * SUBLANES + SUBLANES, :]
            a_im = apow_ref[lvl * 2 * SUBLANES + SUBLANES:(lvl + 1) * 2 * SUBLANES, :]
            s_re = pltpu.roll(x_re, d, 0)
            s_im = pltpu.roll(x_im, d, 0)
            t_re, t_im = _cmul(a_re, a_im, s_re, s_im)
            x_re = x_re + t_re
            x_im = x_im + t_im
        p_re = apow_ref[6 * SUBLANES:7 * SUBLANES, :]
        p_im = apow_ref[7 * SUBLANES:8 * SUBLANES, :]
        t_re, t_im = _cmul(p_re, p_im, c_re, c_im)
        x_re = x_re + t_re
        x_im = x_im + t_im
        x_ref[pl.ds(r0, SUBLANES), 0:n] = x_re
        x_ref[pl.ds(r0, SUBLANES), n:2 * n] = x_im
        return x_re[SUBLANES - 1:SUBLANES, :], x_im[SUBLANES - 1:SUBLANES, :]

    c_re, c_im = lax.fori_loop(0, tm // SUBLANES, group,
                               (carry_ref[0:1, :], carry_ref[1:2, :]))
    carry_ref[0:1, :] = c_re
    carry_ref[1:2, :] = c_im

    yv = _dot(x_ref[...].astype(BF16), ccat_ref[...]) + d_ref[...] * u
    yv = 0.5 * yv * (1.0 + jnp.tanh(0.7978845608028654 * (yv + 0.044715 * (yv * yv * yv))))
    z = _dot(yv.astype(BF16), wglu_ref[...])
    o_ref[...] = yv * jax.nn.sigmoid(z)


def _s5_params(lam_re, lam_im, log_dt, b_re, b_im, c_re, c_im):
    f32 = F32
    lre, lim = lam_re.astype(f32), lam_im.astype(f32)
    dt = jnp.exp(log_dt.astype(f32))[:, None]
    mag = jnp.exp(lre * dt)
    ab_re, ab_im = mag * jnp.cos(lim * dt), mag * jnp.sin(lim * dt)
    num_re, num_im = ab_re - 1.0, ab_im
    den = lre * lre + lim * lim
    coef_re = (num_re * lre + num_im * lim) / den
    coef_im = (num_im * lre - num_re * lim) / den
    br, bi = b_re.astype(f32), b_im.astype(f32)
    bb_re = coef_re[..., None] * br - coef_im[..., None] * bi
    bb_im = coef_re[..., None] * bi + coef_im[..., None] * br
    eye = jnp.eye(S5_GROUPS, dtype=f32)
    blk_re = jnp.einsum('gph,gk->ghkp', bb_re, eye).reshape(S5_WIDTH, S5_LANES)
    blk_im = jnp.einsum('gph,gk->ghkp', bb_im, eye).reshape(S5_WIDTH, S5_LANES)
    bblk = jnp.concatenate([blk_re, blk_im], axis=1).astype(BF16)
    cb_re = jnp.einsum('ghp,gk->gpkh', c_re.astype(f32), eye).reshape(S5_LANES, S5_WIDTH)
    cb_im = jnp.einsum('ghp,gk->gpkh', c_im.astype(f32), eye).reshape(S5_LANES, S5_WIDTH)
    ccat = jnp.concatenate([cb_re, -cb_im], axis=0).astype(BF16)
    a1 = (ab_re.reshape(1, -1), ab_im.reshape(1, -1))
    pows = [a1]
    for _ in range(SUBLANES - 1):
        pows.append(_cmul(pows[-1][0], pows[-1][1], a1[0], a1[1]))
    rid = jnp.arange(SUBLANES)[:, None]
    rows = []
    for d in (1, 2, 4):
        mask = (rid >= d).astype(f32)
        rows.append(mask * pows[d - 1][0])
        rows.append(mask * pows[d - 1][1])
    rows.append(jnp.concatenate([pows[r][0] for r in range(SUBLANES)], axis=0))
    rows.append(jnp.concatenate([pows[r][1] for r in range(SUBLANES)], axis=0))
    apow = jnp.concatenate(rows, axis=0)
    return bblk, ccat, apow


def _s5_mixer(cu, lam_re, lam_im, log_dt, b_re, b_im, c_re, c_im, d, w_glu, *, batch, seq,
              tm=_Tiles.s5_tm):
    t = batch * seq
    ns = seq // tm
    bblk, ccat, apow = _s5_params(lam_re, lam_im, log_dt, b_re, b_im, c_re, c_im)
    row = lambda b, i: (b * ns + i, 0)
    const = lambda b, i: (0, 0)
    return pl.pallas_call(
        functools.partial(_s5_kernel, tm=tm),
        out_shape=jax.ShapeDtypeStruct((t, S5_WIDTH), F32),
        grid=(batch, ns),
        in_specs=[
            pl.BlockSpec((tm, S5_WIDTH), row),
            pl.BlockSpec(bblk.shape, const),
            pl.BlockSpec(ccat.shape, const),
            pl.BlockSpec(apow.shape, const),
            pl.BlockSpec((1, S5_WIDTH), const),
            pl.BlockSpec((S5_WIDTH, S5_WIDTH), const),
        ],
        out_specs=pl.BlockSpec((tm, S5_WIDTH), row),
        scratch_shapes=[
            pltpu.VMEM((tm, 2 * S5_LANES), F32),
            pltpu.VMEM((tm, 2 * S5_LANES), F32),
            pltpu.VMEM((SUBLANES, S5_LANES), F32),
        ],
        compiler_params=_cparams(("parallel", "arbitrary")),
        name="s5_mixer",
    )(cu, bblk, ccat, apow, d.reshape(1, -1).astype(F32), w_glu.astype(BF16))


def _route_rows(lt, n_tok):
    g = [lt[r:r + 1, :] for r in range(N_EXPERT_GROUPS)]
    gm = functools.reduce(jnp.maximum, g)
    gsum = functools.reduce(lambda a, b: a + b, [jnp.exp(x - gm) for x in g])
    g_p = 1.0 / gsum
    taken = jnp.zeros_like(gm) > 1.0
    g_hot = []
    for x in g:
        hit = jnp.logical_and(x == gm, jnp.logical_not(taken))
        g_hot.append(hit)
        taken = jnp.logical_or(taken, hit)
    e_sel = []
    for j in range(EXPERTS_PER_GROUP):
        acc = jnp.zeros_like(gm)
        for gi in range(N_EXPERT_GROUPS):
            r = ROUTER_EXPERT_ROW0 + gi * EXPERTS_PER_GROUP + j
            acc = acc + jnp.where(g_hot[gi], lt[r:r + 1, :], 0.0)
        e_sel.append(acc)
    m1 = functools.reduce(jnp.maximum, e_sel)
    taken = jnp.zeros_like(gm) > 1.0
    hot1 = []
    for x in e_sel:
        hit = jnp.logical_and(x == m1, jnp.logical_not(taken))
        hot1.append(hit)
        taken = jnp.logical_or(taken, hit)
    rest = [jnp.where(hh, NEG_BIG, x) for hh, x in zip(hot1, e_sel)]
    m2 = functools.reduce(jnp.maximum, rest)
    taken = jnp.zeros_like(gm) > 1.0
    hot2 = []
    for hh, x in zip(hot1, rest):
        hit = jnp.logical_and(jnp.logical_and(x == m2, jnp.logical_not(hh)), jnp.logical_not(taken))
        hot2.append(hit)
        taken = jnp.logical_or(taken, hit)
    e2 = jnp.exp(m2 - m1)
    w1 = g_p / (1.0 + e2)
    w2 = g_p * e2 / (1.0 + e2)
    rows = []
    for gi in range(N_EXPERT_GROUPS):
        for j in range(EXPERTS_PER_GROUP):
            val = jnp.where(hot1[j], w1, 0.0) + jnp.where(hot2[j], w2, 0.0)
            rows.append(jnp.where(g_hot[gi], val, 0.0))
    return jnp.concatenate(rows, axis=0)


def _outproj_kernel(h_ref, yaT_ref, yb_ref, yc_ref, wa_ref, wb_ref, wc_ref, g_ref, b_ref,
                    wrT_ref, br_ref, h1_ref, h1b_ref, comb_ref, *, tm):
    ya = yaT_ref[0].T
    mix = _dot(ya.astype(BF16), wa_ref[...])
    mix = mix + _dot(yb_ref[...].astype(BF16), wb_ref[...])
    mix = mix + _dot(yc_ref[...].astype(BF16), wc_ref[...])
    h1 = _layer_norm(ALPHA * h_ref[...] + mix, g_ref[...], b_ref[...])
    h1_ref[...] = h1
    h1b_ref[...] = h1.astype(BF16)
    h_hi, h_lo = _split_bf16(h1)
    w_hi = wrT_ref[0:LANES, :]
    w_lo = wrT_ref[LANES:2 * LANES, :]
    lt = _dot_nt(w_hi, h_hi) + _dot_nt(w_hi, h_lo) + _dot_nt(w_lo, h_hi) + br_ref[...]
    comb = _route_rows(lt, tm)
    combp = jnp.concatenate([comb, jnp.zeros((LANES - N_EXPERTS, tm), F32)], axis=0)
    comb_ref[...] = combp.T


def _out_projection(h, yaT, yb, yc, w_out, ln_g, ln_b, w_grp, b_grp, w_exp, b_exp,
                    *, batch, seq, tm=_Tiles.out_tm):
    t = batch * seq
    nt = seq // tm
    wa = w_out[0:DA_WIDTH].astype(BF16)
    wb = w_out[DA_WIDTH:DA_WIDTH + GDN_WIDTH].astype(BF16)
    wc = w_out[DA_WIDTH + GDN_WIDTH:].astype(BF16)
    wr = jnp.zeros((D_MODEL, LANES), F32)
    e0 = ROUTER_EXPERT_ROW0
    wr = wr.at[:, 0:N_EXPERT_GROUPS].set(w_grp.astype(F32)).at[:, e0:e0 + N_EXPERTS].set(w_exp.astype(F32))
    wrT = wr.T
    wr_hi = wrT.astype(BF16)
    wr_lo = (wrT - wr_hi.astype(F32)).astype(BF16)
    wr_cat = jnp.concatenate([wr_hi, wr_lo], axis=0)
    br = jnp.zeros((LANES, 1), F32)
    br = br.at[0:N_EXPERT_GROUPS, 0].set(b_grp.astype(F32)).at[e0:e0 + N_EXPERTS, 0].set(b_exp.astype(F32))
    row = lambda b, i: (b * nt + i, 0)
    const = lambda b, i: (0, 0)
    return pl.pallas_call(
        functools.partial(_outproj_kernel, tm=tm),
        out_shape=[
            jax.ShapeDtypeStruct((t, D_MODEL), F32),
            jax.ShapeDtypeStruct((t, D_MODEL), BF16),
            jax.ShapeDtypeStruct((t, LANES), F32),
        ],
        grid=(batch, nt),
        in_specs=[
            pl.BlockSpec((tm, D_MODEL), row),
            pl.BlockSpec((1, DA_WIDTH, tm), lambda b, i: (b, 0, i)),
            pl.BlockSpec((tm, GDN_WIDTH), row),
            pl.BlockSpec((tm, S5_WIDTH), row),
            pl.BlockSpec(wa.shape, const),
            pl.BlockSpec(wb.shape, const),
            pl.BlockSpec(wc.shape, const),
            pl.BlockSpec((1, D_MODEL), const),
            pl.BlockSpec((1, D_MODEL), const),
            pl.BlockSpec(wr_cat.shape, const),
            pl.BlockSpec((LANES, 1), const),
        ],
        out_specs=[
            pl.BlockSpec((tm, D_MODEL), row),
            pl.BlockSpec((tm, D_MODEL), row),
            pl.BlockSpec((tm, LANES), row),
        ],
        compiler_params=_cparams(("parallel", "parallel")),
        name="out_projection_router",
    )(h, yaT, yb, yc, wa, wb, wc, ln_g.reshape(1, -1), ln_b.reshape(1, -1), wr_cat, br)


MOE_EXPERTS_PER_STEP = _Tiles.moe_experts


def _moe_kernel(hb_ref, h1_ref, comb_ref, w1_ref, w3_ref, w2_ref, g_ref, b_ref, o_ref, acc_ref):
    s = pl.program_id(1)
    eps = MOE_EXPERTS_PER_STEP

    @pl.when(s == 0)
    def _():
        acc_ref[...] = jnp.zeros_like(acc_ref)

    x = hb_ref[...]
    lane = lax.broadcasted_iota(jnp.int32, (1, LANES), 1)
    comb = comb_ref[...]
    ups = [(_dot(x, w1_ref[j]), _dot(x, w3_ref[j])) for j in range(eps)]
    y = None
    for j, (a, b) in enumerate(ups):
        c = jnp.sum(jnp.where(lane == s * eps + j, comb, 0.0), axis=1, keepdims=True)
        hid = (a * jax.nn.sigmoid(a) * b * c).astype(BF16)
        part = _dot(hid, w2_ref[j])
        y = part if y is None else y + part
    acc_ref[...] += y

    @pl.when(s == N_EXPERTS // eps - 1)
    def _():
        o_ref[...] = _layer_norm(ALPHA * h1_ref[...] + acc_ref[...], g_ref[...], b_ref[...])


def _moe(h1, h1b, comb, w1, w3, w2, ln_g, ln_b, *, tm=_Tiles.moe_tm):
    t = h1.shape[0]
    nt = t // tm
    eps = MOE_EXPERTS_PER_STEP
    row = lambda i, e: (i, 0)
    const = lambda i, e: (0, 0)
    return pl.pallas_call(
        _moe_kernel,
        out_shape=jax.ShapeDtypeStruct((t, D_MODEL), F32),
        grid=(nt, N_EXPERTS // eps),
        in_specs=[
            pl.BlockSpec((tm, D_MODEL), row),
            pl.BlockSpec((tm, D_MODEL), row),
            pl.BlockSpec((tm, LANES), row),
            pl.BlockSpec((eps, D_MODEL, D_EXPERT), lambda i, e: (e, 0, 0)),
            pl.BlockSpec((eps, D_MODEL, D_EXPERT), lambda i, e: (e, 0, 0)),
            pl.BlockSpec((eps, D_EXPERT, D_MODEL), lambda i, e: (e, 0, 0)),
            pl.BlockSpec((1, D_MODEL), const),
            pl.BlockSpec((1, D_MODEL), const),
        ],
        out_specs=pl.BlockSpec((tm, D_MODEL), row),
        scratch_shapes=[pltpu.VMEM((tm, D_MODEL), F32)],
        compiler_params=_cparams(("parallel", "arbitrary")),
        name="moe_ffn",
    )(h1b, h1, comb, w1, w3, w2, ln_g.reshape(1, -1), ln_b.reshape(1, -1))


def _split_w_in(w):
    wt = w.T
    o = 0
    wq = wt[o:o + DA_WIDTH]; o += DA_WIDTH
    wk = wt[o:o + DA_WIDTH]; o += DA_WIDTH
    wv = wt[o:o + DA_WIDTH]; o += DA_WIDTH
    wg = wt[o:o + 4 * GDN_WIDTH]; o += 4 * GDN_WIDTH
    wbeta = wt[o:o + GDN_HEADS]; o += GDN_HEADS
    wa = wt[o:o + GDN_HEADS]; o += GDN_HEADS
    wc = wt[o:o + S5_WIDTH]
    kd = 2 * DA_HEAD_DIM
    zrows = lambda n: jnp.zeros((n, D_MODEL), w.dtype)
    wk_pad = [p for hh in range(DA_HEADS) for p in (wk[hh * kd:(hh + 1) * kd], zrows(K_PAD - kd))]
    wsm = [zrows(BETA_LANE0), wbeta, zrows(A_LANE0 - BETA_LANE0 - GDN_HEADS), wa,
           zrows(LANES - A_LANE0 - GDN_HEADS)]
    kone = jnp.zeros((1, DA_HEADS, K_PAD), F32).at[:, :, kd].set(1.0).reshape(1, DA_HEADS * K_PAD)
    w_all = jnp.concatenate([wq, wv] + wk_pad + [wg] + wsm + [wc], axis=0).astype(BF16)
    return w_all, kone


def kernel(x, ln_in_g, ln_in_b, w_in, w_out, lam_q1, lam_k1, lam_q2, lam_k2, diff_norm_g, dn_conv_w, dn_a_log, dn_dt_bias, dn_norm_g, s5_lambda_re, s5_lambda_im, s5_log_dt, s5_b_re, s5_b_im, s5_c_re, s5_c_im, s5_d, s5_w_glu, ln1_g, ln1_b, moe_w_grp, moe_b_grp, moe_w_exp, moe_b_exp, moe_w1, moe_w3, moe_w2, ln2_g, ln2_b):
    batch, seq, d = x.shape
    h = x.reshape(batch * seq, d)
    for l in range(DEPTH):
        lam_init = 0.8 - 0.6 * math.exp(-0.3 * l)
        wts = _split_w_in(w_in[l])
        outs = _in_projection(h, ln_in_g, ln_in_b, wts, batch=batch, seq=seq, apply_ln=(l == 0))
        if l == 0:
            h, qT, vT, k, gdn_in, small, cu = outs
        else:
            qT, vT, k, gdn_in, small, cu = outs
        lam = (jnp.exp(jnp.sum(lam_q1[l] * lam_k1[l])) - jnp.exp(jnp.sum(lam_q2[l] * lam_k2[l]))
               ).astype(F32) + lam_init
        yaT = _diff_attention(lam, qT, k, vT, diff_norm_g[l].astype(F32), lam_init=lam_init)
        yb = _gdn_mixer(gdn_in, small, dn_conv_w[l], dn_a_log[l], dn_dt_bias[l], dn_norm_g[l],
                        batch=batch, seq=seq)
        yc = _s5_mixer(cu, s5_lambda_re[l], s5_lambda_im[l], s5_log_dt[l], s5_b_re[l], s5_b_im[l],
                       s5_c_re[l], s5_c_im[l], s5_d[l], s5_w_glu[l], batch=batch, seq=seq)
        h1, h1b, comb = _out_projection(h, yaT, yb, yc, w_out[l], ln1_g[l], ln1_b[l],
                                        moe_w_grp[l], moe_b_grp[l], moe_w_exp[l], moe_b_exp[l],
                                        batch=batch, seq=seq)
        h = _moe(h1, h1b, comb, moe_w1[l].astype(BF16), moe_w3[l].astype(BF16),
                 moe_w2[l].astype(BF16), ln2_g[l], ln2_b[l])
    return h.reshape(batch, seq, d)
```

```python
import functools
import math

import jax
import jax.numpy as jnp
from jax import lax
from jax.experimental import pallas as pl
from jax.experimental.pallas import tpu as pltpu

F32 = jnp.float32
BF16 = jnp.bfloat16

D_MODEL = 1024
DEPTH = 2
CHUNK = 64
DA_HEADS = 6
DA_HEAD_DIM = 32
DA_V_DIM = 64
DA_WIDTH = 384
GDN_HEADS = 6
GDN_HEAD_DIM = 64
GDN_WIDTH = 384
CONV_K = 4
S5_GROUP_DIM = 16
S5_GROUPS = 16
S5_WIDTH = 256
S5_STATE = 64
S5_LANES = S5_GROUPS * S5_STATE
N_EXPERT_GROUPS = 4
EXPERTS_PER_GROUP = 4
N_EXPERTS = 16
D_EXPERT = 512
ALPHA = (2 * DEPTH) ** 0.25
LN_EPS = 1e-5
RMS_EPS = 1e-6
LOG2E = 1.4426950408889634

V7X_VMEM_LIMIT_BYTES = 56 * 1024 * 1024
SUBLANES = 8
BF16_SUBLANES = 16
LANES = 128
NEG_BIG = -1e30


class _Tiles:
    proj_tm = 512
    attn_tq = 1024
    attn_tk = 256
    gdn_rows = 128
    s5_tm = 256
    out_tm = 512
    moe_tm = 1024
    moe_experts = 2


ROUTER_EXPERT_ROW0 = 8
FAST_MAX_LOG2 = 100.0
K_PAD = LANES

BETA_LANE0 = 0
A_LANE0 = 8


def _cparams(sem):
    return pltpu.CompilerParams(dimension_semantics=sem, vmem_limit_bytes=V7X_VMEM_LIMIT_BYTES)


def _layer_norm(x, g, b):
    mu = jnp.mean(x, axis=-1, keepdims=True)
    xc = x - mu
    var = jnp.mean(xc * xc, axis=-1, keepdims=True)
    return xc * lax.rsqrt(var + LN_EPS) * g + b


def _dot(a, b):
    return jnp.dot(a, b, preferred_element_type=F32)


def _dot_nt(a, b):
    return lax.dot_general(a, b, (((1,), (1,)), ((), ())), preferred_element_type=F32)


def _dot_tn(a, b):
    return lax.dot_general(a, b, (((0,), (0,)), ((), ())), preferred_element_type=F32)


def _proj_kernel(x_ref, g_ref, b_ref, w_ref, kone_ref, *out_refs, apply_ln, q_scale):
    edges = (0, DA_WIDTH, 2 * DA_WIDTH, 2 * DA_WIDTH + DA_HEADS * K_PAD)
    edges = edges + (edges[-1] + 4 * GDN_WIDTH, edges[-1] + 4 * GDN_WIDTH + LANES,
                     edges[-1] + 4 * GDN_WIDTH + LANES + S5_WIDTH)
    wq_ref, wv_ref, wk_ref, wg_ref, wsm_ref, wc_ref = (
        w_ref.at[lo:hi, :] for lo, hi in zip(edges[:-1], edges[1:]))
    if apply_ln:
        h_ref, qT_ref, vT_ref, k_ref, gdn_ref, small_ref, cu_ref = out_refs
        h = _layer_norm(x_ref[...], g_ref[...], b_ref[...])
        h_ref[...] = h
    else:
        qT_ref, vT_ref, k_ref, gdn_ref, small_ref, cu_ref = out_refs
        h = x_ref[...]
    hb = h.astype(BF16)
    qT_ref[0] = (_dot_nt(wq_ref[...], hb) * q_scale).astype(BF16)
    vT_ref[0] = _dot_nt(wv_ref[...], hb).astype(BF16)
    k_ref[0] = (_dot_nt(hb, wk_ref[...]) + kone_ref[...]).astype(BF16)
    gdn_ref[...] = _dot_nt(hb, wg_ref[...])
    small_ref[...] = _dot_nt(hb, wsm_ref[...])
    cu_ref[...] = _dot_nt(hb, wc_ref[...])


def _in_projection(x2d, g, b, wts, *, batch, seq, apply_ln, tm=_Tiles.proj_tm):
    t = batch * seq
    nt = seq // tm
    w_all, kone = wts
    kw = DA_HEADS * K_PAD
    q_scale = (DA_HEAD_DIM ** -0.5) * LOG2E
    row = lambda bi, i: (bi * nt + i, 0)
    const = lambda bi, i: (0, 0)
    out_shape = [
        jax.ShapeDtypeStruct((batch, DA_WIDTH, seq), BF16),
        jax.ShapeDtypeStruct((batch, DA_WIDTH, seq), BF16),
        jax.ShapeDtypeStruct((batch, seq, kw), BF16),
        jax.ShapeDtypeStruct((t, 4 * GDN_WIDTH), F32),
        jax.ShapeDtypeStruct((t, LANES), F32),
        jax.ShapeDtypeStruct((t, S5_WIDTH), F32),
    ]
    out_specs = [
        pl.BlockSpec((1, DA_WIDTH, tm), lambda bi, i: (bi, 0, i)),
        pl.BlockSpec((1, DA_WIDTH, tm), lambda bi, i: (bi, 0, i)),
        pl.BlockSpec((1, tm, kw), lambda bi, i: (bi, i, 0)),
        pl.BlockSpec((tm, 4 * GDN_WIDTH), row),
        pl.BlockSpec((tm, LANES), row),
        pl.BlockSpec((tm, S5_WIDTH), row),
    ]
    if apply_ln:
        out_shape = [jax.ShapeDtypeStruct((t, D_MODEL), F32)] + out_shape
        out_specs = [pl.BlockSpec((tm, D_MODEL), row)] + out_specs
    in_specs = [
        pl.BlockSpec((tm, D_MODEL), row),
        pl.BlockSpec((1, D_MODEL), const),
        pl.BlockSpec((1, D_MODEL), const),
        pl.BlockSpec(w_all.shape, const),
        pl.BlockSpec(kone.shape, const),
    ]
    return pl.pallas_call(
        functools.partial(_proj_kernel, apply_ln=apply_ln, q_scale=q_scale),
        out_shape=out_shape,
        grid=(batch, nt),
        in_specs=in_specs,
        out_specs=out_specs,
        compiler_params=_cparams(("parallel", "parallel")),
        name="in_projection_ln" if apply_ln else "in_projection",
    )(x2d, g.reshape(1, -1), b.reshape(1, -1), w_all, kone)


def _attn_kernel(lam_ref, qT_ref, k_ref, vT_ref, g_ref, o_ref,
                 qbd_ref, pa_ref, pb_ref, m_ref, l_ref, acc_ref, cm_ref, *, tq, tk, out_scale):
    i = pl.program_id(2)
    dh = DA_HEAD_DIM
    ref_row = 2 * dh
    n_diag = 4
    assert tq == n_diag * tk

    def qk(j):
        start = pl.multiple_of(j * tk, tk)
        return _dot(k_ref[0, pl.ds(start, tk), :], qbd_ref[...])

    def pv(j, p):
        start = pl.multiple_of(j * tk, tk)
        return _dot(vT_ref[0, :, pl.ds(start, tk)], p.astype(BF16))

    def init_stats():
        m_ref[...] = jnp.full_like(m_ref, NEG_BIG)
        l_ref[...] = jnp.zeros_like(l_ref)
        acc_ref[...] = jnp.zeros_like(acc_ref)

    def exact_step(j, mask):
        s = qk(j)
        if mask is not None:
            s = jnp.where(mask, s, NEG_BIG)
        m_old = m_ref[...]
        m_new = jnp.maximum(m_old, jnp.max(s, axis=0, keepdims=True))
        alpha = jnp.exp2(m_old - m_new)
        p = jnp.exp2(s - m_new)
        l_ref[...] = alpha * l_ref[...] + jnp.sum(p, axis=0, keepdims=True)
        m_ref[...] = m_new
        acc_ref[...] = alpha * acc_ref[...] + pv(j, p)

    def score_exp(j):
        s = qk(j)
        cm_ref[...] = jnp.maximum(cm_ref[...], jnp.max(s, axis=0, keepdims=True))
        p = jnp.exp2(s)
        l_ref[...] += jnp.sum(p, axis=0, keepdims=True)
        return p.astype(BF16)

    def accumulate(p_ref, j):
        acc_ref[...] += pv(j, p_ref[...])

    kc = lax.broadcasted_iota(jnp.int32, (tk, 1), 0) // CHUNK
    col = lax.broadcasted_iota(jnp.int32, (1, 2 * tq), 1)
    qc = jnp.where(col >= tq, col - tq, col) // CHUNK

    def diag_mask(d):
        return (kc + d * (tk // CHUNK)) <= qc

    qbd_ref[...] = jnp.zeros_like(qbd_ref)
    q = qT_ref[0]
    qbd_ref[0:dh, 0:tq] = q[0:dh]
    qbd_ref[dh:2 * dh, tq:2 * tq] = q[dh:2 * dh]
    init_stats()

    off_diag = i > 0
    t_diag = n_diag * i
    k0 = pl.multiple_of(t_diag * tk, tk)
    s0 = _dot(k_ref[0, pl.ds(k0, CHUNK), :], qbd_ref[...])
    mref = jnp.max(s0, axis=0, keepdims=True).astype(BF16)
    qbd_ref[ref_row:ref_row + BF16_SUBLANES, :] = jnp.broadcast_to(-mref, (BF16_SUBLANES, 2 * tq))
    cm_ref[...] = jnp.zeros_like(cm_ref)

    def visible_columns(d):
        return ((d * tk, tq), (tq + d * tk, 2 * tq))

    def diag_score_exp(p_ref, d):
        start = pl.multiple_of((t_diag + d) * tk, tk)
        kt = k_ref[0, pl.ds(start, tk), :]
        mask = diag_mask(d)
        for lo, hi in visible_columns(d):
            s = jnp.where(mask[:, lo:hi], _dot(kt, qbd_ref[:, lo:hi]), NEG_BIG)
            cm_ref[:, lo:hi] = jnp.maximum(cm_ref[:, lo:hi], jnp.max(s, axis=0, keepdims=True))
            p = jnp.exp2(s)
            l_ref[:, lo:hi] += jnp.sum(p, axis=0, keepdims=True)
            p_ref[:, lo:hi] = p.astype(BF16)

    def diag_accumulate(p_ref, d):
        start = pl.multiple_of((t_diag + d) * tk, tk)
        vt = vT_ref[0, :, pl.ds(start, tk)]
        for lo, hi in visible_columns(d):
            acc_ref[:, lo:hi] += _dot(vt, p_ref[:, lo:hi])

    diag_score_exp(pa_ref, 0)
    diag_score_exp(pb_ref, 1)
    diag_accumulate(pa_ref, 0)
    diag_score_exp(pa_ref, 2)
    diag_accumulate(pb_ref, 1)
    diag_score_exp(pb_ref, 3)
    diag_accumulate(pa_ref, 2)
    diag_accumulate(pb_ref, 3)

    @pl.when(off_diag)
    def _():
        pa_ref[...] = score_exp(0)

        def quad(t, lookahead):
            pb_ref[...] = score_exp(t + 1)
            accumulate(pa_ref, t)
            pa_ref[...] = score_exp(t + 2)
            accumulate(pb_ref, t + 1)
            pb_ref[...] = score_exp(t + 3)
            accumulate(pa_ref, t + 2)
            if lookahead:
                pa_ref[...] = score_exp(t + 4)
            accumulate(pb_ref, t + 3)

        def octet(t, lookahead):
            quad(t, True)
            quad(t + 4, lookahead)

        def octet_body(r, carry):
            octet(8 * r, True)
            return carry

        odd = i % 2
        n_loop = i // 2 - 1 + odd
        lax.fori_loop(0, n_loop, octet_body, 0)
        t = 8 * n_loop

        @pl.when(odd == 1)
        def _():
            quad(t, False)

        @pl.when(odd == 0)
        def _():
            octet(t, False)

    @pl.when(jnp.max(cm_ref[...]) > FAST_MAX_LOG2)
    def _():
        qbd_ref[ref_row:ref_row + BF16_SUBLANES, :] = jnp.zeros((BF16_SUBLANES, 2 * tq), BF16)
        init_stats()

        def body(j, carry):
            exact_step(j, None)
            return carry

        lax.fori_loop(0, t_diag, body, 0)
        for d in range(n_diag):
            exact_step(t_diag + d, diag_mask(d))

    l = l_ref[...]
    acc = acc_ref[...]
    lam = lam_ref[0]
    o = acc[:, 0:tq] / l[:, 0:tq] - lam * (acc[:, tq:2 * tq] / l[:, tq:2 * tq])
    ms = jnp.mean(o * o, axis=0, keepdims=True)
    o_ref[0] = o * lax.rsqrt(ms + RMS_EPS) * g_ref[...] * out_scale


def _diff_attention(lam, qT, k, vT, norm_g, *, lam_init, tq=_Tiles.attn_tq, tk=_Tiles.attn_tk):
    batch, _, seq = qT.shape
    nq = seq // tq
    dv = DA_V_DIM
    return pl.pallas_call(
        functools.partial(_attn_kernel, tq=tq, tk=tk, out_scale=1.0 - lam_init),
        out_shape=jax.ShapeDtypeStruct((batch, DA_WIDTH, seq), F32),
        grid=(batch, DA_HEADS, nq),
        in_specs=[
            pl.BlockSpec(memory_space=pltpu.SMEM),
            pl.BlockSpec((1, dv, tq), lambda b, h, i: (b, h, i)),
            pl.BlockSpec((1, seq, LANES), lambda b, h, i: (b, 0, h)),
            pl.BlockSpec((1, dv, seq), lambda b, h, i: (b, h, 0)),
            pl.BlockSpec((dv, 1), lambda b, h, i: (0, 0)),
        ],
        out_specs=pl.BlockSpec((1, dv, tq), lambda b, h, i: (b, h, i)),
        scratch_shapes=[
            pltpu.VMEM((LANES, 2 * tq), BF16),
            pltpu.VMEM((tk, 2 * tq), BF16),
            pltpu.VMEM((tk, 2 * tq), BF16),
            pltpu.VMEM((1, 2 * tq), F32),
            pltpu.VMEM((1, 2 * tq), F32),
            pltpu.VMEM((dv, 2 * tq), F32),
            pltpu.VMEM((1, 2 * tq), F32),
        ],
        compiler_params=_cparams(("parallel", "parallel", "parallel")),
        name="diff_attention",
    )(lam.reshape(1), qT, k, vT, norm_g.reshape(dv, 1))


def _split_bf16(x):
    hi = x.astype(BF16)
    lo = (x - hi.astype(F32)).astype(BF16)
    return hi, lo


def _mm_bf16(a, b):
    return _dot(a.astype(BF16), b.astype(BF16))


def _gdn_kernel(qkv_ref, gate_ref, small_ref, convw_ref, gl_ref, ng_ref, o_ref,
                xbuf_ref, state_ref, pb16_ref, pf32_ref, prhs_ref, pegl_ref, *, rows):
    step_i = pl.program_id(1)
    dk = GDN_HEAD_DIM
    nch = rows // CHUNK
    halo = SUBLANES
    heads = range(GDN_HEADS)
    items = [(c, hh) for c in range(nch) for hh in heads]
    cur = (step_i + 1) % 2
    nxt = step_i % 2

    @pl.when(step_i == 0)
    def _():
        xbuf_ref[0:halo, :] = jnp.zeros((halo, 3 * GDN_WIDTH), F32)
        state_ref[...] = jnp.zeros_like(state_ref)
        pb16_ref[...] = jnp.zeros_like(pb16_ref)
        pf32_ref[...] = jnp.zeros_like(pf32_ref)
        prhs_ref[...] = jnp.zeros_like(prhs_ref)
        pegl_ref[...] = jnp.zeros_like(pegl_ref)

    qb, kbb, kbf, qdec, kdec, decays, gsilu, rhss, egl = {}, {}, {}, {}, {}, {}, {}, {}, {}
    for n, it in enumerate(items):
        qb[it] = pb16_ref[cur, 0, n]
        kbb[it] = pb16_ref[cur, 1, n]
        kbf[it] = pb16_ref[cur, 2, n]
        qdec[it] = pb16_ref[cur, 3, n]
        kdec[it] = pb16_ref[cur, 4, n]
        decays[it] = pf32_ref[cur, 0, n]
        gsilu[it] = pf32_ref[cur, 1, n]
        rhss[it] = prhs_ref[cur, n]
        egl[it] = pegl_ref[cur, n][0:1, 0:1]

    ri = lax.broadcasted_iota(jnp.int32, (CHUNK, CHUNK), 0)
    ci = lax.broadcasted_iota(jnp.int32, (CHUNK, CHUNK), 1)
    tri = ri >= ci
    strict = ri > ci

    def prepare_block():
        xbuf_ref[halo:halo + rows, :] = qkv_ref[...]
        y = convw_ref[CONV_K - 1:CONV_K, :] * xbuf_ref[halo:halo + rows, :]
        for j in range(CONV_K - 1):
            off = halo - (CONV_K - 1) + j
            y = y + convw_ref[j:j + 1, :] * xbuf_ref[off:off + rows, :]
        xbuf_ref[0:halo, :] = xbuf_ref[rows:rows + halo, :]
        y = y * jax.nn.sigmoid(y)

        small = small_ref[...]
        beta_all = jax.nn.sigmoid(small)
        sp_in = small + gl_ref[1:2, :]
        softplus = jnp.maximum(sp_in, 0.0) + jnp.log(1.0 + jnp.exp(-jnp.abs(sp_in)))
        g_all = gl_ref[0:1, :] * softplus
        tril_f = tri.astype(F32)
        gc_parts = []
        for c in range(nch):
            gch = g_all[c * CHUNK:(c + 1) * CHUNK, :]
            gc_parts.append(jnp.dot(tril_f, gch, preferred_element_type=F32,
                                    precision=lax.Precision.HIGHEST))
        gc_all = jnp.concatenate(gc_parts, axis=0) if nch > 1 else gc_parts[0]
        pad = (-rows) % LANES
        gc_sq = jnp.concatenate([gc_all, jnp.zeros((pad, LANES), F32)], axis=0) if pad else gc_all
        gcT = gc_sq.T

        gate = gate_ref[...]
        for n, it in enumerate(items):
            c, hh = it
            r0 = c * CHUNK
            q = y[r0:r0 + CHUNK, hh * dk:(hh + 1) * dk]
            k = y[r0:r0 + CHUNK, GDN_WIDTH + hh * dk:GDN_WIDTH + (hh + 1) * dk]
            v = y[r0:r0 + CHUNK, 2 * GDN_WIDTH + hh * dk:2 * GDN_WIDTH + (hh + 1) * dk]
            q = q * lax.rsqrt(jnp.sum(q * q, axis=-1, keepdims=True) + RMS_EPS) * (dk ** -0.5)
            k = k * lax.rsqrt(jnp.sum(k * k, axis=-1, keepdims=True) + RMS_EPS)
            beta = beta_all[r0:r0 + CHUNK, BETA_LANE0 + hh:BETA_LANE0 + hh + 1]
            gcol = gc_all[r0:r0 + CHUNK, A_LANE0 + hh:A_LANE0 + hh + 1]
            grow = gcT[A_LANE0 + hh:A_LANE0 + hh + 1, r0:r0 + CHUNK]
            glast = gcT[A_LANE0 + hh:A_LANE0 + hh + 1, r0 + CHUNK - 1:r0 + CHUNK]
            eg = jnp.exp(gcol)
            kb = k * beta
            gt = gate[r0:r0 + CHUNK, hh * dk:(hh + 1) * dk]
            pb16_ref[nxt, 0, n] = q.astype(BF16)
            pb16_ref[nxt, 1, n] = kb.astype(BF16)
            pb16_ref[nxt, 2, n] = k.astype(BF16)
            pb16_ref[nxt, 3, n] = (q * eg).astype(BF16)
            pb16_ref[nxt, 4, n] = (k * jnp.exp(glast - gcol)).astype(BF16)
            pf32_ref[nxt, 0, n] = jnp.where(tri, jnp.exp(jnp.where(tri, gcol - grow, 0.0)), 0.0)
            pf32_ref[nxt, 1, n] = gt * jax.nn.sigmoid(gt)
            prhs_ref[nxt, n] = jnp.concatenate([v * beta, kb * eg], axis=1)
            pegl_ref[nxt, n] = jnp.broadcast_to(jnp.exp(glast), (SUBLANES, LANES))

    ng = ng_ref[...]
    kk = {it: _dot_nt(kbb[it], kbf[it]) for it in items}
    qk = {it: _dot_nt(qb[it], kbf[it]) for it in items}
    lm = {it: jnp.where(strict, kk[it] * decays[it], 0.0) for it in items}
    a_intra = {it: qk[it] * decays[it] for it in items}
    xs = {it: rhss[it] - _mm_bf16(lm[it], rhss[it]) for it in items}
    ps = lm
    for _ in range(5):
        ps = {it: _mm_bf16(ps[it], ps[it]) for it in items}
        xs = {it: xs[it] + _mm_bf16(ps[it], xs[it]) for it in items}

    state = [state_ref[hh] for hh in heads]
    for c in range(nch):
        r0 = c * CHUNK
        stb = [state[hh].astype(BF16) for hh in heads]
        ws = [_dot(xs[(c, hh)][:, dk:2 * dk].astype(BF16), stb[hh]) for hh in heads]
        qst = [_dot(qdec[(c, hh)], stb[hh]) for hh in heads]
        vn = [(xs[(c, hh)][:, 0:dk] - ws[hh]).astype(BF16) for hh in heads]
        av = [_dot(a_intra[(c, hh)].astype(BF16), vn[hh]) for hh in heads]
        kv = [_dot_tn(kdec[(c, hh)], vn[hh]) for hh in heads]
        for hh in heads:
            state[hh] = state[hh] * egl[(c, hh)] + kv[hh]
            o = qst[hh] + av[hh]
            ms = jnp.mean(o * o, axis=-1, keepdims=True)
            on = o * lax.rsqrt(ms + RMS_EPS) * ng
            o_ref[r0:r0 + CHUNK, hh * dk:(hh + 1) * dk] = on * gsilu[(c, hh)]
    for hh in heads:
        state_ref[hh] = state[hh]

    prepare_block()


def _gdn_mixer(gdn_in, small, conv_w, a_log, dt_bias, norm_g, *, batch, seq, rows=_Tiles.gdn_rows):
    t = batch * seq
    ns = seq // rows
    gl = jnp.zeros((SUBLANES, LANES), F32)
    gl = gl.at[0, A_LANE0:A_LANE0 + GDN_HEADS].set(-jnp.exp(a_log.astype(F32)))
    gl = gl.at[1, A_LANE0:A_LANE0 + GDN_HEADS].set(dt_bias.astype(F32))
    convw = jnp.zeros((SUBLANES, 3 * GDN_WIDTH), F32).at[0:CONV_K].set(conv_w.astype(F32))
    nitems = (rows // CHUNK) * GDN_HEADS
    dk = GDN_HEAD_DIM
    rin = lambda b, i: (b * ns + jnp.minimum(i, ns - 1), 0)
    rout = lambda b, i: (b * ns + jnp.maximum(i - 1, 0), 0)
    const = lambda b, i: (0, 0)
    return pl.pallas_call(
        functools.partial(_gdn_kernel, rows=rows),
        out_shape=jax.ShapeDtypeStruct((t, GDN_WIDTH), F32),
        grid=(batch, ns + 1),
        in_specs=[
            pl.BlockSpec((rows, 3 * GDN_WIDTH), rin),
            pl.BlockSpec((rows, GDN_WIDTH), lambda b, i: (b * ns + jnp.minimum(i, ns - 1), 3)),
            pl.BlockSpec((rows, LANES), rin),
            pl.BlockSpec((SUBLANES, 3 * GDN_WIDTH), const),
            pl.BlockSpec((SUBLANES, LANES), const),
            pl.BlockSpec((1, GDN_HEAD_DIM), const),
        ],
        out_specs=pl.BlockSpec((rows, GDN_WIDTH), rout),
        scratch_shapes=[
            pltpu.VMEM((rows + SUBLANES, 3 * GDN_WIDTH), F32),
            pltpu.VMEM((GDN_HEADS, dk, dk), F32),
            pltpu.VMEM((2, 5, nitems, CHUNK, dk), BF16),
            pltpu.VMEM((2, 2, nitems, CHUNK, dk), F32),
            pltpu.VMEM((2, nitems, CHUNK, 2 * dk), F32),
            pltpu.VMEM((2, nitems, SUBLANES, LANES), F32),
        ],
        compiler_params=_cparams(("parallel", "arbitrary")),
        name="gated_deltanet",
    )(gdn_in, gdn_in, small, convw, gl, norm_g.reshape(1, -1).astype(F32))


def _cmul(ar, ai, br, bi):
    return ar * br - ai * bi, ar * bi + ai * br


def _s5_kernel(u_ref, bblk_ref, ccat_ref, apow_ref, d_ref, wglu_ref, o_ref,
               x_ref, usc_ref, carry_ref, *, tm):
    n = S5_LANES
    step_i = pl.program_id(1)
    cur = (step_i + 1) % 2
    nxt = step_i % 2

    @pl.when(step_i == 0)
    def _():
        carry_ref[...] = jnp.zeros_like(carry_ref)
        x_ref[...] = jnp.zeros_like(x_ref)
        usc_ref[...] = jnp.zeros_like(usc_ref)

    x_prev = x_ref[cur]
    u_prev = usc_ref[cur]

    u = u_ref[...]
    usc_ref[nxt] = u
    bu = _dot(u.astype(BF16), bblk_ref[...])
    c_re = carry_ref[0:1, :]
    c_im = carry_ref[1:2, :]
    for gidx in range(tm // SUBLANES):
        r0 = gidx * SUBLANES
        x_re = bu[r0:r0 + SUBLANES, 0:n]
        x_im = bu[r0:r0 + SUBLANES, n:2 * n]
        for lvl, d in enumerate((1, 2, 4)):
            a_re = apow_ref[lvl * 2 * SUBLANES:lvl * 2 * SUBLANES + SUBLANES, :]
            a_im = apow_ref[lvl * 2 * SUBLANES + SUBLANES:(lvl + 1) * 2 * SUBLANES, :]
            s_re = pltpu.roll(x_re, d, 0)
            s_im = pltpu.roll(x_im, d, 0)
            t_re, t_im = _cmul(a_re, a_im, s_re, s_im)
            x_re = x_re + t_re
            x_im = x_im + t_im
        p_re = apow_ref[6 * SUBLANES:7 * SUBLANES, :]
        p_im = apow_ref[7 * SUBLANES:8 * SUBLANES, :]
        t_re, t_im = _cmul(p_re, p_im, c_re, c_im)
        x_re = x_re + t_re
        x_im = x_im + t_im
        x_ref[nxt, r0:r0 + SUBLANES, 0:n] = x_re
        x_ref[nxt, r0:r0 + SUBLANES, n:2 * n] = x_im
        c_re = x_re[SUBLANES - 1:SUBLANES, :]
        c_im = x_im[SUBLANES - 1:SUBLANES, :]
    carry_ref[0:1, :] = c_re
    carry_ref[1:2, :] = c_im

    yv = _dot(x_prev.astype(BF16), ccat_ref[...]) + d_ref[...] * u_prev
    yv = 0.5 * yv * (1.0 + jnp.tanh(0.7978845608028654 * (yv + 0.044715 * (yv * yv * yv))))
    z = _dot(yv.astype(BF16), wglu_ref[...])
    o_ref[...] = yv * jax.nn.sigmoid(z)


def _s5_params(lam_re, lam_im, log_dt, b_re, b_im, c_re, c_im):
    f32 = F32
    lre, lim = lam_re.astype(f32), lam_im.astype(f32)
    dt = jnp.exp(log_dt.astype(f32))[:, None]
    mag = jnp.exp(lre * dt)
    ab_re, ab_im = mag * jnp.cos(lim * dt), mag * jnp.sin(lim * dt)
    num_re, num_im = ab_re - 1.0, ab_im
    den = lre * lre + lim * lim
    coef_re = (num_re * lre + num_im * lim) / den
    coef_im = (num_im * lre - num_re * lim) / den
    br, bi = b_re.astype(f32), b_im.astype(f32)
    bb_re = coef_re[..., None] * br - coef_im[..., None] * bi
    bb_im = coef_re[..., None] * bi + coef_im[..., None] * br
    eye = jnp.eye(S5_GROUPS, dtype=f32)
    blk_re = jnp.einsum('gph,gk->ghkp', bb_re, eye).reshape(S5_WIDTH, S5_LANES)
    blk_im = jnp.einsum('gph,gk->ghkp', bb_im, eye).reshape(S5_WIDTH, S5_LANES)
    bblk = jnp.concatenate([blk_re, blk_im], axis=1).astype(BF16)
    cb_re = jnp.einsum('ghp,gk->gpkh', c_re.astype(f32), eye).reshape(S5_LANES, S5_WIDTH)
    cb_im = jnp.einsum('ghp,gk->gpkh', c_im.astype(f32), eye).reshape(S5_LANES, S5_WIDTH)
    ccat = jnp.concatenate([cb_re, -cb_im], axis=0).astype(BF16)
    a1 = (ab_re.reshape(1, -1), ab_im.reshape(1, -1))
    pows = [a1]
    for _ in range(SUBLANES - 1):
        pows.append(_cmul(pows[-1][0], pows[-1][1], a1[0], a1[1]))
    rid = jnp.arange(SUBLANES)[:, None]
    rows = []
    for d in (1, 2, 4):
        mask = (rid >= d).astype(f32)
        rows.append(mask * pows[d - 1][0])
        rows.append(mask * pows[d - 1][1])
    rows.append(jnp.concatenate([pows[r][0] for r in range(SUBLANES)], axis=0))
    rows.append(jnp.concatenate([pows[r][1] for r in range(SUBLANES)], axis=0))
    apow = jnp.concatenate(rows, axis=0)
    return bblk, ccat, apow


def _s5_mixer(cu, lam_re, lam_im, log_dt, b_re, b_im, c_re, c_im, d, w_glu, *, batch, seq,
              tm=_Tiles.s5_tm):
    t = batch * seq
    ns = seq // tm
    bblk, ccat, apow = _s5_params(lam_re, lam_im, log_dt, b_re, b_im, c_re, c_im)
    rin = lambda b, i: (b * ns + jnp.minimum(i, ns - 1), 0)
    rout = lambda b, i: (b * ns + jnp.maximum(i - 1, 0), 0)
    const = lambda b, i: (0, 0)
    return pl.pallas_call(
        functools.partial(_s5_kernel, tm=tm),
        out_shape=jax.ShapeDtypeStruct((t, S5_WIDTH), F32),
        grid=(batch, ns + 1),
        in_specs=[
            pl.BlockSpec((tm, S5_WIDTH), rin),
            pl.BlockSpec(bblk.shape, const),
            pl.BlockSpec(ccat.shape, const),
            pl.BlockSpec(apow.shape, const),
            pl.BlockSpec((1, S5_WIDTH), const),
            pl.BlockSpec((S5_WIDTH, S5_WIDTH), const),
        ],
        out_specs=pl.BlockSpec((tm, S5_WIDTH), rout),
        scratch_shapes=[
            pltpu.VMEM((2, tm, 2 * S5_LANES), F32),
            pltpu.VMEM((2, tm, S5_WIDTH), F32),
            pltpu.VMEM((SUBLANES, S5_LANES), F32),
        ],
        compiler_params=_cparams(("parallel", "arbitrary")),
        name="s5_mixer",
    )(cu, bblk, ccat, apow, d.reshape(1, -1).astype(F32), w_glu.astype(BF16))


def _route_rows(lt, n_tok):
    g = [lt[r:r + 1, :] for r in range(N_EXPERT_GROUPS)]
    gm = functools.reduce(jnp.maximum, g)
    gsum = functools.reduce(lambda a, b: a + b, [jnp.exp(x - gm) for x in g])
    g_p = 1.0 / gsum
    taken = jnp.zeros_like(gm) > 1.0
    g_hot = []
    for x in g:
        hit = jnp.logical_and(x == gm, jnp.logical_not(taken))
        g_hot.append(hit)
        taken = jnp.logical_or(taken, hit)
    e_sel = []
    for j in range(EXPERTS_PER_GROUP):
        acc = jnp.zeros_like(gm)
        for gi in range(N_EXPERT_GROUPS):
            r = ROUTER_EXPERT_ROW0 + gi * EXPERTS_PER_GROUP + j
            acc = acc + jnp.where(g_hot[gi], lt[r:r + 1, :], 0.0)
        e_sel.append(acc)
    m1 = functools.reduce(jnp.maximum, e_sel)
    taken = jnp.zeros_like(gm) > 1.0
    hot1 = []
    for x in e_sel:
        hit = jnp.logical_and(x == m1, jnp.logical_not(taken))
        hot1.append(hit)
        taken = jnp.logical_or(taken, hit)
    rest = [jnp.where(hh, NEG_BIG, x) for hh, x in zip(hot1, e_sel)]
    m2 = functools.reduce(jnp.maximum, rest)
    taken = jnp.zeros_like(gm) > 1.0
    hot2 = []
    for hh, x in zip(hot1, rest):
        hit = jnp.logical_and(jnp.logical_and(x == m2, jnp.logical_not(hh)), jnp.logical_not(taken))
        hot2.append(hit)
        taken = jnp.logical_or(taken, hit)
    e2 = jnp.exp(m2 - m1)
    w1 = g_p / (1.0 + e2)
    w2 = g_p * e2 / (1.0 + e2)
    rows = []
    for gi in range(N_EXPERT_GROUPS):
        for j in range(EXPERTS_PER_GROUP):
            val = jnp.where(hot1[j], w1, 0.0) + jnp.where(hot2[j], w2, 0.0)
            rows.append(jnp.where(g_hot[gi], val, 0.0))
    return jnp.concatenate(rows, axis=0)


def _outproj_kernel(h_ref, yaT_ref, yb_ref, yc_ref, wa_ref, wb_ref, wc_ref, g_ref, b_ref,
                    wrT_ref, br_ref, h1_ref, h1b_ref, comb_ref, *, tm):
    ya = yaT_ref[0].T
    mix = _dot(ya.astype(BF16), wa_ref[...])
    mix = mix + _dot(yb_ref[...].astype(BF16), wb_ref[...])
    mix = mix + _dot(yc_ref[...].astype(BF16), wc_ref[...])
    h1 = _layer_norm(ALPHA * h_ref[...] + mix, g_ref[...], b_ref[...])
    h1_ref[...] = h1
    h1b_ref[...] = h1.astype(BF16)
    h_hi, h_lo = _split_bf16(h1)
    w_hi = wrT_ref[0:LANES, :]
    w_lo = wrT_ref[LANES:2 * LANES, :]
    lt = _dot_nt(w_hi, h_hi) + _dot_nt(w_hi, h_lo) + _dot_nt(w_lo, h_hi) + br_ref[...]
    comb = _route_rows(lt, tm)
    combp = jnp.concatenate([comb, jnp.zeros((LANES - N_EXPERTS, tm), F32)], axis=0)
    comb_ref[...] = combp.T


def _out_projection(h, yaT, yb, yc, w_out, ln_g, ln_b, w_grp, b_grp, w_exp, b_exp,
                    *, batch, seq, tm=_Tiles.out_tm):
    t = batch * seq
    nt = seq // tm
    wa = w_out[0:DA_WIDTH].astype(BF16)
    wb = w_out[DA_WIDTH:DA_WIDTH + GDN_WIDTH].astype(BF16)
    wc = w_out[DA_WIDTH + GDN_WIDTH:].astype(BF16)
    wr = jnp.zeros((D_MODEL, LANES), F32)
    e0 = ROUTER_EXPERT_ROW0
    wr = wr.at[:, 0:N_EXPERT_GROUPS].set(w_grp.astype(F32)).at[:, e0:e0 + N_EXPERTS].set(w_exp.astype(F32))
    wrT = wr.T
    wr_hi = wrT.astype(BF16)
    wr_lo = (wrT - wr_hi.astype(F32)).astype(BF16)
    wr_cat = jnp.concatenate([wr_hi, wr_lo], axis=0)
    br = jnp.zeros((LANES, 1), F32)
    br = br.at[0:N_EXPERT_GROUPS, 0].set(b_grp.astype(F32)).at[e0:e0 + N_EXPERTS, 0].set(b_exp.astype(F32))
    row = lambda b, i: (b * nt + i, 0)
    const = lambda b, i: (0, 0)
    return pl.pallas_call(
        functools.partial(_outproj_kernel, tm=tm),
        out_shape=[
            jax.ShapeDtypeStruct((t, D_MODEL), F32),
            jax.ShapeDtypeStruct((t, D_MODEL), BF16),
            jax.ShapeDtypeStruct((t, LANES), F32),
        ],
        grid=(batch, nt),
        in_specs=[
            pl.BlockSpec((tm, D_MODEL), row),
            pl.BlockSpec((1, DA_WIDTH, tm), lambda b, i: (b, 0, i)),
            pl.BlockSpec((tm, GDN_WIDTH), row),
            pl.BlockSpec((tm, S5_WIDTH), row),
            pl.BlockSpec(wa.shape, const),
            pl.BlockSpec(wb.shape, const),
            pl.BlockSpec(wc.shape, const),
            pl.BlockSpec((1, D_MODEL), const),
            pl.BlockSpec((1, D_MODEL), const),
            pl.BlockSpec(wr_cat.shape, const),
            pl.BlockSpec((LANES, 1), const),
        ],
        out_specs=[
            pl.BlockSpec((tm, D_MODEL), row),
            pl.BlockSpec((tm, D_MODEL), row),
            pl.BlockSpec((tm, LANES), row),
        ],
        compiler_params=_cparams(("parallel", "parallel")),
        name="out_projection_router",
    )(h, yaT, yb, yc, wa, wb, wc, ln_g.reshape(1, -1), ln_b.reshape(1, -1), wr_cat, br)


MOE_EXPERTS_PER_STEP = _Tiles.moe_experts


def _moe_kernel(hb_ref, h1_ref, comb_ref, w1_ref, w3_ref, w2_ref, g_ref, b_ref, o_ref, acc_ref):
    s = pl.program_id(1)
    eps = MOE_EXPERTS_PER_STEP

    @pl.when(s == 0)
    def _():
        acc_ref[...] = jnp.zeros_like(acc_ref)

    x = hb_ref[...]
    lane = lax.broadcasted_iota(jnp.int32, (1, LANES), 1)
    comb = comb_ref[...]
    ups = [(_dot(x, w1_ref[j]), _dot(x, w3_ref[j])) for j in range(eps)]
    y = None
    for j, (a, b) in enumerate(ups):
        c = jnp.sum(jnp.where(lane == s * eps + j, comb, 0.0), axis=1, keepdims=True)
        hid = (a * jax.nn.sigmoid(a) * b * c).astype(BF16)
        part = _dot(hid, w2_ref[j])
        y = part if y is None else y + part
    acc_ref[...] += y

    @pl.when(s == N_EXPERTS // eps - 1)
    def _():
        o_ref[...] = _layer_norm(ALPHA * h1_ref[...] + acc_ref[...], g_ref[...], b_ref[...])


def _moe(h1, h1b, comb, w1, w3, w2, ln_g, ln_b, *, tm=_Tiles.moe_tm):
    t = h1.shape[0]
    nt = t // tm
    eps = MOE_EXPERTS_PER_STEP
    row = lambda i, e: (i, 0)
    const = lambda i, e: (0, 0)
    return pl.pallas_call(
        _moe_kernel,
        out_shape=jax.ShapeDtypeStruct((t, D_MODEL), F32),
        grid=(nt, N_EXPERTS // eps),
        in_specs=[
            pl.BlockSpec((tm, D_MODEL), row),
            pl.BlockSpec((tm, D_MODEL), row),
            pl.BlockSpec((tm, LANES), row),
            pl.BlockSpec((eps, D_MODEL, D_EXPERT), lambda i, e: (e, 0, 0)),
            pl.BlockSpec((eps, D_MODEL, D_EXPERT), lambda i, e: (e, 0, 0)),
            pl.BlockSpec((eps, D_EXPERT, D_MODEL), lambda i, e: (e, 0, 0)),
            pl.BlockSpec((1, D_MODEL), const),
            pl.BlockSpec((1, D_MODEL), const),
        ],
        out_specs=pl.BlockSpec((tm, D_MODEL), row),
        scratch_shapes=[pltpu.VMEM((tm, D_MODEL), F32)],
        compiler_params=_cparams(("parallel", "arbitrary")),
        name="moe_ffn",
    )(h1b, h1, comb, w1, w3, w2, ln_g.reshape(1, -1), ln_b.reshape(1, -1))


def _split_w_in(w):
    wt = w.T
    o = 0
    wq = wt[o:o + DA_WIDTH]; o += DA_WIDTH
    wk = wt[o:o + DA_WIDTH]; o += DA_WIDTH
    wv = wt[o:o + DA_WIDTH]; o += DA_WIDTH
    wg = wt[o:o + 4 * GDN_WIDTH]; o += 4 * GDN_WIDTH
    wbeta = wt[o:o + GDN_HEADS]; o += GDN_HEADS
    wa = wt[o:o + GDN_HEADS]; o += GDN_HEADS
    wc = wt[o:o + S5_WIDTH]
    kd = 2 * DA_HEAD_DIM
    zrows = lambda n: jnp.zeros((n, D_MODEL), w.dtype)
    wk_pad = [p for hh in range(DA_HEADS) for p in (wk[hh * kd:(hh + 1) * kd], zrows(K_PAD - kd))]
    wsm = [zrows(BETA_LANE0), wbeta, zrows(A_LANE0 - BETA_LANE0 - GDN_HEADS), wa,
           zrows(LANES - A_LANE0 - GDN_HEADS)]
    kone = jnp.zeros((1, DA_HEADS, K_PAD), F32).at[:, :, kd].set(1.0).reshape(1, DA_HEADS * K_PAD)
    w_all = jnp.concatenate([wq, wv] + wk_pad + [wg] + wsm + [wc], axis=0).astype(BF16)
    return w_all, kone


def kernel(x, ln_in_g, ln_in_b, w_in, w_out, lam_q1, lam_k1, lam_q2, lam_k2, diff_norm_g, dn_conv_w, dn_a_log, dn_dt_bias, dn_norm_g, s5_lambda_re, s5_lambda_im, s5_log_dt, s5_b_re, s5_b_im, s5_c_re, s5_c_im, s5_d, s5_w_glu, ln1_g, ln1_b, moe_w_grp, moe_b_grp, moe_w_exp, moe_b_exp, moe_w1, moe_w3, moe_w2, ln2_g, ln2_b):
    batch, seq, d = x.shape
    h = x.reshape(batch * seq, d)
    for l in range(DEPTH):
        lam_init = 0.8 - 0.6 * math.exp(-0.3 * l)
        wts = _split_w_in(w_in[l])
        outs = _in_projection(h, ln_in_g, ln_in_b, wts, batch=batch, seq=seq, apply_ln=(l == 0))
        if l == 0:
            h, qT, vT, k, gdn_in, small, cu = outs
        else:
            qT, vT, k, gdn_in, small, cu = outs
        lam = (jnp.exp(jnp.sum(lam_q1[l] * lam_k1[l])) - jnp.exp(jnp.sum(lam_q2[l] * lam_k2[l]))
               ).astype(F32) + lam_init
        yaT = _diff_attention(lam, qT, k, vT, diff_norm_g[l].astype(F32), lam_init=lam_init)
        yb = _gdn_mixer(gdn_in, small, dn_conv_w[l], dn_a_log[l], dn_dt_bias[l], dn_norm_g[l],
                        batch=batch, seq=seq)
        yc = _s5_mixer(cu, s5_lambda_re[l], s5_lambda_im[l], s5_log_dt[l], s5_b_re[l], s5_b_im[l],
                       s5_c_re[l], s5_c_im[l], s5_d[l], s5_w_glu[l], batch=batch, seq=seq)
        h1, h1b, comb = _out_projection(h, yaT, yb, yc, w_out[l], ln1_g[l], ln1_b[l],
                                        moe_w_grp[l], moe_b_grp[l], moe_w_exp[l], moe_b_exp[l],
                                        batch=batch, seq=seq)
        h = _moe(h1, h1b, comb, moe_w1[l].astype(BF16), moe_w3[l].astype(BF16),
                 moe_w2[l].astype(BF16), ln2_g[l], ln2_b[l])
    return h.reshape(batch, seq, d)
```

```python
import functools
import math

import jax
import jax.numpy as jnp
from jax import lax
from jax.experimental import pallas as pl
from jax.experimental.pallas import tpu as pltpu

F32 = jnp.float32
BF16 = jnp.bfloat16

D_MODEL = 1024
DEPTH = 2
CHUNK = 64
DA_HEADS = 6
DA_HEAD_DIM = 32
DA_V_DIM = 64
DA_WIDTH = 384
GDN_HEADS = 6
GDN_HEAD_DIM = 64
GDN_WIDTH = 384
CONV_K = 4
S5_GROUP_DIM = 16
S5_GROUPS = 16
S5_WIDTH = 256
S5_STATE = 64
S5_LANES = S5_GROUPS * S5_STATE
N_EXPERT_GROUPS = 4
EXPERTS_PER_GROUP = 4
N_EXPERTS = 16
D_EXPERT = 512
ALPHA = (2 * DEPTH) ** 0.25
LN_EPS = 1e-5
RMS_EPS = 1e-6
LOG2E = 1.4426950408889634

V7X_VMEM_LIMIT_BYTES = 56 * 1024 * 1024
SUBLANES = 8
BF16_SUBLANES = 16
LANES = 128
NEG_BIG = -1e30


class _Tiles:
    proj_tm = 512
    attn_tq = 1024
    attn_tk = 256
    gdn_rows = 128
    s5_tm = 256
    out_tm = 512
    moe_tm = 1024
    moe_experts = 4


ROUTER_EXPERT_ROW0 = 8
FAST_MAX_LOG2 = 100.0
K_PAD = LANES

BETA_LANE0 = 0
A_LANE0 = 8


def _cparams(sem):
    return pltpu.CompilerParams(dimension_semantics=sem, vmem_limit_bytes=V7X_VMEM_LIMIT_BYTES)


def _layer_norm(x, g, b):
    mu = jnp.mean(x, axis=-1, keepdims=True)
    xc = x - mu
    var = jnp.mean(xc * xc, axis=-1, keepdims=True)
    return xc * lax.rsqrt(var + LN_EPS) * g + b


def _dot(a, b):
    return jnp.dot(a, b, preferred_element_type=F32)


def _dot_nt(a, b):
    return lax.dot_general(a, b, (((1,), (1,)), ((), ())), preferred_element_type=F32)


def _dot_tn(a, b):
    return lax.dot_general(a, b, (((0,), (0,)), ((), ())), preferred_element_type=F32)


def _proj_kernel(x_ref, g_ref, b_ref, w_ref, kone_ref, *out_refs, apply_ln, q_scale):
    edges = (0, DA_WIDTH, 2 * DA_WIDTH, 2 * DA_WIDTH + DA_HEADS * K_PAD)
    edges = edges + (edges[-1] + 4 * GDN_WIDTH, edges[-1] + 4 * GDN_WIDTH + LANES,
                     edges[-1] + 4 * GDN_WIDTH + LANES + S5_WIDTH)
    wq_ref, wv_ref, wk_ref, wg_ref, wsm_ref, wc_ref = (
        w_ref.at[lo:hi, :] for lo, hi in zip(edges[:-1], edges[1:]))
    if apply_ln:
        h_ref, qT_ref, vT_ref, k_ref, gdn_ref, small_ref, cu_ref = out_refs
        h = _layer_norm(x_ref[...], g_ref[...], b_ref[...])
        h_ref[...] = h
    else:
        qT_ref, vT_ref, k_ref, gdn_ref, small_ref, cu_ref = out_refs
        h = x_ref[...]
    hb = h.astype(BF16)
    qT_ref[0] = (_dot_nt(wq_ref[...], hb) * q_scale).astype(BF16)
    vT_ref[0] = _dot_nt(wv_ref[...], hb).astype(BF16)
    k_ref[0] = (_dot_nt(hb, wk_ref[...]) + kone_ref[...]).astype(BF16)
    gdn_ref[...] = _dot_nt(hb, wg_ref[...])
    small_ref[...] = _dot_nt(hb, wsm_ref[...])
    cu_ref[...] = _dot_nt(hb, wc_ref[...])


def _in_projection(x2d, g, b, wts, *, batch, seq, apply_ln, tm=_Tiles.proj_tm):
    t = batch * seq
    nt = seq // tm
    w_all, kone = wts
    kw = DA_HEADS * K_PAD
    q_scale = (DA_HEAD_DIM ** -0.5) * LOG2E
    row = lambda bi, i: (bi * nt + i, 0)
    const = lambda bi, i: (0, 0)
    out_shape = [
        jax.ShapeDtypeStruct((batch, DA_WIDTH, seq), BF16),
        jax.ShapeDtypeStruct((batch, DA_WIDTH, seq), BF16),
        jax.ShapeDtypeStruct((batch, seq, kw), BF16),
        jax.ShapeDtypeStruct((t, 4 * GDN_WIDTH), F32),
        jax.ShapeDtypeStruct((t, LANES), F32),
        jax.ShapeDtypeStruct((t, S5_WIDTH), F32),
    ]
    out_specs = [
        pl.BlockSpec((1, DA_WIDTH, tm), lambda bi, i: (bi, 0, i)),
        pl.BlockSpec((1, DA_WIDTH, tm), lambda bi, i: (bi, 0, i)),
        pl.BlockSpec((1, tm, kw), lambda bi, i: (bi, i, 0)),
        pl.BlockSpec((tm, 4 * GDN_WIDTH), row),
        pl.BlockSpec((tm, LANES), row),
        pl.BlockSpec((tm, S5_WIDTH), row),
    ]
    if apply_ln:
        out_shape = [jax.ShapeDtypeStruct((t, D_MODEL), F32)] + out_shape
        out_specs = [pl.BlockSpec((tm, D_MODEL), row)] + out_specs
    in_specs = [
        pl.BlockSpec((tm, D_MODEL), row),
        pl.BlockSpec((1, D_MODEL), const),
        pl.BlockSpec((1, D_MODEL), const),
        pl.BlockSpec(w_all.shape, const),
        pl.BlockSpec(kone.shape, const),
    ]
    return pl.pallas_call(
        functools.partial(_proj_kernel, apply_ln=apply_ln, q_scale=q_scale),
        out_shape=out_shape,
        grid=(batch, nt),
        in_specs=in_specs,
        out_specs=out_specs,
        compiler_params=_cparams(("parallel", "parallel")),
        name="in_projection_ln" if apply_ln else "in_projection",
    )(x2d, g.reshape(1, -1), b.reshape(1, -1), w_all, kone)


def _attn_kernel(lam_ref, qT_ref, k_ref, vT_ref, g_ref, o_ref,
                 qbd_ref, pa_ref, pb_ref, m_ref, l_ref, acc_ref, cm_ref, *, tq, tk, out_scale):
    i = pl.program_id(2)
    dh = DA_HEAD_DIM
    ref_row = 2 * dh
    n_diag = 4
    assert tq == n_diag * tk

    def qk(j):
        start = pl.multiple_of(j * tk, tk)
        return _dot(k_ref[0, pl.ds(start, tk), :], qbd_ref[...])

    def pv(j, p):
        start = pl.multiple_of(j * tk, tk)
        return _dot(vT_ref[0, :, pl.ds(start, tk)], p.astype(BF16))

    def init_stats():
        m_ref[...] = jnp.full_like(m_ref, NEG_BIG)
        l_ref[...] = jnp.zeros_like(l_ref)
        acc_ref[...] = jnp.zeros_like(acc_ref)

    def exact_step(j, mask):
        s = qk(j)
        if mask is not None:
            s = jnp.where(mask, s, NEG_BIG)
        m_old = m_ref[...]
        m_new = jnp.maximum(m_old, jnp.max(s, axis=0, keepdims=True))
        alpha = jnp.exp2(m_old - m_new)
        p = jnp.exp2(s - m_new)
        l_ref[...] = alpha * l_ref[...] + jnp.sum(p, axis=0, keepdims=True)
        m_ref[...] = m_new
        acc_ref[...] = alpha * acc_ref[...] + pv(j, p)

    def score_exp(j):
        s = qk(j)
        cm_ref[...] = jnp.maximum(cm_ref[...], jnp.max(s, axis=0, keepdims=True))
        p = jnp.exp2(s)
        l_ref[...] += jnp.sum(p, axis=0, keepdims=True)
        return p.astype(BF16)

    def accumulate(p_ref, j):
        acc_ref[...] += pv(j, p_ref[...])

    kc = lax.broadcasted_iota(jnp.int32, (tk, 1), 0) // CHUNK
    col = lax.broadcasted_iota(jnp.int32, (1, 2 * tq), 1)
    qc = jnp.where(col >= tq, col - tq, col) // CHUNK

    def diag_mask(d):
        return (kc + d * (tk // CHUNK)) <= qc

    qbd_ref[...] = jnp.zeros_like(qbd_ref)
    q = qT_ref[0]
    qbd_ref[0:dh, 0:tq] = q[0:dh]
    qbd_ref[dh:2 * dh, tq:2 * tq] = q[dh:2 * dh]
    init_stats()

    off_diag = i > 0
    t_diag = n_diag * i
    k0 = pl.multiple_of(t_diag * tk, tk)
    s0 = _dot(k_ref[0, pl.ds(k0, CHUNK), :], qbd_ref[...])
    mref = jnp.max(s0, axis=0, keepdims=True).astype(BF16)
    qbd_ref[ref_row:ref_row + BF16_SUBLANES, :] = jnp.broadcast_to(-mref, (BF16_SUBLANES, 2 * tq))
    cm_ref[...] = jnp.zeros_like(cm_ref)

    def visible_columns(d):
        return ((d * tk, tq), (tq + d * tk, 2 * tq))

    def diag_score_exp(p_ref, d):
        start = pl.multiple_of((t_diag + d) * tk, tk)
        kt = k_ref[0, pl.ds(start, tk), :]
        mask = diag_mask(d)
        for lo, hi in visible_columns(d):
            s = jnp.where(mask[:, lo:hi], _dot(kt, qbd_ref[:, lo:hi]), NEG_BIG)
            cm_ref[:, lo:hi] = jnp.maximum(cm_ref[:, lo:hi], jnp.max(s, axis=0, keepdims=True))
            p = jnp.exp2(s)
            l_ref[:, lo:hi] += jnp.sum(p, axis=0, keepdims=True)
            p_ref[:, lo:hi] = p.astype(BF16)

    def diag_accumulate(p_ref, d):
        start = pl.multiple_of((t_diag + d) * tk, tk)
        vt = vT_ref[0, :, pl.ds(start, tk)]
        for lo, hi in visible_columns(d):
            acc_ref[:, lo:hi] += _dot(vt, p_ref[:, lo:hi])

    diag_score_exp(pa_ref, 0)
    diag_score_exp(pb_ref, 1)
    diag_accumulate(pa_ref, 0)
    diag_score_exp(pa_ref, 2)
    diag_accumulate(pb_ref, 1)
    diag_score_exp(pb_ref, 3)
    diag_accumulate(pa_ref, 2)
    diag_accumulate(pb_ref, 3)

    @pl.when(off_diag)
    def _():
        pa_ref[...] = score_exp(0)

        def quad(t, lookahead):
            pb_ref[...] = score_exp(t + 1)
            accumulate(pa_ref, t)
            pa_ref[...] = score_exp(t + 2)
            accumulate(pb_ref, t + 1)
            pb_ref[...] = score_exp(t + 3)
            accumulate(pa_ref, t + 2)
            if lookahead:
                pa_ref[...] = score_exp(t + 4)
            accumulate(pb_ref, t + 3)

        def octet(t, lookahead):
            quad(t, True)
            quad(t + 4, lookahead)

        def octet_body(r, carry):
            octet(8 * r, True)
            return carry

        odd = i % 2
        n_loop = i // 2 - 1 + odd
        lax.fori_loop(0, n_loop, octet_body, 0)
        t = 8 * n_loop

        @pl.when(odd == 1)
        def _():
            quad(t, False)

        @pl.when(odd == 0)
        def _():
            octet(t, False)

    @pl.when(jnp.max(cm_ref[...]) > FAST_MAX_LOG2)
    def _():
        qbd_ref[ref_row:ref_row + BF16_SUBLANES, :] = jnp.zeros((BF16_SUBLANES, 2 * tq), BF16)
        init_stats()

        def body(j, carry):
            exact_step(j, None)
            return carry

        lax.fori_loop(0, t_diag, body, 0)
        for d in range(n_diag):
            exact_step(t_diag + d, diag_mask(d))

    l = l_ref[...]
    acc = acc_ref[...]
    lam = lam_ref[0]
    o = acc[:, 0:tq] / l[:, 0:tq] - lam * (acc[:, tq:2 * tq] / l[:, tq:2 * tq])
    ms = jnp.mean(o * o, axis=0, keepdims=True)
    o_ref[0] = o * lax.rsqrt(ms + RMS_EPS) * g_ref[...] * out_scale


def _diff_attention(lam, qT, k, vT, norm_g, *, lam_init, tq=_Tiles.attn_tq, tk=_Tiles.attn_tk):
    batch, _, seq = qT.shape
    nq = seq // tq
    dv = DA_V_DIM
    return pl.pallas_call(
        functools.partial(_attn_kernel, tq=tq, tk=tk, out_scale=1.0 - lam_init),
        out_shape=jax.ShapeDtypeStruct((batch, DA_WIDTH, seq), F32),
        grid=(batch, DA_HEADS, nq),
        in_specs=[
            pl.BlockSpec(memory_space=pltpu.SMEM),
            pl.BlockSpec((1, dv, tq), lambda b, h, i: (b, h, i)),
            pl.BlockSpec((1, seq, LANES), lambda b, h, i: (b, 0, h)),
            pl.BlockSpec((1, dv, seq), lambda b, h, i: (b, h, 0)),
            pl.BlockSpec((dv, 1), lambda b, h, i: (0, 0)),
        ],
        out_specs=pl.BlockSpec((1, dv, tq), lambda b, h, i: (b, h, i)),
        scratch_shapes=[
            pltpu.VMEM((LANES, 2 * tq), BF16),
            pltpu.VMEM((tk, 2 * tq), BF16),
            pltpu.VMEM((tk, 2 * tq), BF16),
            pltpu.VMEM((1, 2 * tq), F32),
            pltpu.VMEM((1, 2 * tq), F32),
            pltpu.VMEM((dv, 2 * tq), F32),
            pltpu.VMEM((1, 2 * tq), F32),
        ],
        compiler_params=_cparams(("parallel", "parallel", "parallel")),
        name="diff_attention",
    )(lam.reshape(1), qT, k, vT, norm_g.reshape(dv, 1))


def _split_bf16(x):
    hi = x.astype(BF16)
    lo = (x - hi.astype(F32)).astype(BF16)
    return hi, lo


def _mm_bf16(a, b):
    return _dot(a.astype(BF16), b.astype(BF16))


def _gdn_kernel(qkv_ref, gate_ref, small_ref, convw_ref, gl_ref, ng_ref, o_ref,
                xbuf_ref, state_ref, pb16_ref, pf32_ref, prhs_ref, pegl_ref, *, rows):
    step_i = pl.program_id(1)
    dk = GDN_HEAD_DIM
    nch = rows // CHUNK
    halo = SUBLANES
    heads = range(GDN_HEADS)
    items = [(c, hh) for c in range(nch) for hh in heads]
    cur = (step_i + 1) % 2
    nxt = step_i % 2

    @pl.when(step_i == 0)
    def _():
        xbuf_ref[0:halo, :] = jnp.zeros((halo, 3 * GDN_WIDTH), F32)
        state_ref[...] = jnp.zeros_like(state_ref)
        pb16_ref[...] = jnp.zeros_like(pb16_ref)
        pf32_ref[...] = jnp.zeros_like(pf32_ref)
        prhs_ref[...] = jnp.zeros_like(prhs_ref)
        pegl_ref[...] = jnp.zeros_like(pegl_ref)

    qb, kbb, kbf, qdec, kdec, decays, gsilu, rhss, egl = {}, {}, {}, {}, {}, {}, {}, {}, {}
    for n, it in enumerate(items):
        qb[it] = pb16_ref[cur, 0, n]
        kbb[it] = pb16_ref[cur, 1, n]
        kbf[it] = pb16_ref[cur, 2, n]
        qdec[it] = pb16_ref[cur, 3, n]
        kdec[it] = pb16_ref[cur, 4, n]
        decays[it] = pf32_ref[cur, 0, n]
        gsilu[it] = pf32_ref[cur, 1, n]
        rhss[it] = prhs_ref[cur, n]
        egl[it] = pegl_ref[cur, n][0:1, 0:1]

    ri = lax.broadcasted_iota(jnp.int32, (CHUNK, CHUNK), 0)
    ci = lax.broadcasted_iota(jnp.int32, (CHUNK, CHUNK), 1)
    tri = ri >= ci
    strict = ri > ci

    def prepare_block():
        xbuf_ref[halo:halo + rows, :] = qkv_ref[...]
        y = convw_ref[CONV_K - 1:CONV_K, :] * xbuf_ref[halo:halo + rows, :]
        for j in range(CONV_K - 1):
            off = halo - (CONV_K - 1) + j
            y = y + convw_ref[j:j + 1, :] * xbuf_ref[off:off + rows, :]
        xbuf_ref[0:halo, :] = xbuf_ref[rows:rows + halo, :]
        y = y * jax.nn.sigmoid(y)

        small = small_ref[...]
        beta_all = jax.nn.sigmoid(small)
        sp_in = small + gl_ref[1:2, :]
        softplus = jnp.maximum(sp_in, 0.0) + jnp.log(1.0 + jnp.exp(-jnp.abs(sp_in)))
        g_all = gl_ref[0:1, :] * softplus
        tril_f = tri.astype(F32)
        gc_parts = []
        for c in range(nch):
            gch = g_all[c * CHUNK:(c + 1) * CHUNK, :]
            gc_parts.append(jnp.dot(tril_f, gch, preferred_element_type=F32,
                                    precision=lax.Precision.HIGHEST))
        gc_all = jnp.concatenate(gc_parts, axis=0) if nch > 1 else gc_parts[0]
        pad = (-rows) % LANES
        gc_sq = jnp.concatenate([gc_all, jnp.zeros((pad, LANES), F32)], axis=0) if pad else gc_all
        gcT = gc_sq.T

        gate = gate_ref[...]
        for n, it in enumerate(items):
            c, hh = it
            r0 = c * CHUNK
            q = y[r0:r0 + CHUNK, hh * dk:(hh + 1) * dk]
            k = y[r0:r0 + CHUNK, GDN_WIDTH + hh * dk:GDN_WIDTH + (hh + 1) * dk]
            v = y[r0:r0 + CHUNK, 2 * GDN_WIDTH + hh * dk:2 * GDN_WIDTH + (hh + 1) * dk]
            q = q * lax.rsqrt(jnp.sum(q * q, axis=-1, keepdims=True) + RMS_EPS) * (dk ** -0.5)
            k = k * lax.rsqrt(jnp.sum(k * k, axis=-1, keepdims=True) + RMS_EPS)
            beta = beta_all[r0:r0 + CHUNK, BETA_LANE0 + hh:BETA_LANE0 + hh + 1]
            gcol = gc_all[r0:r0 + CHUNK, A_LANE0 + hh:A_LANE0 + hh + 1]
            grow = gcT[A_LANE0 + hh:A_LANE0 + hh + 1, r0:r0 + CHUNK]
            glast = gcT[A_LANE0 + hh:A_LANE0 + hh + 1, r0 + CHUNK - 1:r0 + CHUNK]
            eg = jnp.exp(gcol)
            kb = k * beta
            gt = gate[r0:r0 + CHUNK, hh * dk:(hh + 1) * dk]
            pb16_ref[nxt, 0, n] = q.astype(BF16)
            pb16_ref[nxt, 1, n] = kb.astype(BF16)
            pb16_ref[nxt, 2, n] = k.astype(BF16)
            pb16_ref[nxt, 3, n] = (q * eg).astype(BF16)
            pb16_ref[nxt, 4, n] = (k * jnp.exp(glast - gcol)).astype(BF16)
            pf32_ref[nxt, 0, n] = jnp.where(tri, jnp.exp(jnp.where(tri, gcol - grow, 0.0)), 0.0)
            pf32_ref[nxt, 1, n] = gt * jax.nn.sigmoid(gt)
            prhs_ref[nxt, n] = jnp.concatenate([v * beta, kb * eg], axis=1)
            pegl_ref[nxt, n] = jnp.broadcast_to(jnp.exp(glast), (SUBLANES, LANES))

    ng = ng_ref[...]
    kk = {it: _dot_nt(kbb[it], kbf[it]) for it in items}
    qk = {it: _dot_nt(qb[it], kbf[it]) for it in items}
    lm = {it: jnp.where(strict, kk[it] * decays[it], 0.0) for it in items}
    a_intra = {it: qk[it] * decays[it] for it in items}
    xs = {it: rhss[it] - _mm_bf16(lm[it], rhss[it]) for it in items}
    ps = lm
    for _ in range(5):
        ps = {it: _mm_bf16(ps[it], ps[it]) for it in items}
        xs = {it: xs[it] + _mm_bf16(ps[it], xs[it]) for it in items}

    state = [state_ref[hh] for hh in heads]
    for c in range(nch):
        r0 = c * CHUNK
        stb = [state[hh].astype(BF16) for hh in heads]
        ws = [_dot(xs[(c, hh)][:, dk:2 * dk].astype(BF16), stb[hh]) for hh in heads]
        qst = [_dot(qdec[(c, hh)], stb[hh]) for hh in heads]
        vn = [(xs[(c, hh)][:, 0:dk] - ws[hh]).astype(BF16) for hh in heads]
        av = [_dot(a_intra[(c, hh)].astype(BF16), vn[hh]) for hh in heads]
        kv = [_dot_tn(kdec[(c, hh)], vn[hh]) for hh in heads]
        for hh in heads:
            state[hh] = state[hh] * egl[(c, hh)] + kv[hh]
            o = qst[hh] + av[hh]
            ms = jnp.mean(o * o, axis=-1, keepdims=True)
            on = o * lax.rsqrt(ms + RMS_EPS) * ng
            o_ref[r0:r0 + CHUNK, hh * dk:(hh + 1) * dk] = on * gsilu[(c, hh)]
    for hh in heads:
        state_ref[hh] = state[hh]

    prepare_block()


def _gdn_mixer(gdn_in, small, conv_w, a_log, dt_bias, norm_g, *, batch, seq, rows=_Tiles.gdn_rows):
    t = batch * seq
    ns = seq // rows
    gl = jnp.zeros((SUBLANES, LANES), F32)
    gl = gl.at[0, A_LANE0:A_LANE0 + GDN_HEADS].set(-jnp.exp(a_log.astype(F32)))
    gl = gl.at[1, A_LANE0:A_LANE0 + GDN_HEADS].set(dt_bias.astype(F32))
    convw = jnp.zeros((SUBLANES, 3 * GDN_WIDTH), F32).at[0:CONV_K].set(conv_w.astype(F32))
    nitems = (rows // CHUNK) * GDN_HEADS
    dk = GDN_HEAD_DIM
    rin = lambda b, i: (b * ns + jnp.minimum(i, ns - 1), 0)
    rout = lambda b, i: (b * ns + jnp.maximum(i - 1, 0), 0)
    const = lambda b, i: (0, 0)
    return pl.pallas_call(
        functools.partial(_gdn_kernel, rows=rows),
        out_shape=jax.ShapeDtypeStruct((t, GDN_WIDTH), F32),
        grid=(batch, ns + 1),
        in_specs=[
            pl.BlockSpec((rows, 3 * GDN_WIDTH), rin),
            pl.BlockSpec((rows, GDN_WIDTH), lambda b, i: (b * ns + jnp.minimum(i, ns - 1), 3)),
            pl.BlockSpec((rows, LANES), rin),
            pl.BlockSpec((SUBLANES, 3 * GDN_WIDTH), const),
            pl.BlockSpec((SUBLANES, LANES), const),
            pl.BlockSpec((1, GDN_HEAD_DIM), const),
        ],
        out_specs=pl.BlockSpec((rows, GDN_WIDTH), rout),
        scratch_shapes=[
            pltpu.VMEM((rows + SUBLANES, 3 * GDN_WIDTH), F32),
            pltpu.VMEM((GDN_HEADS, dk, dk), F32),
            pltpu.VMEM((2, 5, nitems, CHUNK, dk), BF16),
            pltpu.VMEM((2, 2, nitems, CHUNK, dk), F32),
            pltpu.VMEM((2, nitems, CHUNK, 2 * dk), F32),
            pltpu.VMEM((2, nitems, SUBLANES, LANES), F32),
        ],
        compiler_params=_cparams(("parallel", "arbitrary")),
        name="gated_deltanet",
    )(gdn_in, gdn_in, small, convw, gl, norm_g.reshape(1, -1).astype(F32))


def _cmul(ar, ai, br, bi):
    return ar * br - ai * bi, ar * bi + ai * br


def _s5_kernel(u_ref, bblk_ref, ccat_ref, apow_ref, d_ref, wglu_ref, o_ref,
               x_ref, usc_ref, carry_ref, *, tm):
    n = S5_LANES
    step_i = pl.program_id(1)
    cur = (step_i + 1) % 2
    nxt = step_i % 2

    @pl.when(step_i == 0)
    def _():
        carry_ref[...] = jnp.zeros_like(carry_ref)
        x_ref[...] = jnp.zeros_like(x_ref)
        usc_ref[...] = jnp.zeros_like(usc_ref)

    x_prev = x_ref[cur]
    u_prev = usc_ref[cur]

    u = u_ref[...]
    usc_ref[nxt] = u
    bu = _dot(u.astype(BF16), bblk_ref[...])
    c_re = carry_ref[0:1, :]
    c_im = carry_ref[1:2, :]
    for gidx in range(tm // SUBLANES):
        r0 = gidx * SUBLANES
        x_re = bu[r0:r0 + SUBLANES, 0:n]
        x_im = bu[r0:r0 + SUBLANES, n:2 * n]
        for lvl, d in enumerate((1, 2, 4)):
            a_re = apow_ref[lvl * 2 * SUBLANES:lvl * 2 * SUBLANES + SUBLANES, :]
            a_im = apow_ref[lvl * 2 * SUBLANES + SUBLANES:(lvl + 1) * 2 * SUBLANES, :]
            s_re = pltpu.roll(x_re, d, 0)
            s_im = pltpu.roll(x_im, d, 0)
            t_re, t_im = _cmul(a_re, a_im, s_re, s_im)
            x_re = x_re + t_re
            x_im = x_im + t_im
        p_re = apow_ref[6 * SUBLANES:7 * SUBLANES, :]
        p_im = apow_ref[7 * SUBLANES:8 * SUBLANES, :]
        t_re, t_im = _cmul(p_re, p_im, c_re, c_im)
        x_re = x_re + t_re
        x_im = x_im + t_im
        x_ref[nxt, r0:r0 + SUBLANES, 0:n] = x_re
        x_ref[nxt, r0:r0 + SUBLANES, n:2 * n] = x_im
        c_re = x_re[SUBLANES - 1:SUBLANES, :]
        c_im = x_im[SUBLANES - 1:SUBLANES, :]
    carry_ref[0:1, :] = c_re
    carry_ref[1:2, :] = c_im

    yv = _dot(x_prev.astype(BF16), ccat_ref[...]) + d_ref[...] * u_prev
    yv = 0.5 * yv * (1.0 + jnp.tanh(0.7978845608028654 * (yv + 0.044715 * (yv * yv * yv))))
    z = _dot(yv.astype(BF16), wglu_ref[...])
    o_ref[...] = yv * jax.nn.sigmoid(z)


def _s5_params(lam_re, lam_im, log_dt, b_re, b_im, c_re, c_im):
    f32 = F32
    lre, lim = lam_re.astype(f32), lam_im.astype(f32)
    dt = jnp.exp(log_dt.astype(f32))[:, None]
    mag = jnp.exp(lre * dt)
    ab_re, ab_im = mag * jnp.cos(lim * dt), mag * jnp.sin(lim * dt)
    num_re, num_im = ab_re - 1.0, ab_im
    den = lre * lre + lim * lim
    coef_re = (num_re * lre + num_im * lim) / den
    coef_im = (num_im * lre - num_re * lim) / den
    br, bi = b_re.astype(f32), b_im.astype(f32)
    bb_re = coef_re[..., None] * br - coef_im[..., None] * bi
    bb_im = coef_re[..., None] * bi + coef_im[..., None] * br
    eye = jnp.eye(S5_GROUPS, dtype=f32)
    blk_re = jnp.einsum('gph,gk->ghkp', bb_re, eye).reshape(S5_WIDTH, S5_LANES)
    blk_im = jnp.einsum('gph,gk->ghkp', bb_im, eye).reshape(S5_WIDTH, S5_LANES)
    bblk = jnp.concatenate([blk_re, blk_im], axis=1).astype(BF16)
    cb_re = jnp.einsum('ghp,gk->gpkh', c_re.astype(f32), eye).reshape(S5_LANES, S5_WIDTH)
    cb_im = jnp.einsum('ghp,gk->gpkh', c_im.astype(f32), eye).reshape(S5_LANES, S5_WIDTH)
    ccat = jnp.concatenate([cb_re, -cb_im], axis=0).astype(BF16)
    a1 = (ab_re.reshape(1, -1), ab_im.reshape(1, -1))
    pows = [a1]
    for _ in range(SUBLANES - 1):
        pows.append(_cmul(pows[-1][0], pows[-1][1], a1[0], a1[1]))
    rid = jnp.arange(SUBLANES)[:, None]
    rows = []
    for d in (1, 2, 4):
        mask = (rid >= d).astype(f32)
        rows.append(mask * pows[d - 1][0])
        rows.append(mask * pows[d - 1][1])
    rows.append(jnp.concatenate([pows[r][0] for r in range(SUBLANES)], axis=0))
    rows.append(jnp.concatenate([pows[r][1] for r in range(SUBLANES)], axis=0))
    apow = jnp.concatenate(rows, axis=0)
    return bblk, ccat, apow


def _s5_mixer(cu, lam_re, lam_im, log_dt, b_re, b_im, c_re, c_im, d, w_glu, *, batch, seq,
              tm=_Tiles.s5_tm):
    t = batch * seq
    ns = seq // tm
    bblk, ccat, apow = _s5_params(lam_re, lam_im, log_dt, b_re, b_im, c_re, c_im)
    rin = lambda b, i: (b * ns + jnp.minimum(i, ns - 1), 0)
    rout = lambda b, i: (b * ns + jnp.maximum(i - 1, 0), 0)
    const = lambda b, i: (0, 0)
    return pl.pallas_call(
        functools.partial(_s5_kernel, tm=tm),
        out_shape=jax.ShapeDtypeStruct((t, S5_WIDTH), F32),
        grid=(batch, ns + 1),
        in_specs=[
            pl.BlockSpec((tm, S5_WIDTH), rin),
            pl.BlockSpec(bblk.shape, const),
            pl.BlockSpec(ccat.shape, const),
            pl.BlockSpec(apow.shape, const),
            pl.BlockSpec((1, S5_WIDTH), const),
            pl.BlockSpec((S5_WIDTH, S5_WIDTH), const),
        ],
        out_specs=pl.BlockSpec((tm, S5_WIDTH), rout),
        scratch_shapes=[
            pltpu.VMEM((2, tm, 2 * S5_LANES), F32),
            pltpu.VMEM((2, tm, S5_WIDTH), F32),
            pltpu.VMEM((SUBLANES, S5_LANES), F32),
        ],
        compiler_params=_cparams(("parallel", "arbitrary")),
        name="s5_mixer",
    )(cu, bblk, ccat, apow, d.reshape(1, -1).astype(F32), w_glu.astype(BF16))


def _route_rows(lt, n_tok):
    g = [lt[r:r + 1, :] for r in range(N_EXPERT_GROUPS)]
    gm = functools.reduce(jnp.maximum, g)
    gsum = functools.reduce(lambda a, b: a + b, [jnp.exp(x - gm) for x in g])
    g_p = 1.0 / gsum
    taken = jnp.zeros_like(gm) > 1.0
    g_hot = []
    for x in g:
        hit = jnp.logical_and(x == gm, jnp.logical_not(taken))
        g_hot.append(hit)
        taken = jnp.logical_or(taken, hit)
    e_sel = []
    for j in range(EXPERTS_PER_GROUP):
        acc = jnp.zeros_like(gm)
        for gi in range(N_EXPERT_GROUPS):
            r = ROUTER_EXPERT_ROW0 + gi * EXPERTS_PER_GROUP + j
            acc = acc + jnp.where(g_hot[gi], lt[r:r + 1, :], 0.0)
        e_sel.append(acc)
    m1 = functools.reduce(jnp.maximum, e_sel)
    taken = jnp.zeros_like(gm) > 1.0
    hot1 = []
    for x in e_sel:
        hit = jnp.logical_and(x == m1, jnp.logical_not(taken))
        hot1.append(hit)
        taken = jnp.logical_or(taken, hit)
    rest = [jnp.where(hh, NEG_BIG, x) for hh, x in zip(hot1, e_sel)]
    m2 = functools.reduce(jnp.maximum, rest)
    taken = jnp.zeros_like(gm) > 1.0
    hot2 = []
    for hh, x in zip(hot1, rest):
        hit = jnp.logical_and(jnp.logical_and(x == m2, jnp.logical_not(hh)), jnp.logical_not(taken))
        hot2.append(hit)
        taken = jnp.logical_or(taken, hit)
    e2 = jnp.exp(m2 - m1)
    w1 = g_p / (1.0 + e2)
    w2 = g_p * e2 / (1.0 + e2)
    rows = []
    for gi in range(N_EXPERT_GROUPS):
        for j in range(EXPERTS_PER_GROUP):
            val = jnp.where(hot1[j], w1, 0.0) + jnp.where(hot2[j], w2, 0.0)
            rows.append(jnp.where(g_hot[gi], val, 0.0))
    return jnp.concatenate(rows, axis=0)


def _outproj_kernel(h_ref, yaT_ref, yb_ref, yc_ref, wa_ref, wb_ref, wc_ref, g_ref, b_ref,
                    wrT_ref, br_ref, h1_ref, h1b_ref, comb_ref, *, tm):
    ya = yaT_ref[0].T
    mix = _dot(ya.astype(BF16), wa_ref[...])
    mix = mix + _dot(yb_ref[...].astype(BF16), wb_ref[...])
    mix = mix + _dot(yc_ref[...].astype(BF16), wc_ref[...])
    h1 = _layer_norm(ALPHA * h_ref[...] + mix, g_ref[...], b_ref[...])
    h1_ref[...] = h1
    h1b_ref[...] = h1.astype(BF16)
    h_hi, h_lo = _split_bf16(h1)
    w_hi = wrT_ref[0:LANES, :]
    w_lo = wrT_ref[LANES:2 * LANES, :]
    lt = _dot_nt(w_hi, h_hi) + _dot_nt(w_hi, h_lo) + _dot_nt(w_lo, h_hi) + br_ref[...]
    comb = _route_rows(lt, tm)
    combp = jnp.concatenate([comb, jnp.zeros((LANES - N_EXPERTS, tm), F32)], axis=0)
    comb_ref[...] = combp.T


def _out_projection(h, yaT, yb, yc, w_out, ln_g, ln_b, w_grp, b_grp, w_exp, b_exp,
                    *, batch, seq, tm=_Tiles.out_tm):
    t = batch * seq
    nt = seq // tm
    wa = w_out[0:DA_WIDTH].astype(BF16)
    wb = w_out[DA_WIDTH:DA_WIDTH + GDN_WIDTH].astype(BF16)
    wc = w_out[DA_WIDTH + GDN_WIDTH:].astype(BF16)
    wr = jnp.zeros((D_MODEL, LANES), F32)
    e0 = ROUTER_EXPERT_ROW0
    wr = wr.at[:, 0:N_EXPERT_GROUPS].set(w_grp.astype(F32)).at[:, e0:e0 + N_EXPERTS].set(w_exp.astype(F32))
    wrT = wr.T
    wr_hi = wrT.astype(BF16)
    wr_lo = (wrT - wr_hi.astype(F32)).astype(BF16)
    wr_cat = jnp.concatenate([wr_hi, wr_lo], axis=0)
    br = jnp.zeros((LANES, 1), F32)
    br = br.at[0:N_EXPERT_GROUPS, 0].set(b_grp.astype(F32)).at[e0:e0 + N_EXPERTS, 0].set(b_exp.astype(F32))
    row = lambda b, i: (b * nt + i, 0)
    const = lambda b, i: (0, 0)
    return pl.pallas_call(
        functools.partial(_outproj_kernel, tm=tm),
        out_shape=[
            jax.ShapeDtypeStruct((t, D_MODEL), F32),
            jax.ShapeDtypeStruct((t, D_MODEL), BF16),
            jax.ShapeDtypeStruct((t, LANES), F32),
        ],
        grid=(batch, nt),
        in_specs=[
            pl.BlockSpec((tm, D_MODEL), row),
            pl.BlockSpec((1, DA_WIDTH, tm), lambda b, i: (b, 0, i)),
            pl.BlockSpec((tm, GDN_WIDTH), row),
            pl.BlockSpec((tm, S5_WIDTH), row),
            pl.BlockSpec(wa.shape, const),
            pl.BlockSpec(wb.shape, const),
            pl.BlockSpec(wc.shape, const),
            pl.BlockSpec((1, D_MODEL), const),
            pl.BlockSpec((1, D_MODEL), const),
            pl.BlockSpec(wr_cat.shape, const),
            pl.BlockSpec((LANES, 1), const),
        ],
        out_specs=[
            pl.BlockSpec((tm, D_MODEL), row),
            pl.BlockSpec((tm, D_MODEL), row),
            pl.BlockSpec((tm, LANES), row),
        ],
        compiler_params=_cparams(("parallel", "parallel")),
        name="out_projection_router",
    )(h, yaT, yb, yc, wa, wb, wc, ln_g.reshape(1, -1), ln_b.reshape(1, -1), wr_cat, br)


MOE_EXPERT_GROUP = _Tiles.moe_experts


def _moe_kernel(hb_ref, h1_ref, comb_ref, w1_hbm, w3_hbm, w2_hbm, g_ref, b_ref, o_ref,
                w1_buf, w3_buf, w2_buf, acc_ref, sem):
    def weight_copies(e, slot):
        return (pltpu.make_async_copy(w1_hbm.at[e], w1_buf.at[slot], sem.at[0, slot]),
                pltpu.make_async_copy(w3_hbm.at[e], w3_buf.at[slot], sem.at[1, slot]),
                pltpu.make_async_copy(w2_hbm.at[e], w2_buf.at[slot], sem.at[2, slot]))

    group = MOE_EXPERT_GROUP
    last = N_EXPERTS - 1
    for cp in weight_copies(0, 0):
        cp.start()
    acc_ref[...] = jnp.zeros_like(acc_ref)
    lane = lax.broadcasted_iota(jnp.int32, (1, LANES), 1)

    def expert_group(gi, carry):
        x = hb_ref[...]
        comb = comb_ref[...]
        y = None
        for j in range(group):
            e = gi * group + j
            slot = j % 2
            for cp in weight_copies(jnp.minimum(e + 1, last), 1 - slot):
                cp.start()
            for cp in weight_copies(e, slot):
                cp.wait()
            a = _dot(x, w1_buf[slot])
            b = _dot(x, w3_buf[slot])
            c = jnp.sum(jnp.where(lane == e, comb, 0.0), axis=1, keepdims=True)
            hid = (a * jax.nn.sigmoid(a) * b * c).astype(BF16)
            part = _dot(hid, w2_buf[slot])
            y = part if y is None else y + part
        acc_ref[...] += y
        return carry

    lax.fori_loop(0, N_EXPERTS // group, expert_group, 0)
    for cp in weight_copies(last, 0):
        cp.wait()
    o_ref[...] = _layer_norm(ALPHA * h1_ref[...] + acc_ref[...], g_ref[...], b_ref[...])


def _moe(h1, h1b, comb, w1, w3, w2, ln_g, ln_b, *, tm=_Tiles.moe_tm):
    t = h1.shape[0]
    nt = t // tm
    row = lambda i: (i, 0)
    const = lambda i: (0, 0)
    return pl.pallas_call(
        _moe_kernel,
        out_shape=jax.ShapeDtypeStruct((t, D_MODEL), F32),
        grid=(nt,),
        in_specs=[
            pl.BlockSpec((tm, D_MODEL), row),
            pl.BlockSpec((tm, D_MODEL), row),
            pl.BlockSpec((tm, LANES), row),
            pl.BlockSpec(memory_space=pl.ANY),
            pl.BlockSpec(memory_space=pl.ANY),
            pl.BlockSpec(memory_space=pl.ANY),
            pl.BlockSpec((1, D_MODEL), const),
            pl.BlockSpec((1, D_MODEL), const),
        ],
        out_specs=pl.BlockSpec((tm, D_MODEL), row),
        scratch_shapes=[
            pltpu.VMEM((2, D_MODEL, D_EXPERT), BF16),
            pltpu.VMEM((2, D_MODEL, D_EXPERT), BF16),
            pltpu.VMEM((2, D_EXPERT, D_MODEL), BF16),
            pltpu.VMEM((tm, D_MODEL), F32),
            pltpu.SemaphoreType.DMA((3, 2)),
        ],
        compiler_params=_cparams(("arbitrary",)),
        name="moe_ffn",
    )(h1b, h1, comb, w1, w3, w2, ln_g.reshape(1, -1), ln_b.reshape(1, -1))


def _split_w_in(w):
    wt = w.T
    o = 0
    wq = wt[o:o + DA_WIDTH]; o += DA_WIDTH
    wk = wt[o:o + DA_WIDTH]; o += DA_WIDTH
    wv = wt[o:o + DA_WIDTH]; o += DA_WIDTH
    wg = wt[o:o + 4 * GDN_WIDTH]; o += 4 * GDN_WIDTH
    wbeta = wt[o:o + GDN_HEADS]; o += GDN_HEADS
    wa = wt[o:o + GDN_HEADS]; o += GDN_HEADS
    wc = wt[o:o + S5_WIDTH]
    kd = 2 * DA_HEAD_DIM
    zrows = lambda n: jnp.zeros((n, D_MODEL), w.dtype)
    wk_pad = [p for hh in range(DA_HEADS) for p in (wk[hh * kd:(hh + 1) * kd], zrows(K_PAD - kd))]
    wsm = [zrows(BETA_LANE0), wbeta, zrows(A_LANE0 - BETA_LANE0 - GDN_HEADS), wa,
           zrows(LANES - A_LANE0 - GDN_HEADS)]
    kone = jnp.zeros((1, DA_HEADS, K_PAD), F32).at[:, :, kd].set(1.0).reshape(1, DA_HEADS * K_PAD)
    w_all = jnp.concatenate([wq, wv] + wk_pad + [wg] + wsm + [wc], axis=0).astype(BF16)
    return w_all, kone


def kernel(x, ln_in_g, ln_in_b, w_in, w_out, lam_q1, lam_k1, lam_q2, lam_k2, diff_norm_g, dn_conv_w, dn_a_log, dn_dt_bias, dn_norm_g, s5_lambda_re, s5_lambda_im, s5_log_dt, s5_b_re, s5_b_im, s5_c_re, s5_c_im, s5_d, s5_w_glu, ln1_g, ln1_b, moe_w_grp, moe_b_grp, moe_w_exp, moe_b_exp, moe_w1, moe_w3, moe_w2, ln2_g, ln2_b):
    batch, seq, d = x.shape
    h = x.reshape(batch * seq, d)
    for l in range(DEPTH):
        lam_init = 0.8 - 0.6 * math.exp(-0.3 * l)
        wts = _split_w_in(w_in[l])
        outs = _in_projection(h, ln_in_g, ln_in_b, wts, batch=batch, seq=seq, apply_ln=(l == 0))
        if l == 0:
            h, qT, vT, k, gdn_in, small, cu = outs
        else:
            qT, vT, k, gdn_in, small, cu = outs
        lam = (jnp.exp(jnp.sum(lam_q1[l] * lam_k1[l])) - jnp.exp(jnp.sum(lam_q2[l] * lam_k2[l]))
               ).astype(F32) + lam_init
        yaT = _diff_attention(lam, qT, k, vT, diff_norm_g[l].astype(F32), lam_init=lam_init)
        yb = _gdn_mixer(gdn_in, small, dn_conv_w[l], dn_a_log[l], dn_dt_bias[l], dn_norm_g[l],
                        batch=batch, seq=seq)
        yc = _s5_mixer(cu, s5_lambda_re[l], s5_lambda_im[l], s5_log_dt[l], s5_b_re[l], s5_b_im[l],
                       s5_c_re[l], s5_c_im[l], s5_d[l], s5_w_glu[l], batch=batch, seq=seq)
        h1, h1b, comb = _out_projection(h, yaT, yb, yc, w_out[l], ln1_g[l], ln1_b[l],
                                        moe_w_grp[l], moe_b_grp[l], moe_w_exp[l], moe_b_exp[l],
                                        batch=batch, seq=seq)
        h = _moe(h1, h1b, comb, moe_w1[l].astype(BF16), moe_w3[l].astype(BF16),
                 moe_w2[l].astype(BF16), ln2_g[l], ln2_b[l])
    return h.reshape(batch, seq, d)
```

```python
import functools
import math

import jax
import jax.numpy as jnp
from jax import lax
from jax.experimental import pallas as pl
from jax.experimental.pallas import tpu as pltpu

F32 = jnp.float32
BF16 = jnp.bfloat16

D_MODEL = 1024
DEPTH = 2
CHUNK = 64
DA_HEADS = 6
DA_HEAD_DIM = 32
DA_V_DIM = 64
DA_WIDTH = 384
GDN_HEADS = 6
GDN_HEAD_DIM = 64
GDN_WIDTH = 384
CONV_K = 4
S5_GROUP_DIM = 16
S5_GROUPS = 16
S5_WIDTH = 256
S5_STATE = 64
S5_LANES = S5_GROUPS * S5_STATE
N_EXPERT_GROUPS = 4
EXPERTS_PER_GROUP = 4
N_EXPERTS = 16
D_EXPERT = 512
ALPHA = (2 * DEPTH) ** 0.25
LN_EPS = 1e-5
RMS_EPS = 1e-6
LOG2E = 1.4426950408889634

V7X_VMEM_LIMIT_BYTES = 56 * 1024 * 1024
SUBLANES = 8
BF16_SUBLANES = 16
LANES = 128
NEG_BIG = -1e30


class _Tiles:
    proj_tm = 512
    attn_tq = 1024
    attn_tk = 256
    gdn_rows = 128
    s5_tm = 256
    out_tm = 512
    moe_tm = 1024
    moe_experts = 2


ROUTER_EXPERT_ROW0 = 8
FAST_MAX_LOG2 = 100.0
K_PAD = LANES

BETA_LANE0 = 0
A_LANE0 = 8


def _cparams(sem):
    return pltpu.CompilerParams(dimension_semantics=sem, vmem_limit_bytes=V7X_VMEM_LIMIT_BYTES)


def _layer_norm(x, g, b):
    mu = jnp.mean(x, axis=-1, keepdims=True)
    xc = x - mu
    var = jnp.mean(xc * xc, axis=-1, keepdims=True)
    return xc * lax.rsqrt(var + LN_EPS) * g + b


def _dot(a, b):
    return jnp.dot(a, b, preferred_element_type=F32)


def _dot_nt(a, b):
    return lax.dot_general(a, b, (((1,), (1,)), ((), ())), preferred_element_type=F32)


def _dot_tn(a, b):
    return lax.dot_general(a, b, (((0,), (0,)), ((), ())), preferred_element_type=F32)


def _proj_kernel(x_ref, g_ref, b_ref, w_ref, kone_ref, *out_refs, apply_ln, q_scale):
    edges = (0, DA_WIDTH, 2 * DA_WIDTH, 3 * DA_WIDTH)
    edges = edges + (edges[-1] + 4 * GDN_WIDTH, edges[-1] + 4 * GDN_WIDTH + LANES,
                     edges[-1] + 4 * GDN_WIDTH + LANES + S5_WIDTH)
    wq_ref, wv_ref, wk_ref, wg_ref, wsm_ref, wc_ref = (
        w_ref.at[lo:hi, :] for lo, hi in zip(edges[:-1], edges[1:]))
    if apply_ln:
        h_ref, qT_ref, vT_ref, k_ref, gdn_ref, small_ref, cu_ref = out_refs
        h = _layer_norm(x_ref[...], g_ref[...], b_ref[...])
        h_ref[...] = h
    else:
        qT_ref, vT_ref, k_ref, gdn_ref, small_ref, cu_ref = out_refs
        h = x_ref[...]
    hb = h.astype(BF16)
    qT_ref[0] = (_dot_nt(wq_ref[...], hb) * q_scale).astype(BF16)
    vT_ref[0] = _dot_nt(wv_ref[...], hb).astype(BF16)
    kd = 2 * DA_HEAD_DIM
    k = _dot_nt(hb, wk_ref[...])
    zpad = jnp.zeros((k.shape[0], K_PAD - kd), F32)
    k_wide = jnp.concatenate(
        [p for hh in range(DA_HEADS) for p in (k[:, hh * kd:(hh + 1) * kd], zpad)], axis=1)
    k_ref[0] = (k_wide + kone_ref[...]).astype(BF16)
    gdn_ref[...] = _dot_nt(hb, wg_ref[...])
    small_ref[...] = _dot_nt(hb, wsm_ref[...])
    cu_ref[...] = _dot_nt(hb, wc_ref[...])


def _in_projection(x2d, g, b, wts, *, batch, seq, apply_ln, tm=_Tiles.proj_tm):
    t = batch * seq
    nt = seq // tm
    w_all, kone = wts
    kw = DA_HEADS * K_PAD
    q_scale = (DA_HEAD_DIM ** -0.5) * LOG2E
    row = lambda bi, i: (bi * nt + i, 0)
    const = lambda bi, i: (0, 0)
    out_shape = [
        jax.ShapeDtypeStruct((batch, DA_WIDTH, seq), BF16),
        jax.ShapeDtypeStruct((batch, DA_WIDTH, seq), BF16),
        jax.ShapeDtypeStruct((batch, seq, kw), BF16),
        jax.ShapeDtypeStruct((t, 4 * GDN_WIDTH), F32),
        jax.ShapeDtypeStruct((t, LANES), F32),
        jax.ShapeDtypeStruct((t, S5_WIDTH), F32),
    ]
    out_specs = [
        pl.BlockSpec((1, DA_WIDTH, tm), lambda bi, i: (bi, 0, i)),
        pl.BlockSpec((1, DA_WIDTH, tm), lambda bi, i: (bi, 0, i)),
        pl.BlockSpec((1, tm, kw), lambda bi, i: (bi, i, 0)),
        pl.BlockSpec((tm, 4 * GDN_WIDTH), row),
        pl.BlockSpec((tm, LANES), row),
        pl.BlockSpec((tm, S5_WIDTH), row),
    ]
    if apply_ln:
        out_shape = [jax.ShapeDtypeStruct((t, D_MODEL), F32)] + out_shape
        out_specs = [pl.BlockSpec((tm, D_MODEL), row)] + out_specs
    in_specs = [
        pl.BlockSpec((tm, D_MODEL), row),
        pl.BlockSpec((1, D_MODEL), const),
        pl.BlockSpec((1, D_MODEL), const),
        pl.BlockSpec(w_all.shape, const),
        pl.BlockSpec(kone.shape, const),
    ]
    return pl.pallas_call(
        functools.partial(_proj_kernel, apply_ln=apply_ln, q_scale=q_scale),
        out_shape=out_shape,
        grid=(batch, nt),
        in_specs=in_specs,
        out_specs=out_specs,
        compiler_params=_cparams(("parallel", "parallel")),
        name="in_projection_ln" if apply_ln else "in_projection",
    )(x2d, g.reshape(1, -1), b.reshape(1, -1), w_all, kone)


def _attn_kernel(lam_ref, qT_ref, k_ref, vT_ref, g_ref, o_ref,
                 qbd_ref, pa_ref, pb_ref, m_ref, l_ref, acc_ref, cm_ref, *, tq, tk, out_scale):
    i = pl.program_id(2)
    dh = DA_HEAD_DIM
    ref_row = 2 * dh
    n_diag = 4
    assert tq == n_diag * tk

    def qk(j):
        start = pl.multiple_of(j * tk, tk)
        return _dot(k_ref[0, pl.ds(start, tk), :], qbd_ref[...])

    def pv(j, p):
        start = pl.multiple_of(j * tk, tk)
        return _dot(vT_ref[0, :, pl.ds(start, tk)], p.astype(BF16))

    def init_stats():
        m_ref[...] = jnp.full_like(m_ref, NEG_BIG)
        l_ref[...] = jnp.zeros_like(l_ref)
        acc_ref[...] = jnp.zeros_like(acc_ref)

    def exact_step(j, mask):
        s = qk(j)
        if mask is not None:
            s = jnp.where(mask, s, NEG_BIG)
        m_old = m_ref[...]
        m_new = jnp.maximum(m_old, jnp.max(s, axis=0, keepdims=True))
        alpha = jnp.exp2(m_old - m_new)
        p = jnp.exp2(s - m_new)
        l_ref[...] = alpha * l_ref[...] + jnp.sum(p, axis=0, keepdims=True)
        m_ref[...] = m_new
        acc_ref[...] = alpha * acc_ref[...] + pv(j, p)

    def score_exp(j):
        s = qk(j)
        cm_ref[...] = jnp.maximum(cm_ref[...], jnp.max(s, axis=0, keepdims=True))
        p = jnp.exp2(s)
        l_ref[...] += jnp.sum(p, axis=0, keepdims=True)
        return p.astype(BF16)

    def accumulate(p_ref, j):
        acc_ref[...] += pv(j, p_ref[...])

    kc = lax.broadcasted_iota(jnp.int32, (tk, 1), 0) // CHUNK
    col = lax.broadcasted_iota(jnp.int32, (1, 2 * tq), 1)
    qc = jnp.where(col >= tq, col - tq, col) // CHUNK

    def diag_mask(d):
        return (kc + d * (tk // CHUNK)) <= qc

    qbd_ref[...] = jnp.zeros_like(qbd_ref)
    q = qT_ref[0]
    qbd_ref[0:dh, 0:tq] = q[0:dh]
    qbd_ref[dh:2 * dh, tq:2 * tq] = q[dh:2 * dh]
    init_stats()

    off_diag = i > 0
    t_diag = n_diag * i
    k0 = pl.multiple_of(t_diag * tk, tk)
    s0 = _dot(k_ref[0, pl.ds(k0, CHUNK), :], qbd_ref[...])
    mref = jnp.max(s0, axis=0, keepdims=True).astype(BF16)
    qbd_ref[ref_row:ref_row + BF16_SUBLANES, :] = jnp.broadcast_to(-mref, (BF16_SUBLANES, 2 * tq))
    cm_ref[...] = jnp.zeros_like(cm_ref)

    def visible_columns(d):
        return ((d * tk, tq), (tq + d * tk, 2 * tq))

    def diag_score_exp(p_ref, d):
        start = pl.multiple_of((t_diag + d) * tk, tk)
        kt = k_ref[0, pl.ds(start, tk), :]
        mask = diag_mask(d)
        for lo, hi in visible_columns(d):
            s = jnp.where(mask[:, lo:hi], _dot(kt, qbd_ref[:, lo:hi]), NEG_BIG)
            cm_ref[:, lo:hi] = jnp.maximum(cm_ref[:, lo:hi], jnp.max(s, axis=0, keepdims=True))
            p = jnp.exp2(s)
            l_ref[:, lo:hi] += jnp.sum(p, axis=0, keepdims=True)
            p_ref[:, lo:hi] = p.astype(BF16)

    def diag_accumulate(p_ref, d):
        start = pl.multiple_of((t_diag + d) * tk, tk)
        vt = vT_ref[0, :, pl.ds(start, tk)]
        for lo, hi in visible_columns(d):
            acc_ref[:, lo:hi] += _dot(vt, p_ref[:, lo:hi])

    diag_score_exp(pa_ref, 0)
    diag_score_exp(pb_ref, 1)
    diag_accumulate(pa_ref, 0)
    diag_score_exp(pa_ref, 2)
    diag_accumulate(pb_ref, 1)
    diag_score_exp(pb_ref, 3)
    diag_accumulate(pa_ref, 2)
    diag_accumulate(pb_ref, 3)

    @pl.when(off_diag)
    def _():
        pa_ref[...] = score_exp(0)

        def quad(t, lookahead):
            pb_ref[...] = score_exp(t + 1)
            accumulate(pa_ref, t)
            pa_ref[...] = score_exp(t + 2)
            accumulate(pb_ref, t + 1)
            pb_ref[...] = score_exp(t + 3)
            accumulate(pa_ref, t + 2)
            if lookahead:
                pa_ref[...] = score_exp(t + 4)
            accumulate(pb_ref, t + 3)

        def octet(t, lookahead):
            quad(t, True)
            quad(t + 4, lookahead)

        def octet_body(r, carry):
            octet(8 * r, True)
            return carry

        odd = i % 2
        n_loop = i // 2 - 1 + odd
        lax.fori_loop(0, n_loop, octet_body, 0)
        t = 8 * n_loop

        @pl.when(odd == 1)
        def _():
            quad(t, False)

        @pl.when(odd == 0)
        def _():
            octet(t, False)

    @pl.when(jnp.max(cm_ref[...]) > FAST_MAX_LOG2)
    def _():
        qbd_ref[ref_row:ref_row + BF16_SUBLANES, :] = jnp.zeros((BF16_SUBLANES, 2 * tq), BF16)
        init_stats()

        def body(j, carry):
            exact_step(j, None)
            return carry

        lax.fori_loop(0, t_diag, body, 0)
        for d in range(n_diag):
            exact_step(t_diag + d, diag_mask(d))

    l = l_ref[...]
    acc = acc_ref[...]
    lam = lam_ref[0]
    o = acc[:, 0:tq] / l[:, 0:tq] - lam * (acc[:, tq:2 * tq] / l[:, tq:2 * tq])
    ms = jnp.mean(o * o, axis=0, keepdims=True)
    o_ref[0] = o * lax.rsqrt(ms + RMS_EPS) * g_ref[...] * out_scale


def _diff_attention(lam, qT, k, vT, norm_g, *, lam_init, tq=_Tiles.attn_tq, tk=_Tiles.attn_tk):
    batch, _, seq = qT.shape
    nq = seq // tq
    dv = DA_V_DIM
    return pl.pallas_call(
        functools.partial(_attn_kernel, tq=tq, tk=tk, out_scale=1.0 - lam_init),
        out_shape=jax.ShapeDtypeStruct((batch, DA_WIDTH, seq), F32),
        grid=(batch, DA_HEADS, nq),
        in_specs=[
            pl.BlockSpec(memory_space=pltpu.SMEM),
            pl.BlockSpec((1, dv, tq), lambda b, h, i: (b, h, i)),
            pl.BlockSpec((1, seq, LANES), lambda b, h, i: (b, 0, h)),
            pl.BlockSpec((1, dv, seq), lambda b, h, i: (b, h, 0)),
            pl.BlockSpec((dv, 1), lambda b, h, i: (0, 0)),
        ],
        out_specs=pl.BlockSpec((1, dv, tq), lambda b, h, i: (b, h, i)),
        scratch_shapes=[
            pltpu.VMEM((LANES, 2 * tq), BF16),
            pltpu.VMEM((tk, 2 * tq), BF16),
            pltpu.VMEM((tk, 2 * tq), BF16),
            pltpu.VMEM((1, 2 * tq), F32),
            pltpu.VMEM((1, 2 * tq), F32),
            pltpu.VMEM((dv, 2 * tq), F32),
            pltpu.VMEM((1, 2 * tq), F32),
        ],
        compiler_params=_cparams(("parallel", "parallel", "parallel")),
        name="diff_attention",
    )(lam.reshape(1), qT, k, vT, norm_g.reshape(dv, 1))


def _split_bf16(x):
    hi = x.astype(BF16)
    lo = (x - hi.astype(F32)).astype(BF16)
    return hi, lo


def _mm_bf16(a, b):
    return _dot(a.astype(BF16), b.astype(BF16))


def _gdn_kernel(qkv_ref, gate_ref, small_ref, convw_ref, gl_ref, ng_ref, o_ref,
                xbuf_ref, state_ref, pb16_ref, pf32_ref, prhs_ref, pegl_ref, *, rows):
    step_i = pl.program_id(1)
    dk = GDN_HEAD_DIM
    nch = rows // CHUNK
    halo = SUBLANES
    heads = range(GDN_HEADS)
    items = [(c, hh) for c in range(nch) for hh in heads]
    cur = (step_i + 1) % 2
    nxt = step_i % 2

    @pl.when(step_i == 0)
    def _():
        xbuf_ref[0:halo, :] = jnp.zeros((halo, 3 * GDN_WIDTH), F32)
        state_ref[...] = jnp.zeros_like(state_ref)
        pb16_ref[...] = jnp.zeros_like(pb16_ref)
        pf32_ref[...] = jnp.zeros_like(pf32_ref)
        prhs_ref[...] = jnp.zeros_like(prhs_ref)
        pegl_ref[...] = jnp.zeros_like(pegl_ref)

    qb, kbb, kbf, qdec, kdec, decays, gsilu, rhss, egl = {}, {}, {}, {}, {}, {}, {}, {}, {}
    for n, it in enumerate(items):
        qb[it] = pb16_ref[cur, 0, n]
        kbb[it] = pb16_ref[cur, 1, n]
        kbf[it] = pb16_ref[cur, 2, n]
        qdec[it] = pb16_ref[cur, 3, n]
        kdec[it] = pb16_ref[cur, 4, n]
        decays[it] = pf32_ref[cur, 0, n]
        gsilu[it] = pf32_ref[cur, 1, n]
        rhss[it] = prhs_ref[cur, n]
        egl[it] = pegl_ref[cur, n][0:1, 0:1]

    ri = lax.broadcasted_iota(jnp.int32, (CHUNK, CHUNK), 0)
    ci = lax.broadcasted_iota(jnp.int32, (CHUNK, CHUNK), 1)
    tri = ri >= ci
    strict = ri > ci

    def prepare_block():
        xbuf_ref[halo:halo + rows, :] = qkv_ref[...]
        y = convw_ref[CONV_K - 1:CONV_K, :] * xbuf_ref[halo:halo + rows, :]
        for j in range(CONV_K - 1):
            off = halo - (CONV_K - 1) + j
            y = y + convw_ref[j:j + 1, :] * xbuf_ref[off:off + rows, :]
        xbuf_ref[0:halo, :] = xbuf_ref[rows:rows + halo, :]
        y = y * jax.nn.sigmoid(y)

        small = small_ref[...]
        beta_all = jax.nn.sigmoid(small)
        sp_in = small + gl_ref[1:2, :]
        softplus = jnp.maximum(sp_in, 0.0) + jnp.log(1.0 + jnp.exp(-jnp.abs(sp_in)))
        g_all = gl_ref[0:1, :] * softplus
        tril_f = tri.astype(F32)
        gc_parts = []
        for c in range(nch):
            gch = g_all[c * CHUNK:(c + 1) * CHUNK, :]
            gc_parts.append(jnp.dot(tril_f, gch, preferred_element_type=F32,
                                    precision=lax.Precision.HIGHEST))
        gc_all = jnp.concatenate(gc_parts, axis=0) if nch > 1 else gc_parts[0]
        pad = (-rows) % LANES
        gc_sq = jnp.concatenate([gc_all, jnp.zeros((pad, LANES), F32)], axis=0) if pad else gc_all
        gcT = gc_sq.T

        gate = gate_ref[...]
        for n, it in enumerate(items):
            c, hh = it
            r0 = c * CHUNK
            q = y[r0:r0 + CHUNK, hh * dk:(hh + 1) * dk]
            k = y[r0:r0 + CHUNK, GDN_WIDTH + hh * dk:GDN_WIDTH + (hh + 1) * dk]
            v = y[r0:r0 + CHUNK, 2 * GDN_WIDTH + hh * dk:2 * GDN_WIDTH + (hh + 1) * dk]
            q = q * lax.rsqrt(jnp.sum(q * q, axis=-1, keepdims=True) + RMS_EPS) * (dk ** -0.5)
            k = k * lax.rsqrt(jnp.sum(k * k, axis=-1, keepdims=True) + RMS_EPS)
            beta = beta_all[r0:r0 + CHUNK, BETA_LANE0 + hh:BETA_LANE0 + hh + 1]
            gcol = gc_all[r0:r0 + CHUNK, A_LANE0 + hh:A_LANE0 + hh + 1]
            grow = gcT[A_LANE0 + hh:A_LANE0 + hh + 1, r0:r0 + CHUNK]
            glast = gcT[A_LANE0 + hh:A_LANE0 + hh + 1, r0 + CHUNK - 1:r0 + CHUNK]
            eg = jnp.exp(gcol)
            kb = k * beta
            gt = gate[r0:r0 + CHUNK, hh * dk:(hh + 1) * dk]
            pb16_ref[nxt, 0, n] = q.astype(BF16)
            pb16_ref[nxt, 1, n] = kb.astype(BF16)
            pb16_ref[nxt, 2, n] = k.astype(BF16)
            pb16_ref[nxt, 3, n] = (q * eg).astype(BF16)
            pb16_ref[nxt, 4, n] = (k * jnp.exp(glast - gcol)).astype(BF16)
            pf32_ref[nxt, 0, n] = jnp.where(tri, jnp.exp(jnp.where(tri, gcol - grow, 0.0)), 0.0)
            pf32_ref[nxt, 1, n] = gt * jax.nn.sigmoid(gt)
            prhs_ref[nxt, n] = jnp.concatenate([v * beta, kb * eg], axis=1)
            pegl_ref[nxt, n] = jnp.broadcast_to(jnp.exp(glast), (SUBLANES, LANES))

    ng = ng_ref[...]
    kk = {it: _dot_nt(kbb[it], kbf[it]) for it in items}
    qk = {it: _dot_nt(qb[it], kbf[it]) for it in items}
    lm = {it: jnp.where(strict, kk[it] * decays[it], 0.0) for it in items}
    a_intra = {it: qk[it] * decays[it] for it in items}
    xs = {it: rhss[it] - _mm_bf16(lm[it], rhss[it]) for it in items}
    ps = lm
    for _ in range(5):
        ps = {it: _mm_bf16(ps[it], ps[it]) for it in items}
        xs = {it: xs[it] + _mm_bf16(ps[it], xs[it]) for it in items}

    state = [state_ref[hh] for hh in heads]
    for c in range(nch):
        r0 = c * CHUNK
        stb = [state[hh].astype(BF16) for hh in heads]
        ws = [_dot(xs[(c, hh)][:, dk:2 * dk].astype(BF16), stb[hh]) for hh in heads]
        qst = [_dot(qdec[(c, hh)], stb[hh]) for hh in heads]
        vn = [(xs[(c, hh)][:, 0:dk] - ws[hh]).astype(BF16) for hh in heads]
        av = [_dot(a_intra[(c, hh)].astype(BF16), vn[hh]) for hh in heads]
        kv = [_dot_tn(kdec[(c, hh)], vn[hh]) for hh in heads]
        for hh in heads:
            state[hh] = state[hh] * egl[(c, hh)] + kv[hh]
            o = qst[hh] + av[hh]
            ms = jnp.mean(o * o, axis=-1, keepdims=True)
            on = o * lax.rsqrt(ms + RMS_EPS) * ng
            o_ref[r0:r0 + CHUNK, hh * dk:(hh + 1) * dk] = on * gsilu[(c, hh)]
    for hh in heads:
        state_ref[hh] = state[hh]

    prepare_block()


def _gdn_mixer(gdn_in, small, conv_w, a_log, dt_bias, norm_g, *, batch, seq, rows=_Tiles.gdn_rows):
    t = batch * seq
    ns = seq // rows
    gl = jnp.zeros((SUBLANES, LANES), F32)
    gl = gl.at[0, A_LANE0:A_LANE0 + GDN_HEADS].set(-jnp.exp(a_log.astype(F32)))
    gl = gl.at[1, A_LANE0:A_LANE0 + GDN_HEADS].set(dt_bias.astype(F32))
    convw = jnp.zeros((SUBLANES, 3 * GDN_WIDTH), F32).at[0:CONV_K].set(conv_w.astype(F32))
    nitems = (rows // CHUNK) * GDN_HEADS
    dk = GDN_HEAD_DIM
    rin = lambda b, i: (b * ns + jnp.minimum(i, ns - 1), 0)
    rout = lambda b, i: (b * ns + jnp.maximum(i - 1, 0), 0)
    const = lambda b, i: (0, 0)
    return pl.pallas_call(
        functools.partial(_gdn_kernel, rows=rows),
        out_shape=jax.ShapeDtypeStruct((t, GDN_WIDTH), F32),
        grid=(batch, ns + 1),
        in_specs=[
            pl.BlockSpec((rows, 3 * GDN_WIDTH), rin),
            pl.BlockSpec((rows, GDN_WIDTH), lambda b, i: (b * ns + jnp.minimum(i, ns - 1), 3)),
            pl.BlockSpec((rows, LANES), rin),
            pl.BlockSpec((SUBLANES, 3 * GDN_WIDTH), const),
            pl.BlockSpec((SUBLANES, LANES), const),
            pl.BlockSpec((1, GDN_HEAD_DIM), const),
        ],
        out_specs=pl.BlockSpec((rows, GDN_WIDTH), rout),
        scratch_shapes=[
            pltpu.VMEM((rows + SUBLANES, 3 * GDN_WIDTH), F32),
            pltpu.VMEM((GDN_HEADS, dk, dk), F32),
            pltpu.VMEM((2, 5, nitems, CHUNK, dk), BF16),
            pltpu.VMEM((2, 2, nitems, CHUNK, dk), F32),
            pltpu.VMEM((2, nitems, CHUNK, 2 * dk), F32),
            pltpu.VMEM((2, nitems, SUBLANES, LANES), F32),
        ],
        compiler_params=_cparams(("parallel", "arbitrary")),
        name="gated_deltanet",
    )(gdn_in, gdn_in, small, convw, gl, norm_g.reshape(1, -1).astype(F32))


def _cmul(ar, ai, br, bi):
    return ar * br - ai * bi, ar * bi + ai * br


def _s5_kernel(u_ref, bblk_ref, ccat_ref, apow_ref, d_ref, wglu_ref, o_ref,
               x_ref, usc_ref, carry_ref, *, tm):
    n = S5_LANES
    step_i = pl.program_id(1)
    cur = (step_i + 1) % 2
    nxt = step_i % 2

    @pl.when(step_i == 0)
    def _():
        carry_ref[...] = jnp.zeros_like(carry_ref)
        x_ref[...] = jnp.zeros_like(x_ref)
        usc_ref[...] = jnp.zeros_like(usc_ref)

    x_prev = x_ref[cur]
    u_prev = usc_ref[cur]

    u = u_ref[...]
    usc_ref[nxt] = u
    bu = _dot(u.astype(BF16), bblk_ref[...])
    c_re = carry_ref[0:1, :]
    c_im = carry_ref[1:2, :]
    for gidx in range(tm // SUBLANES):
        r0 = gidx * SUBLANES
        x_re = bu[r0:r0 + SUBLANES, 0:n]
        x_im = bu[r0:r0 + SUBLANES, n:2 * n]
        for lvl, d in enumerate((1, 2, 4)):
            a_re = apow_ref[lvl * 2 * SUBLANES:lvl * 2 * SUBLANES + SUBLANES, :]
            a_im = apow_ref[lvl * 2 * SUBLANES + SUBLANES:(lvl + 1) * 2 * SUBLANES, :]
            s_re = pltpu.roll(x_re, d, 0)
            s_im = pltpu.roll(x_im, d, 0)
            t_re, t_im = _cmul(a_re, a_im, s_re, s_im)
            x_re = x_re + t_re
            x_im = x_im + t_im
        p_re = apow_ref[6 * SUBLANES:7 * SUBLANES, :]
        p_im = apow_ref[7 * SUBLANES:8 * SUBLANES, :]
        t_re, t_im = _cmul(p_re, p_im, c_re, c_im)
        x_re = x_re + t_re
        x_im = x_im + t_im
        x_ref[nxt, r0:r0 + SUBLANES, 0:n] = x_re
        x_ref[nxt, r0:r0 + SUBLANES, n:2 * n] = x_im
        c_re = x_re[SUBLANES - 1:SUBLANES, :]
        c_im = x_im[SUBLANES - 1:SUBLANES, :]
    carry_ref[0:1, :] = c_re
    carry_ref[1:2, :] = c_im

    yv = _dot(x_prev.astype(BF16), ccat_ref[...]) + d_ref[...] * u_prev
    yv = 0.5 * yv * (1.0 + jnp.tanh(0.7978845608028654 * (yv + 0.044715 * (yv * yv * yv))))
    z = _dot(yv.astype(BF16), wglu_ref[...])
    o_ref[...] = yv * jax.nn.sigmoid(z)


def _s5_params(lam_re, lam_im, log_dt, b_re, b_im, c_re, c_im):
    f32 = F32
    lre, lim = lam_re.astype(f32), lam_im.astype(f32)
    dt = jnp.exp(log_dt.astype(f32))[:, None]
    mag = jnp.exp(lre * dt)
    ab_re, ab_im = mag * jnp.cos(lim * dt), mag * jnp.sin(lim * dt)
    num_re, num_im = ab_re - 1.0, ab_im
    den = lre * lre + lim * lim
    coef_re = (num_re * lre + num_im * lim) / den
    coef_im = (num_im * lre - num_re * lim) / den
    br, bi = b_re.astype(f32), b_im.astype(f32)
    bb_re = coef_re[..., None] * br - coef_im[..., None] * bi
    bb_im = coef_re[..., None] * bi + coef_im[..., None] * br
    eye = jnp.eye(S5_GROUPS, dtype=f32)
    blk_re = jnp.einsum('gph,gk->ghkp', bb_re, eye).reshape(S5_WIDTH, S5_LANES)
    blk_im = jnp.einsum('gph,gk->ghkp', bb_im, eye).reshape(S5_WIDTH, S5_LANES)
    bblk = jnp.concatenate([blk_re, blk_im], axis=1).astype(BF16)
    cb_re = jnp.einsum('ghp,gk->gpkh', c_re.astype(f32), eye).reshape(S5_LANES, S5_WIDTH)
    cb_im = jnp.einsum('ghp,gk->gpkh', c_im.astype(f32), eye).reshape(S5_LANES, S5_WIDTH)
    ccat = jnp.concatenate([cb_re, -cb_im], axis=0).astype(BF16)
    a1 = (ab_re.reshape(1, -1), ab_im.reshape(1, -1))
    pows = [a1]
    for _ in range(SUBLANES - 1):
        pows.append(_cmul(pows[-1][0], pows[-1][1], a1[0], a1[1]))
    rid = jnp.arange(SUBLANES)[:, None]
    rows = []
    for d in (1, 2, 4):
        mask = (rid >= d).astype(f32)
        rows.append(mask * pows[d - 1][0])
        rows.append(mask * pows[d - 1][1])
    rows.append(jnp.concatenate([pows[r][0] for r in range(SUBLANES)], axis=0))
    rows.append(jnp.concatenate([pows[r][1] for r in range(SUBLANES)], axis=0))
    apow = jnp.concatenate(rows, axis=0)
    return bblk, ccat, apow


def _s5_mixer(cu, lam_re, lam_im, log_dt, b_re, b_im, c_re, c_im, d, w_glu, *, batch, seq,
              tm=_Tiles.s5_tm):
    t = batch * seq
    ns = seq // tm
    bblk, ccat, apow = _s5_params(lam_re, lam_im, log_dt, b_re, b_im, c_re, c_im)
    rin = lambda b, i: (b * ns + jnp.minimum(i, ns - 1), 0)
    rout = lambda b, i: (b * ns + jnp.maximum(i - 1, 0), 0)
    const = lambda b, i: (0, 0)
    return pl.pallas_call(
        functools.partial(_s5_kernel, tm=tm),
        out_shape=jax.ShapeDtypeStruct((t, S5_WIDTH), F32),
        grid=(batch, ns + 1),
        in_specs=[
            pl.BlockSpec((tm, S5_WIDTH), rin),
            pl.BlockSpec(bblk.shape, const),
            pl.BlockSpec(ccat.shape, const),
            pl.BlockSpec(apow.shape, const),
            pl.BlockSpec((1, S5_WIDTH), const),
            pl.BlockSpec((S5_WIDTH, S5_WIDTH), const),
        ],
        out_specs=pl.BlockSpec((tm, S5_WIDTH), rout),
        scratch_shapes=[
            pltpu.VMEM((2, tm, 2 * S5_LANES), F32),
            pltpu.VMEM((2, tm, S5_WIDTH), F32),
            pltpu.VMEM((SUBLANES, S5_LANES), F32),
        ],
        compiler_params=_cparams(("parallel", "arbitrary")),
        name="s5_mixer",
    )(cu, bblk, ccat, apow, d.reshape(1, -1).astype(F32), w_glu.astype(BF16))


def _route_rows(lt, n_tok):
    g = [lt[r:r + 1, :] for r in range(N_EXPERT_GROUPS)]
    gm = functools.reduce(jnp.maximum, g)
    gsum = functools.reduce(lambda a, b: a + b, [jnp.exp(x - gm) for x in g])
    g_p = 1.0 / gsum
    taken = jnp.zeros_like(gm) > 1.0
    g_hot = []
    for x in g:
        hit = jnp.logical_and(x == gm, jnp.logical_not(taken))
        g_hot.append(hit)
        taken = jnp.logical_or(taken, hit)
    e_sel = []
    for j in range(EXPERTS_PER_GROUP):
        acc = jnp.zeros_like(gm)
        for gi in range(N_EXPERT_GROUPS):
            r = ROUTER_EXPERT_ROW0 + gi * EXPERTS_PER_GROUP + j
            acc = acc + jnp.where(g_hot[gi], lt[r:r + 1, :], 0.0)
        e_sel.append(acc)
    m1 = functools.reduce(jnp.maximum, e_sel)
    taken = jnp.zeros_like(gm) > 1.0
    hot1 = []
    for x in e_sel:
        hit = jnp.logical_and(x == m1, jnp.logical_not(taken))
        hot1.append(hit)
        taken = jnp.logical_or(taken, hit)
    rest = [jnp.where(hh, NEG_BIG, x) for hh, x in zip(hot1, e_sel)]
    m2 = functools.reduce(jnp.maximum, rest)
    taken = jnp.zeros_like(gm) > 1.0
    hot2 = []
    for hh, x in zip(hot1, rest):
        hit = jnp.logical_and(jnp.logical_and(x == m2, jnp.logical_not(hh)), jnp.logical_not(taken))
        hot2.append(hit)
        taken = jnp.logical_or(taken, hit)
    e2 = jnp.exp(m2 - m1)
    w1 = g_p / (1.0 + e2)
    w2 = g_p * e2 / (1.0 + e2)
    rows = []
    for gi in range(N_EXPERT_GROUPS):
        for j in range(EXPERTS_PER_GROUP):
            val = jnp.where(hot1[j], w1, 0.0) + jnp.where(hot2[j], w2, 0.0)
            rows.append(jnp.where(g_hot[gi], val, 0.0))
    return jnp.concatenate(rows, axis=0)


def _outproj_kernel(h_ref, yaT_ref, yb_ref, yc_ref, wa_ref, wb_ref, wc_ref, g_ref, b_ref,
                    wrT_ref, br_ref, h1_ref, h1b_ref, comb_ref, *, tm):
    ya = yaT_ref[0].T
    mix = _dot(ya.astype(BF16), wa_ref[...])
    mix = mix + _dot(yb_ref[...].astype(BF16), wb_ref[...])
    mix = mix + _dot(yc_ref[...].astype(BF16), wc_ref[...])
    h1 = _layer_norm(ALPHA * h_ref[...] + mix, g_ref[...], b_ref[...])
    h1_ref[...] = h1
    h1b_ref[...] = h1.astype(BF16)
    h_hi, h_lo = _split_bf16(h1)
    w_hi = wrT_ref[0:LANES, :]
    w_lo = wrT_ref[LANES:2 * LANES, :]
    lt = _dot_nt(w_hi, h_hi) + _dot_nt(w_hi, h_lo) + _dot_nt(w_lo, h_hi) + br_ref[...]
    comb = _route_rows(lt, tm)
    combp = jnp.concatenate([comb, jnp.zeros((LANES - N_EXPERTS, tm), F32)], axis=0)
    comb_ref[...] = combp.T


def _out_projection(h, yaT, yb, yc, w_out, ln_g, ln_b, w_grp, b_grp, w_exp, b_exp,
                    *, batch, seq, tm=_Tiles.out_tm):
    t = batch * seq
    nt = seq // tm
    wa = w_out[0:DA_WIDTH].astype(BF16)
    wb = w_out[DA_WIDTH:DA_WIDTH + GDN_WIDTH].astype(BF16)
    wc = w_out[DA_WIDTH + GDN_WIDTH:].astype(BF16)
    wr = jnp.zeros((D_MODEL, LANES), F32)
    e0 = ROUTER_EXPERT_ROW0
    wr = wr.at[:, 0:N_EXPERT_GROUPS].set(w_grp.astype(F32)).at[:, e0:e0 + N_EXPERTS].set(w_exp.astype(F32))
    wrT = wr.T
    wr_hi = wrT.astype(BF16)
    wr_lo = (wrT - wr_hi.astype(F32)).astype(BF16)
    wr_cat = jnp.concatenate([wr_hi, wr_lo], axis=0)
    br = jnp.zeros((LANES, 1), F32)
    br = br.at[0:N_EXPERT_GROUPS, 0].set(b_grp.astype(F32)).at[e0:e0 + N_EXPERTS, 0].set(b_exp.astype(F32))
    row = lambda b, i: (b * nt + i, 0)
    const = lambda b, i: (0, 0)
    return pl.pallas_call(
        functools.partial(_outproj_kernel, tm=tm),
        out_shape=[
            jax.ShapeDtypeStruct((t, D_MODEL), F32),
            jax.ShapeDtypeStruct((t, D_MODEL), BF16),
            jax.ShapeDtypeStruct((t, LANES), F32),
        ],
        grid=(batch, nt),
        in_specs=[
            pl.BlockSpec((tm, D_MODEL), row),
            pl.BlockSpec((1, DA_WIDTH, tm), lambda b, i: (b, 0, i)),
            pl.BlockSpec((tm, GDN_WIDTH), row),
            pl.BlockSpec((tm, S5_WIDTH), row),
            pl.BlockSpec(wa.shape, const),
            pl.BlockSpec(wb.shape, const),
            pl.BlockSpec(wc.shape, const),
            pl.BlockSpec((1, D_MODEL), const),
            pl.BlockSpec((1, D_MODEL), const),
            pl.BlockSpec(wr_cat.shape, const),
            pl.BlockSpec((LANES, 1), const),
        ],
        out_specs=[
            pl.BlockSpec((tm, D_MODEL), row),
            pl.BlockSpec((tm, D_MODEL), row),
            pl.BlockSpec((tm, LANES), row),
        ],
        compiler_params=_cparams(("parallel", "parallel")),
        name="out_projection_router",
    )(h, yaT, yb, yc, wa, wb, wc, ln_g.reshape(1, -1), ln_b.reshape(1, -1), wr_cat, br)


MOE_EXPERTS_PER_STEP = _Tiles.moe_experts


def _moe_kernel(hb_ref, h1_ref, comb_ref, w1_ref, w3_ref, w2_ref, g_ref, b_ref, o_ref, acc_ref):
    s = pl.program_id(1)
    eps = MOE_EXPERTS_PER_STEP

    @pl.when(s == 0)
    def _():
        acc_ref[...] = jnp.zeros_like(acc_ref)

    x = hb_ref[...]
    lane = lax.broadcasted_iota(jnp.int32, (1, LANES), 1)
    comb = comb_ref[...]
    ups = [(_dot(x, w1_ref[j]), _dot(x, w3_ref[j])) for j in range(eps)]
    y = None
    for j, (a, b) in enumerate(ups):
        c = jnp.sum(jnp.where(lane == s * eps + j, comb, 0.0), axis=1, keepdims=True)
        hid = (a * jax.nn.sigmoid(a) * b * c).astype(BF16)
        part = _dot(hid, w2_ref[j])
        y = part if y is None else y + part
    acc_ref[...] += y

    @pl.when(s == N_EXPERTS // eps - 1)
    def _():
        o_ref[...] = _layer_norm(ALPHA * h1_ref[...] + acc_ref[...], g_ref[...], b_ref[...])


def _moe(h1, h1b, comb, w1, w3, w2, ln_g, ln_b, *, tm=_Tiles.moe_tm):
    t = h1.shape[0]
    nt = t // tm
    eps = MOE_EXPERTS_PER_STEP
    row = lambda i, e: (i, 0)
    const = lambda i, e: (0, 0)
    return pl.pallas_call(
        _moe_kernel,
        out_shape=jax.ShapeDtypeStruct((t, D_MODEL), F32),
        grid=(nt, N_EXPERTS // eps),
        in_specs=[
            pl.BlockSpec((tm, D_MODEL), row),
            pl.BlockSpec((tm, D_MODEL), row),
            pl.BlockSpec((tm, LANES), row),
            pl.BlockSpec((eps, D_MODEL, D_EXPERT), lambda i, e: (e, 0, 0)),
            pl.BlockSpec((eps, D_MODEL, D_EXPERT), lambda i, e: (e, 0, 0)),
            pl.BlockSpec((eps, D_EXPERT, D_MODEL), lambda i, e: (e, 0, 0)),
            pl.BlockSpec((1, D_MODEL), const),
            pl.BlockSpec((1, D_MODEL), const),
        ],
        out_specs=pl.BlockSpec((tm, D_MODEL), row),
        scratch_shapes=[pltpu.VMEM((tm, D_MODEL), F32)],
        compiler_params=_cparams(("parallel", "arbitrary")),
        name="moe_ffn",
    )(h1b, h1, comb, w1, w3, w2, ln_g.reshape(1, -1), ln_b.reshape(1, -1))


def _split_w_in(w):
    wt = w.T
    o = 0
    wq = wt[o:o + DA_WIDTH]; o += DA_WIDTH
    wk = wt[o:o + DA_WIDTH]; o += DA_WIDTH
    wv = wt[o:o + DA_WIDTH]; o += DA_WIDTH
    wg = wt[o:o + 4 * GDN_WIDTH]; o += 4 * GDN_WIDTH
    wbeta = wt[o:o + GDN_HEADS]; o += GDN_HEADS
    wa = wt[o:o + GDN_HEADS]; o += GDN_HEADS
    wc = wt[o:o + S5_WIDTH]
    kd = 2 * DA_HEAD_DIM
    zrows = lambda n: jnp.zeros((n, D_MODEL), w.dtype)
    wk_pad = [wk]
    wsm = [zrows(BETA_LANE0), wbeta, zrows(A_LANE0 - BETA_LANE0 - GDN_HEADS), wa,
           zrows(LANES - A_LANE0 - GDN_HEADS)]
    kone = jnp.zeros((1, DA_HEADS, K_PAD), F32).at[:, :, kd].set(1.0).reshape(1, DA_HEADS * K_PAD)
    w_all = jnp.concatenate([wq, wv] + wk_pad + [wg] + wsm + [wc], axis=0).astype(BF16)
    return w_all, kone


def kernel(x, ln_in_g, ln_in_b, w_in, w_out, lam_q1, lam_k1, lam_q2, lam_k2, diff_norm_g, dn_conv_w, dn_a_log, dn_dt_bias, dn_norm_g, s5_lambda_re, s5_lambda_im, s5_log_dt, s5_b_re, s5_b_im, s5_c_re, s5_c_im, s5_d, s5_w_glu, ln1_g, ln1_b, moe_w_grp, moe_b_grp, moe_w_exp, moe_b_exp, moe_w1, moe_w3, moe_w2, ln2_g, ln2_b):
    batch, seq, d = x.shape
    h = x.reshape(batch * seq, d)
    for l in range(DEPTH):
        lam_init = 0.8 - 0.6 * math.exp(-0.3 * l)
        wts = _split_w_in(w_in[l])
        outs = _in_projection(h, ln_in_g, ln_in_b, wts, batch=batch, seq=seq, apply_ln=(l == 0))
        if l == 0:
            h, qT, vT, k, gdn_in, small, cu = outs
        else:
            qT, vT, k, gdn_in, small, cu = outs
        lam = (jnp.exp(jnp.sum(lam_q1[l] * lam_k1[l])) - jnp.exp(jnp.sum(lam_q2[l] * lam_k2[l]))
               ).astype(F32) + lam_init
        yaT = _diff_attention(lam, qT, k, vT, diff_norm_g[l].astype(F32), lam_init=lam_init)
        yb = _gdn_mixer(gdn_in, small, dn_conv_w[l], dn_a_log[l], dn_dt_bias[l], dn_norm_g[l],
                        batch=batch, seq=seq)
        yc = _s5_mixer(cu, s5_lambda_re[l], s5_lambda_im[l], s5_log_dt[l], s5_b_re[l], s5_b_im[l],
                       s5_c_re[l], s5_c_im[l], s5_d[l], s5_w_glu[l], batch=batch, seq=seq)
        h1, h1b, comb = _out_projection(h, yaT, yb, yc, w_out[l], ln1_g[l], ln1_b[l],
                                        moe_w_grp[l], moe_b_grp[l], moe_w_exp[l], moe_b_exp[l],
                                        batch=batch, seq=seq)
        h = _moe(h1, h1b, comb, moe_w1[l].astype(BF16), moe_w3[l].astype(BF16),
                 moe_w2[l].astype(BF16), ln2_g[l], ln2_b[l])
    return h.reshape(batch, seq, d)
```
